```python
import jax, jax.numpy as jnp
from jax import lax
import numpy as np

D_MODEL = 1024
BATCH = 2
SEQ = 16384
DEPTH = 2
DEC_BATCH = 32
DEC_SEQ = 2048
PAST_LEN = 128

GRID_W = 64
EPS = 1e-6
GLA_HEADS = 4
GLA_DK = 128
GLA_DV = 256
GLA_QK_W = GLA_HEADS * GLA_DK
GLA_V_W = GLA_HEADS * GLA_DV
GLA_CHUNK = 64
GLA_DECAY_RANK = 16
GLA_GATE_TAU = 16.0
NA_HEADS = 8
NA_HEAD_DIM = 64
NA_W = NA_HEADS * NA_HEAD_DIM
NA_KR_MAX = 8
NA_KC = 16
NA_QBLOCK = 16
NA_KBLOCK = NA_QBLOCK + NA_KC
D_IN_PROJ = 2 * GLA_QK_W + 2 * GLA_V_W + 2 * GLA_DECAY_RANK + 3 * NA_W + 2 * D_MODEL
D_FF = 2816
N_EXPERTS = 8
TOP_K = 2
D_FF_EXPERT = 1792
N_DENSE = (DEPTH + 1) // 2
N_MOE = DEPTH // 2

kernel_name = 'hybrid_gla_natten_moe_encoder'


def rmsnorm(x, g):
    xf = x.astype(jnp.float32)
    y = xf * lax.rsqrt(jnp.mean(xf * xf, axis=-1, keepdims=True) + EPS) * g.astype(jnp.float32)
    return y.astype(x.dtype)


def _to_chunks(t):
    B, L, H, d = t.shape
    return t.reshape(B, L // GLA_CHUNK, GLA_CHUNK, H, d).transpose(1, 0, 3, 2, 4)


def gla_one_direction(q, k, v, log_a):
    B, L, H, dk = q.shape
    dv = v.shape[-1]
    lower = jnp.tril(jnp.ones((GLA_CHUNK, GLA_CHUNK), dtype=bool))

    def step(S, inp):
        qi, ki, vi, ai = inp
        b = jnp.cumsum(ai, axis=2)
        b_last = b[:, :, -1:, :]
        inter = jnp.einsum('bhcd,bhde->bhce', qi * jnp.exp(b), S)
        diff = b[:, :, :, None, :] - b[:, :, None, :, :]
        decay = jnp.exp(jnp.where(lower[None, None, :, :, None], diff, -jnp.inf))
        scores = jnp.einsum('bhid,bhjd,bhijd->bhij', qi, ki, decay)
        intra = jnp.einsum('bhij,bhje->bhie', scores, vi)
        S_new = jnp.exp(b_last[:, :, 0, :])[..., None] * S + jnp.einsum('bhcd,bhce->bhde', ki * jnp.exp(b_last - b), vi)
        return S_new, inter + intra

    S0 = jnp.zeros((B, H, dk, dv), jnp.float32)
    _, o = lax.scan(step, S0, (_to_chunks(q), _to_chunks(k), _to_chunks(v), _to_chunks(log_a)))
    return o.transpose(1, 0, 3, 2, 4).reshape(B, L, H, dv)


def bidirectional_gla(q, k, v, la_f, la_b):
    fwd = gla_one_direction(q, k, v, la_f)
    bwd = gla_one_direction(jnp.flip(q, 1), jnp.flip(k, 1), jnp.flip(v, 1), jnp.flip(la_b, 1))
    return fwd + jnp.flip(bwd, 1)


def neighborhood_attention(q, k, v, rpb):
    B, L, H, dh = q.shape
    rows = L // GRID_W
    kr = min(NA_KR_MAX, rows)
    ncb = GRID_W // NA_QBLOCK
    scale = dh ** -0.5
    qg = q.reshape(B, rows, ncb, NA_QBLOCK, H, dh)
    qcol = np.arange(GRID_W).reshape(ncb, NA_QBLOCK)
    col_start = np.clip(qcol - NA_KC // 2, 0, GRID_W - NA_KC)
    kblk_start = np.clip(np.arange(ncb) * NA_QBLOCK - NA_KC // 2, 0, GRID_W - NA_KBLOCK)
    kcol = kblk_start[:, None] + np.arange(NA_KBLOCK)[None, :]
    col_valid = (kcol[:, None, :] >= col_start[..., None]) & (kcol[:, None, :] < col_start[..., None] + NA_KC)
    dc_idx = np.clip(kcol[:, None, :] - qcol[..., None], -(NA_KC - 1), NA_KC - 1) + NA_KC - 1
    kcols = k.reshape(B, rows, GRID_W, H, dh)[:, :, kcol]
    vcols = v.reshape(B, rows, GRID_W, H, dh)[:, :, kcol]

    def row_block(r):
        rs = jnp.clip(r - kr // 2, 0, rows - kr)
        kb = lax.dynamic_slice_in_dim(kcols, rs, kr, axis=1)
        vb = lax.dynamic_slice_in_dim(vcols, rs, kr, axis=1)
        qr = lax.dynamic_index_in_dim(qg, r, axis=1, keepdims=False)
        dr_idx = rs + jnp.arange(kr) - r + NA_KR_MAX - 1
        bias = rpb[:, dr_idx[None, None, :, None], dc_idx[:, :, None, :]]
        s = jnp.einsum('bmqhd,bkmchd->bhmqkc', qr, kb).astype(jnp.float32) * scale + bias[None].astype(jnp.float32)
        s = jnp.where(col_valid[None, None, :, :, None, :], s, -jnp.inf)
        p = jax.nn.softmax(s.reshape(B, H, ncb, NA_QBLOCK, kr * NA_KBLOCK), axis=-1)
        p = p.reshape(B, H, ncb, NA_QBLOCK, kr, NA_KBLOCK).astype(v.dtype)
        o = jnp.einsum('bhmqkc,bkmchd->bmqhd', p, vb)
        return o.reshape(B, GRID_W, H * dh)

    out = lax.map(row_block, jnp.arange(rows))
    return out.transpose(1, 0, 2, 3).reshape(B, L, H * dh)


def token_mixer(h, w_in, w_decay_f, b_decay_f, w_decay_b, b_decay_b, gla_out_norm,
                q_norm, k_norm, rpb, w_o_gla, w_o_na, w_out):
    B, L, _ = h.shape
    f32 = jnp.float32
    sizes = [GLA_QK_W, GLA_QK_W, GLA_V_W, GLA_V_W, GLA_DECAY_RANK, GLA_DECAY_RANK,
             NA_W, NA_W, NA_W, D_MODEL, D_MODEL]
    cuts = [int(c) for c in np.cumsum(sizes)[:-1]]
    proj = h @ w_in
    qa, ka, va, ra, zf, zb, qn_, kn_, vn_, ga, gb = jnp.split(proj, cuts, axis=-1)
    qa = qa.reshape(B, L, GLA_HEADS, GLA_DK).astype(f32) * (GLA_DK ** -0.5)
    ka = ka.reshape(B, L, GLA_HEADS, GLA_DK).astype(f32)
    va = va.reshape(B, L, GLA_HEADS, GLA_DV).astype(f32)
    la_f = (jax.nn.log_sigmoid((zf @ w_decay_f + b_decay_f).astype(f32)) / GLA_GATE_TAU).reshape(B, L, GLA_HEADS, GLA_DK)
    la_b = (jax.nn.log_sigmoid((zb @ w_decay_b + b_decay_b).astype(f32)) / GLA_GATE_TAU).reshape(B, L, GLA_HEADS, GLA_DK)
    o = rmsnorm(bidirectional_gla(qa, ka, va, la_f, la_b), gla_out_norm).astype(h.dtype)
    o = o * jax.nn.silu(ra).reshape(B, L, GLA_HEADS, GLA_DV)
    y_a = o.reshape(B, L, GLA_V_W) @ w_o_gla
    qn_ = rmsnorm(qn_.reshape(B, L, NA_HEADS, NA_HEAD_DIM), q_norm)
    kn_ = rmsnorm(kn_.reshape(B, L, NA_HEADS, NA_HEAD_DIM), k_norm)
    vn_ = vn_.reshape(B, L, NA_HEADS, NA_HEAD_DIM)
    y_b = neighborhood_attention(qn_, kn_, vn_, rpb) @ w_o_na
    merged = jax.nn.sigmoid(ga) * y_a + jax.nn.sigmoid(gb) * y_b
    return merged @ w_out


def swiglu(h, wg, wu, wd):
    return (jax.nn.silu(h @ wg) * (h @ wu)) @ wd


def moe_swiglu(h, router, wg, wu, wd):
    B, L, D = h.shape
    hf = h.reshape(B * L, D)
    logits = (hf @ router).astype(jnp.float32)
    vals, idx = lax.top_k(logits, TOP_K)
    w = jax.nn.softmax(vals, axis=-1)
    combine = jnp.sum(jax.nn.one_hot(idx, N_EXPERTS, dtype=jnp.float32) * w[..., None], axis=1).astype(h.dtype)
    y = jnp.zeros_like(hf)
    for e in range(N_EXPERTS):
        y = y + combine[:, e, None] * swiglu(hf, wg[e], wu[e], wd[e])
    return y.reshape(B, L, D)


def trunk(x, norm_mix, w_in, w_decay_f, b_decay_f, w_decay_b, b_decay_b, gla_out_norm,
          q_norm, k_norm, rpb, w_o_gla, w_o_na, w_out, norm_ffn,
          ffn_w_gate, ffn_w_up, ffn_w_down, moe_router, moe_w_gate, moe_w_up, moe_w_down):
    for l in range(DEPTH):
        h = rmsnorm(x, norm_mix[l])
        x = x + token_mixer(h, w_in[l], w_decay_f[l], b_decay_f[l], w_decay_b[l], b_decay_b[l],
                            gla_out_norm[l], q_norm[l], k_norm[l], rpb[l], w_o_gla[l], w_o_na[l], w_out[l])
        h = rmsnorm(x, norm_ffn[l])
        if l % 2 == 0:
            j = l // 2
            x = x + swiglu(h, ffn_w_gate[j], ffn_w_up[j], ffn_w_down[j])
        else:
            j = l // 2
            x = x + moe_swiglu(h, moe_router[j], moe_w_gate[j], moe_w_up[j], moe_w_down[j])
    return x


def setup_inputs(seed: int = 0) -> dict:
    key = jax.random.key(seed)
    ks = jax.random.split(key, 26)
    f32 = jnp.float32

    def nrm(k, shape, scale):
        return jax.random.normal(k, shape, f32) * scale

    return {
        'x_prompt': nrm(ks[0], (BATCH, SEQ, D_MODEL), 1.0),
        'x_sample': nrm(ks[1], (DEC_BATCH, DEC_SEQ, D_MODEL), 1.0),
        'norm_mix': 1.0 + nrm(ks[2], (DEPTH, D_MODEL), 0.02),
        'w_in': nrm(ks[3], (DEPTH, D_MODEL, D_IN_PROJ), D_MODEL ** -0.5),
        'w_decay_f': nrm(ks[4], (DEPTH, GLA_DECAY_RANK, GLA_QK_W), GLA_DECAY_RANK ** -0.5),
        'b_decay_f': 2.0 + nrm(ks[5], (DEPTH, GLA_QK_W), 0.5),
        'w_decay_b': nrm(ks[6], (DEPTH, GLA_DECAY_RANK, GLA_QK_W), GLA_DECAY_RANK ** -0.5),
        'b_decay_b': 2.0 + nrm(ks[7], (DEPTH, GLA_QK_W), 0.5),
        'gla_out_norm': 1.0 + nrm(ks[8], (DEPTH, GLA_HEADS, GLA_DV), 0.02),
        'q_norm': 1.0 + nrm(ks[9], (DEPTH, NA_HEAD_DIM), 0.02),
        'k_norm': 1.0 + nrm(ks[10], (DEPTH, NA_HEAD_DIM), 0.02),
        'rpb': nrm(ks[11], (DEPTH, NA_HEADS, 2 * NA_KR_MAX - 1, 2 * NA_KC - 1), 0.1),
        'w_o_gla': nrm(ks[12], (DEPTH, GLA_V_W, D_MODEL), GLA_V_W ** -0.5),
        'w_o_na': nrm(ks[13], (DEPTH, NA_W, D_MODEL), NA_W ** -0.5),
        'w_out': nrm(ks[14], (DEPTH, D_MODEL, D_MODEL), D_MODEL ** -0.5),
        'norm_ffn': 1.0 + nrm(ks[15], (DEPTH, D_MODEL), 0.02),
        'ffn_w_gate': nrm(ks[16], (N_DENSE, D_MODEL, D_FF), D_MODEL ** -0.5),
        'ffn_w_up': nrm(ks[17], (N_DENSE, D_MODEL, D_FF), D_MODEL ** -0.5),
        'ffn_w_down': nrm(ks[18], (N_DENSE, D_FF, D_MODEL), D_FF ** -0.5),
        'moe_router': nrm(ks[19], (N_MOE, D_MODEL, N_EXPERTS), D_MODEL ** -0.5),
        'moe_w_gate': nrm(ks[20], (N_MOE, N_EXPERTS, D_MODEL, D_FF_EXPERT), D_MODEL ** -0.5),
        'moe_w_up': nrm(ks[21], (N_MOE, N_EXPERTS, D_MODEL, D_FF_EXPERT), D_MODEL ** -0.5),
        'moe_w_down': nrm(ks[22], (N_MOE, N_EXPERTS, D_FF_EXPERT, D_MODEL), D_FF_EXPERT ** -0.5),
    }


def reference(x_prompt, x_sample, norm_mix, w_in, w_decay_f, b_decay_f, w_decay_b, b_decay_b,
              gla_out_norm, q_norm, k_norm, rpb, w_o_gla, w_o_na, w_out, norm_ffn,
              ffn_w_gate, ffn_w_up, ffn_w_down, moe_router, moe_w_gate, moe_w_up, moe_w_down):
    y_prompt = trunk(x_prompt, norm_mix, w_in, w_decay_f, b_decay_f, w_decay_b, b_decay_b, gla_out_norm,
                     q_norm, k_norm, rpb, w_o_gla, w_o_na, w_out, norm_ffn,
                     ffn_w_gate, ffn_w_up, ffn_w_down, moe_router, moe_w_gate, moe_w_up, moe_w_down)
    y_sample = trunk(x_sample, norm_mix, w_in, w_decay_f, b_decay_f, w_decay_b, b_decay_b, gla_out_norm,
                     q_norm, k_norm, rpb, w_o_gla, w_o_na, w_out, norm_ffn,
                     ffn_w_gate, ffn_w_up, ffn_w_down, moe_router, moe_w_gate, moe_w_up, moe_w_down)
    return (y_prompt, y_sample)
```

```python
import functools

import numpy as np
import jax
import jax.numpy as jnp
from jax import lax
from jax.experimental import pallas as pl
from jax.experimental.pallas import tpu as pltpu

F32 = jnp.float32
BF16 = jnp.bfloat16

EPS = 1e-6
GRID_W = 64
GLA_HEADS = 4
GLA_DK = 128
GLA_DV = 256
GLA_CHUNK = 64
GLA_RANK = 16
GLA_TAU = 16.0
NA_HEADS = 8
NA_DH = 64
NA_KR = 8
NA_KC = 16
N_EXPERTS = 8
NEG = -1e30

VMEM_LIMIT = 56 * 1024 * 1024

QK_W = GLA_HEADS * GLA_DK
V_W = GLA_HEADS * GLA_DV
NA_W = NA_HEADS * NA_DH
PROJ_W = 2 * QK_W + 2 * V_W + 2 * 1024 + 3 * NA_W
CW = 512
NQ_CHUNK = (2 * QK_W + 2 * V_W + 2048) // CW
NK_CHUNK = NQ_CHUNK + 1


def _params(sem):
    return pltpu.CompilerParams(dimension_semantics=sem, vmem_limit_bytes=VMEM_LIMIT)


def _split_bf16(a):
    hi = a.astype(BF16)
    lo = (a - hi.astype(F32)).astype(BF16)
    return hi, lo


def _dot(a, b):
    return jnp.dot(a, b, preferred_element_type=F32)


def _dot_nt(a, b):
    return lax.dot_general(a, b, (((1,), (1,)), ((), ())), preferred_element_type=F32)


def _dot_tn(a, b):
    return lax.dot_general(a, b, (((0,), (0,)), ((), ())), preferred_element_type=F32)


def _sigmoid(x):
    return 1.0 / (1.0 + jnp.exp(-x))


def _rmsnorm(x, g):
    ms = jnp.mean(x * x, axis=-1, keepdims=True)
    return x * lax.rsqrt(ms + EPS) * g


def _inproj_kernel(x_ref, g_ref, w_ref, wz_ref, wdh_ref, wdl_ref, bdec_ref, hm_ref, qg_ref, kg_ref,
                   proj_ref, la_ref):
    h = _rmsnorm(x_ref[...], g_ref[...]).astype(BF16)
    for c in range(PROJ_W // CW):
        acc = _dot(h, w_ref[:, c * CW:(c + 1) * CW])
        if c in (NQ_CHUNK, NK_CHUNK):
            gain = qg_ref if c == NQ_CHUNK else kg_ref
            ms = _dot((acc * acc).astype(BF16), hm_ref[...])
            acc = acc * lax.rsqrt(ms + EPS) * gain[...]
        proj_ref[:, c * CW:(c + 1) * CW] = acc.astype(BF16)
    z = _dot(h, wz_ref[...])
    zh, zl = _split_bf16(z)
    y = _dot(zh, wdh_ref[...]) + _dot(zl, wdh_ref[...]) + _dot(zh, wdl_ref[...]) + bdec_ref[...]
    ls = jnp.minimum(y, 0.0) - jnp.log(1.0 + jnp.exp(-jnp.abs(y)))
    la_ref[...] = ls * (1.0 / GLA_TAU)


def _inproj(x, g, w_main, wz, wd_hi, wd_lo, bdec, hm, qg, kg, tm):
    n, d = x.shape
    const = lambda i: (0, 0)
    return pl.pallas_call(
        _inproj_kernel,
        grid=(n // tm,),
        in_specs=[
            pl.BlockSpec((tm, d), lambda i: (i, 0)),
            pl.BlockSpec((1, d), const),
            pl.BlockSpec(w_main.shape, const),
            pl.BlockSpec(wz.shape, const),
            pl.BlockSpec(wd_hi.shape, const),
            pl.BlockSpec(wd_lo.shape, const),
            pl.BlockSpec(bdec.shape, const),
            pl.BlockSpec(hm.shape, const),
            pl.BlockSpec(qg.shape, const),
            pl.BlockSpec(kg.shape, const),
        ],
        out_specs=[
            pl.BlockSpec((tm, PROJ_W), lambda i: (i, 0)),
            pl.BlockSpec((tm, 2 * QK_W), lambda i: (i, 0)),
        ],
        out_shape=[
            jax.ShapeDtypeStruct((n, PROJ_W), BF16),
            jax.ShapeDtypeStruct((n, 2 * QK_W), F32),
        ],
        compiler_params=_params(("parallel",)),
        name="inproj",
    )(x, g, w_main, wz, wd_hi, wd_lo, bdec, hm, qg, kg)


N_LEVELS = 6
INTER_BLK = N_LEVELS
STATE_BLK = N_LEVELS + 1


def _decay_sum_matrix(reverse):
    c = GLA_CHUNK
    m_all = np.zeros((N_LEVELS + 2, c, c), np.float32)
    for l in range(N_LEVELS):
        m = 1 << l
        for p in range(c):
            mid = (p // (2 * m)) * 2 * m + m
            if p >= mid:
                m_all[l, p, mid + 1:p + 1] = 1.0
            else:
                m_all[l, p, p + 1:mid + 1] = 1.0
    for p in range(c):
        m_all[INTER_BLK, p, :p + 1] = 1.0
        m_all[STATE_BLK, p, p + 1:] = 1.0
    if reverse:
        m_all = m_all[:, ::-1, ::-1]
    return np.ascontiguousarray(m_all).reshape((N_LEVELS + 2) * c, c)


def _level_matrix(reverse):
    i = np.arange(GLA_CHUNK)[:, None]
    j = np.arange(GLA_CHUNK)[None, :]
    x = i ^ j
    lvl = np.where(x > 0, np.floor(np.log2(np.maximum(x, 1))), N_LEVELS).astype(np.int32)
    valid = (i <= j) if reverse else (i >= j)
    return np.where(valid, lvl, N_LEVELS + 1).astype(np.int32)


def _gla_kernel(q_ref, k_ref, v_ref, la_ref, m_ref, lvl_ref, o_ref, s_ref, *, reverse, n_chunks):
    c = GLA_CHUNK

    @pl.when(pl.program_id(1) == 0)
    def _():
        s_ref[...] = jnp.zeros_like(s_ref)

    lvl = lvl_ref[...]
    row = lax.broadcasted_iota(jnp.int32, (c, GLA_DK), 0)
    is_query = [(((row >> l) & 1) == (0 if reverse else 1)) for l in range(N_LEVELS)]
    total_row = 0 if reverse else c - 1

    def chunk(ci, carry):
        cc = (n_chunks - 1 - ci) if reverse else ci
        rows = pl.ds(pl.multiple_of(cc * c, c), c)
        la_hi, la_lo = _split_bf16(la_ref[0, rows, :])
        fac = jnp.exp(_dot(m_ref[...], la_hi) + _dot(m_ref[...], la_lo))
        for h in range(GLA_HEADS):
            hs = slice(h * GLA_DK, (h + 1) * GLA_DK)
            vs = slice(h * GLA_DV, (h + 1) * GLA_DV)
            q = q_ref[0, rows, hs].astype(F32)
            k = k_ref[0, rows, hs].astype(F32)
            v = v_ref[0, rows, vs]
            scores = jnp.zeros((c, c), F32)
            for l in range(N_LEVELS):
                f_l = fac[l * c:(l + 1) * c, hs]
                g_l = (jnp.where(is_query[l], q, k) * f_l).astype(BF16)
                scores = jnp.where(lvl == l, _dot_nt(g_l, g_l), scores)
            scores = jnp.where(lvl == N_LEVELS, _dot_nt(q.astype(BF16), k.astype(BF16)), scores)
            f_in = fac[INTER_BLK * c:(INTER_BLK + 1) * c, hs]
            f_st = fac[STATE_BLK * c:(STATE_BLK + 1) * c, hs]
            st = s_ref[h]
            inter = _dot_nt((q * f_in).astype(BF16), st.astype(BF16))
            intra = _dot(scores.astype(BF16), v)
            o_ref[0, rows, vs] = (inter + intra).astype(o_ref.dtype)
            dec = f_in[total_row:total_row + 1, :]
            s_ref[h] = st * dec + _dot_tn(v, (k * f_st).astype(BF16))
        return carry

    lax.fori_loop(0, n_chunks, chunk, 0)


def _gla(proj3, la3, reverse, tb):
    b, l, _ = proj3.shape
    nb = l // tb
    blk = (lambda i: nb - 1 - i) if reverse else (lambda i: i)
    m_all = jnp.asarray(_decay_sum_matrix(reverse), BF16)
    lvl = jnp.asarray(_level_matrix(reverse))
    kern = functools.partial(_gla_kernel, reverse=reverse, n_chunks=tb // GLA_CHUNK)
    return pl.pallas_call(
        kern,
        grid=(b, nb),
        in_specs=[
            pl.BlockSpec((1, tb, QK_W), lambda s, i: (s, blk(i), 0)),
            pl.BlockSpec((1, tb, QK_W), lambda s, i: (s, blk(i), 1)),
            pl.BlockSpec((1, tb, V_W), lambda s, i: (s, blk(i), 1)),
            pl.BlockSpec((1, tb, QK_W), lambda s, i: (s, blk(i), 1 if reverse else 0)),
            pl.BlockSpec(m_all.shape, lambda s, i: (0, 0)),
            pl.BlockSpec(lvl.shape, lambda s, i: (0, 0)),
        ],
        out_specs=pl.BlockSpec((1, tb, V_W), lambda s, i: (s, blk(i), 0)),
        out_shape=jax.ShapeDtypeStruct((b, l, V_W), BF16),
        scratch_shapes=[pltpu.VMEM((GLA_HEADS, GLA_DV, GLA_DK), F32)],
        compiler_params=_params(("parallel", "arbitrary")),
        name="gla_bwd" if reverse else "gla_fwd",
    )(proj3, proj3, proj3, la3, m_all, lvl)


NA_GROUP = 8
NA_BLK = NA_GROUP * GRID_W
NA_BAND = NA_KR * GRID_W


def _na_bias_table(rpb):
    qc = np.arange(GRID_W)[:, None]
    kc = np.arange(GRID_W)[None, :]
    col_start = np.clip(qc - NA_KC // 2, 0, GRID_W - NA_KC)
    valid = (kc >= col_start) & (kc < col_start + NA_KC)
    dc = np.clip(kc - qc, -(NA_KC - 1), NA_KC - 1) + NA_KC - 1
    d = np.arange(NA_KR)[:, None]
    kr = np.arange(NA_KR)[None, :]
    dr = -d + kr + NA_KR - 1
    t = rpb[:, dr[:, :, None, None], dc[None, None, :, :]]
    t = jnp.where(valid[None, None, None], t, NEG)
    t = t.transpose(1, 0, 3, 2, 4)
    return t.reshape(NA_KR, NA_HEADS // 2, 2 * GRID_W, NA_BAND).astype(F32)


def _na_kernel(q_ref, kp_ref, kc_ref, kn_ref, vp_ref, vc_ref, vn_ref, bias_ref, o_ref,
               kwin_ref, vwin_ref, *, rows):
    g = pl.program_id(1)
    for j, (kr, vr) in enumerate(((kp_ref, vp_ref), (kc_ref, vc_ref), (kn_ref, vn_ref))):
        kwin_ref[j * NA_BLK:(j + 1) * NA_BLK, :] = kr[0]
        vwin_ref[j * NA_BLK:(j + 1) * NA_BLK, :] = vr[0]
    lane = lax.broadcasted_iota(jnp.int32, (GRID_W, 2 * NA_DH), 1)
    first = lane < NA_DH

    def one_row(rl, carry):
        r = g * NA_GROUP + rl
        rs = jnp.clip(r - NA_KR // 2, 0, rows - NA_KR)
        d = r - rs
        off = pl.multiple_of((rs - (g - 1) * NA_GROUP) * GRID_W, GRID_W)
        qrows = pl.ds(pl.multiple_of(rl * GRID_W, GRID_W), GRID_W)
        for p in range(NA_HEADS // 2):
            ps = slice(p * 2 * NA_DH, (p + 1) * 2 * NA_DH)
            qp = q_ref[0, qrows, ps]
            zero = jnp.zeros_like(qp)
            qs = jnp.concatenate([jnp.where(first, qp, zero), jnp.where(first, zero, qp)], axis=0)
            kb = kwin_ref[pl.ds(off, NA_BAND), ps]
            vb = vwin_ref[pl.ds(off, NA_BAND), ps]
            s = _dot_nt(qs, kb) + bias_ref[d, p]
            e = jnp.exp(s - jnp.max(s, axis=-1, keepdims=True))
            den = jnp.sum(e, axis=-1, keepdims=True)
            pv = _dot(e.astype(BF16), vb) / den
            o_ref[0, qrows, ps] = jnp.where(first, pv[:GRID_W], pv[GRID_W:]).astype(o_ref.dtype)
        return carry

    lax.fori_loop(0, NA_GROUP, one_row, 0)


def _natten(proj3, bias):
    b, l, _ = proj3.shape
    rows = l // GRID_W
    ng = rows // NA_GROUP
    qcol, kcol, vcol = NQ_CHUNK, NQ_CHUNK + 1, NQ_CHUNK + 2
    prev = lambda i: jnp.maximum(i - 1, 0)
    nxt = lambda i: jnp.minimum(i + 1, ng - 1)
    spec = lambda f, col: pl.BlockSpec((1, NA_BLK, NA_W), lambda s, i: (s, f(i), col))
    same = lambda i: i
    return pl.pallas_call(
        functools.partial(_na_kernel, rows=rows),
        grid=(b, ng),
        in_specs=[
            spec(same, qcol),
            spec(prev, kcol), spec(same, kcol), spec(nxt, kcol),
            spec(prev, vcol), spec(same, vcol), spec(nxt, vcol),
            pl.BlockSpec(bias.shape, lambda s, i: (0, 0, 0, 0)),
        ],
        out_specs=pl.BlockSpec((1, NA_BLK, NA_W), lambda s, i: (s, i, 0)),
        out_shape=jax.ShapeDtypeStruct((b, l, NA_W), BF16),
        scratch_shapes=[pltpu.VMEM((3 * NA_BLK, NA_W), BF16), pltpu.VMEM((3 * NA_BLK, NA_W), BF16)],
        compiler_params=_params(("parallel", "parallel")),
        name="natten",
    )(proj3, proj3, proj3, proj3, proj3, proj3, proj3, bias)


def _postmix_kernel(*refs, with_router):
    if with_router:
        (of_ref, ob_ref, r_ref, ga_ref, gb_ref, na_ref, x_ref, gog_ref, wog_ref, won_ref, wout_ref,
         gf_ref, rh_ref, rl_ref, x1_ref, h2_ref, lg_ref) = refs
    else:
        (of_ref, ob_ref, r_ref, ga_ref, gb_ref, na_ref, x_ref, gog_ref, wog_ref, won_ref, wout_ref,
         gf_ref, x1_ref, h2_ref) = refs
    o = of_ref[...].astype(F32) + ob_ref[...].astype(F32)
    parts = []
    for h in range(GLA_HEADS):
        seg = o[:, h * GLA_DV:(h + 1) * GLA_DV]
        ms = jnp.mean(seg * seg, axis=-1, keepdims=True)
        parts.append(seg * lax.rsqrt(ms + EPS))
    r = r_ref[...].astype(F32)
    on = jnp.concatenate(parts, axis=-1) * gog_ref[...] * (r * _sigmoid(r))
    ya = _dot(on.astype(BF16), wog_ref[...])
    yb = _dot(na_ref[...], won_ref[...])
    merged = _sigmoid(ga_ref[...].astype(F32)) * ya + _sigmoid(gb_ref[...].astype(F32)) * yb
    x1 = x_ref[...] + _dot(merged.astype(BF16), wout_ref[...])
    x1_ref[...] = x1
    h2 = _rmsnorm(x1, gf_ref[...])
    h2_ref[...] = h2.astype(BF16)
    if with_router:
        hh, hl = _split_bf16(h2)
        lg_ref[...] = _dot(hh, rh_ref[...]) + _dot(hl, rh_ref[...]) + _dot(hh, rl_ref[...])


def _postmix(of, ob, proj, na, x, gog, wog, won, wout, gf, router, tm):
    n, d = x.shape
    const = lambda i: (0, 0)
    with_router = router is not None
    in_specs = [
        pl.BlockSpec((tm, V_W), lambda i: (i, 0)),
        pl.BlockSpec((tm, V_W), lambda i: (i, 0)),
        pl.BlockSpec((tm, 1024), lambda i: (i, 2)),
        pl.BlockSpec((tm, 1024), lambda i: (i, 3)),
        pl.BlockSpec((tm, 1024), lambda i: (i, 4)),
        pl.BlockSpec((tm, NA_W), lambda i: (i, 0)),
        pl.BlockSpec((tm, d), lambda i: (i, 0)),
        pl.BlockSpec((1, V_W), const),
        pl.BlockSpec(wog.shape, const),
        pl.BlockSpec(won.shape, const),
        pl.BlockSpec(wout.shape, const),
        pl.BlockSpec((1, d), const),
    ]
    args = [of, ob, proj, proj, proj, na, x, gog, wog, won, wout, gf]
    out_specs = [pl.BlockSpec((tm, d), lambda i: (i, 0)), pl.BlockSpec((tm, d), lambda i: (i, 0))]
    out_shape = [jax.ShapeDtypeStruct((n, d), F32), jax.ShapeDtypeStruct((n, d), BF16)]
    if with_router:
        in_specs += [pl.BlockSpec(router[0].shape, const), pl.BlockSpec(router[1].shape, const)]
        args += list(router)
        out_specs.append(pl.BlockSpec((tm, N_EXPERTS), lambda i: (i, 0)))
        out_shape.append(jax.ShapeDtypeStruct((n, N_EXPERTS), F32))
    return pl.pallas_call(
        functools.partial(_postmix_kernel, with_router=with_router),
        grid=(n // tm,),
        in_specs=in_specs,
        out_specs=out_specs,
        out_shape=out_shape,
        compiler_params=_params(("parallel",)),
        name="postmix_router" if with_router else "postmix",
    )(*args)


def _ffn_kernel(x1_ref, h_ref, wg_ref, wu_ref, wd_ref, o_ref, acc_ref):
    j = pl.program_id(1)

    @pl.when(j == 0)
    def _():
        acc_ref[...] = jnp.zeros_like(acc_ref)

    h = h_ref[...]
    g = _dot(h, wg_ref[...])
    u = _dot(h, wu_ref[...])
    a = (g * _sigmoid(g) * u).astype(BF16)
    acc_ref[...] += _dot(a, wd_ref[...])

    @pl.when(j == pl.num_programs(1) - 1)
    def _():
        o_ref[...] = x1_ref[...] + acc_ref[...]


def _ffn(x1, h2, wg, wu, wd, tm, tf):
    n, d = x1.shape
    dff = wg.shape[1]
    return pl.pallas_call(
        _ffn_kernel,
        grid=(n // tm, dff // tf),
        in_specs=[
            pl.BlockSpec((tm, d), lambda i, j: (i, 0)),
            pl.BlockSpec((tm, d), lambda i, j: (i, 0)),
            pl.BlockSpec((d, tf), lambda i, j: (0, j)),
            pl.BlockSpec((d, tf), lambda i, j: (0, j)),
            pl.BlockSpec((tf, d), lambda i, j: (j, 0)),
        ],
        out_specs=pl.BlockSpec((tm, d), lambda i, j: (i, 0)),
        out_shape=jax.ShapeDtypeStruct((n, d), F32),
        scratch_shapes=[pltpu.VMEM((tm, d), F32)],
        compiler_params=_params(("parallel", "arbitrary")),
        name="ffn",
    )(x1, h2, wg, wu, wd)


def _moe_kernel(x1_ref, h_ref, lg_ref, wg_ref, wu_ref, wd_ref, o_ref, acc_ref, comb_ref):
    e = pl.program_id(1)
    lane = lax.broadcasted_iota(jnp.int32, lg_ref.shape, 1).astype(F32)

    @pl.when(e == 0)
    def _():
        acc_ref[...] = jnp.zeros_like(acc_ref)
        lg = lg_ref[...]
        m1 = jnp.max(lg, axis=-1, keepdims=True)
        i1 = jnp.min(jnp.where(lg == m1, lane, float(N_EXPERTS)), axis=-1, keepdims=True)
        lg2 = jnp.where(lane == i1, -jnp.inf, lg)
        m2 = jnp.max(lg2, axis=-1, keepdims=True)
        i2 = jnp.min(jnp.where(lg2 == m2, lane, float(N_EXPERTS)), axis=-1, keepdims=True)
        t = jnp.exp(m2 - m1)
        w1 = 1.0 / (1.0 + t)
        comb_ref[...] = jnp.where(lane == i1, w1, 0.0) + jnp.where(lane == i2, t * w1, 0.0)

    ce = jnp.sum(jnp.where(lane == e.astype(F32), comb_ref[...], 0.0), axis=-1, keepdims=True)
    h = h_ref[...]
    g = _dot(h, wg_ref[0])
    u = _dot(h, wu_ref[0])
    a = (g * _sigmoid(g) * u * ce).astype(BF16)
    acc_ref[...] += _dot(a, wd_ref[0])

    @pl.when(e == pl.num_programs(1) - 1)
    def _():
        o_ref[...] = x1_ref[...] + acc_ref[...]


def _moe(x1, h2, logits, wg, wu, wd, tm):
    n, d = x1.shape
    ne, _, dfe = wg.shape
    return pl.pallas_call(
        _moe_kernel,
        grid=(n // tm, ne),
        in_specs=[
            pl.BlockSpec((tm, d), lambda i, e: (i, 0)),
            pl.BlockSpec((tm, d), lambda i, e: (i, 0)),
            pl.BlockSpec((tm, ne), lambda i, e: (i, 0)),
            pl.BlockSpec((1, d, dfe), lambda i, e: (e, 0, 0)),
            pl.BlockSpec((1, d, dfe), lambda i, e: (e, 0, 0)),
            pl.BlockSpec((1, dfe, d), lambda i, e: (e, 0, 0)),
        ],
        out_specs=pl.BlockSpec((tm, d), lambda i, e: (i, 0)),
        out_shape=jax.ShapeDtypeStruct((n, d), F32),
        scratch_shapes=[pltpu.VMEM((tm, d), F32), pltpu.VMEM((tm, ne), F32)],
        compiler_params=_params(("parallel", "arbitrary")),
        name="moe",
    )(x1, h2, logits, wg, wu, wd)


def _row_tile(n, want):
    t = min(n, want)
    assert n % t == 0
    return t


def _prep_layer(l, norm_mix, w_in, w_decay_f, b_decay_f, w_decay_b, b_decay_b, gla_out_norm, q_norm,
                k_norm, rpb, w_o_gla, w_o_na, w_out, norm_ffn):
    w = w_in[l]
    c0 = 2 * QK_W + 2 * V_W
    c1 = c0 + 2 * GLA_RANK
    c2 = c1 + 3 * NA_W
    w_main = jnp.concatenate([w[:, :QK_W] * (GLA_DK ** -0.5), w[:, QK_W:c0], w[:, c2:], w[:, c1:c2]],
                             axis=1).astype(BF16)
    wz = w[:, c0:c1].astype(BF16)
    zero = jnp.zeros((GLA_RANK, QK_W), F32)
    wdec = jnp.concatenate([jnp.concatenate([w_decay_f[l], zero], axis=1),
                            jnp.concatenate([zero, w_decay_b[l]], axis=1)], axis=0)
    wd_hi, wd_lo = _split_bf16(wdec)
    bdec = jnp.concatenate([b_decay_f[l], b_decay_b[l]])[None, :]
    qg = jnp.tile(q_norm[l] * (NA_DH ** -0.5), NA_HEADS)[None, :]
    kg = jnp.tile(k_norm[l], NA_HEADS)[None, :]
    return dict(
        g_mix=norm_mix[l][None, :], w_main=w_main, wz=wz, wd_hi=wd_hi, wd_lo=wd_lo, bdec=bdec,
        qg=qg, kg=kg, bias=_na_bias_table(rpb[l]),
        gog=gla_out_norm[l].reshape(1, V_W), wog=w_o_gla[l].astype(BF16),
        won=w_o_na[l].astype(BF16), wout=w_out[l].astype(BF16), g_ffn=norm_ffn[l][None, :])


def _trunk(x, layers, dense, moe):
    b, l, d = x.shape
    n = b * l
    xf = x.reshape(n, d)
    head_mean = jnp.asarray(np.kron(np.eye(NA_HEADS), np.full((NA_DH, NA_DH), 1.0 / NA_DH)), BF16)
    tb = _row_tile(l, 2048)
    for li, p in enumerate(layers):
        proj, la = _inproj(xf, p["g_mix"], p["w_main"], p["wz"], p["wd_hi"], p["wd_lo"], p["bdec"],
                           head_mean, p["qg"], p["kg"], _row_tile(n, 256))
        proj3 = proj.reshape(b, l, PROJ_W)
        la3 = la.reshape(b, l, 2 * QK_W)
        of = _gla(proj3, la3, False, tb).reshape(n, V_W)
        ob = _gla(proj3, la3, True, tb).reshape(n, V_W)
        na = _natten(proj3, p["bias"]).reshape(n, NA_W)
        if li % 2 == 0:
            wg, wu, wd = dense[li // 2]
            x1, h2 = _postmix(of, ob, proj, na, xf, p["gog"], p["wog"], p["won"], p["wout"],
                              p["g_ffn"], None, _row_tile(n, 512))
            xf = _ffn(x1, h2, wg, wu, wd, _row_tile(n, 1024), wg.shape[1] // 2)
        else:
            router, wg, wu, wd = moe[li // 2]
            x1, h2, logits = _postmix(of, ob, proj, na, xf, p["gog"], p["wog"], p["won"], p["wout"],
                                      p["g_ffn"], router, _row_tile(n, 512))
            xf = _moe(x1, h2, logits, wg, wu, wd, _row_tile(n, 512))
    return xf.reshape(b, l, d)


def kernel(x_prompt, x_sample, norm_mix, w_in, w_decay_f, b_decay_f, w_decay_b, b_decay_b, gla_out_norm,
           q_norm, k_norm, rpb, w_o_gla, w_o_na, w_out, norm_ffn, ffn_w_gate, ffn_w_up, ffn_w_down,
           moe_router, moe_w_gate, moe_w_up, moe_w_down):
    depth = w_in.shape[0]
    layers = [_prep_layer(l, norm_mix, w_in, w_decay_f, b_decay_f, w_decay_b, b_decay_b, gla_out_norm,
                          q_norm, k_norm, rpb, w_o_gla, w_o_na, w_out, norm_ffn) for l in range(depth)]
    dense = [(ffn_w_gate[j].astype(BF16), ffn_w_up[j].astype(BF16), ffn_w_down[j].astype(BF16))
             for j in range(ffn_w_gate.shape[0])]
    moe = [(_split_bf16(moe_router[j]), moe_w_gate[j].astype(BF16), moe_w_up[j].astype(BF16),
            moe_w_down[j].astype(BF16)) for j in range(moe_router.shape[0])]
    return (_trunk(x_prompt, layers, dense, moe), _trunk(x_sample, layers, dense, moe))
```

```python
import functools

import numpy as np
import jax
import jax.numpy as jnp
from jax import lax
from jax.experimental import pallas as pl
from jax.experimental.pallas import tpu as pltpu

F32 = jnp.float32
BF16 = jnp.bfloat16

EPS = 1e-6
GRID_W = 64
GLA_HEADS = 4
GLA_DK = 128
GLA_DV = 256
GLA_CHUNK = 64
GLA_RANK = 16
GLA_TAU = 16.0
NA_HEADS = 8
NA_DH = 64
NA_KR = 8
NA_KC = 16
N_EXPERTS = 8
NEG = -1e30

VMEM_LIMIT = 56 * 1024 * 1024

QK_W = GLA_HEADS * GLA_DK
V_W = GLA_HEADS * GLA_DV
NA_W = NA_HEADS * NA_DH
PROJ_W = 2 * QK_W + 2 * V_W + 2 * 1024 + 3 * NA_W
CW = 512
NQ_CHUNK = (2 * QK_W + 2 * V_W + 2048) // CW
NK_CHUNK = NQ_CHUNK + 1


def _params(sem):
    return pltpu.CompilerParams(dimension_semantics=sem, vmem_limit_bytes=VMEM_LIMIT)


def _resident(a):
    nd = a.ndim
    return pl.BlockSpec(a.shape, lambda *_: (0,) * nd, pipeline_mode=pl.Buffered(1))


def _split_bf16(a):
    hi = a.astype(BF16)
    lo = (a - hi.astype(F32)).astype(BF16)
    return hi, lo


def _dot(a, b):
    return jnp.dot(a, b, preferred_element_type=F32)


def _dot_nt(a, b):
    return lax.dot_general(a, b, (((1,), (1,)), ((), ())), preferred_element_type=F32)


def _dot_tn(a, b):
    return lax.dot_general(a, b, (((0,), (0,)), ((), ())), preferred_element_type=F32)


def _sigmoid(x):
    return 1.0 / (1.0 + jnp.exp(-x))


def _rmsnorm(x, g):
    ms = jnp.mean(x * x, axis=-1, keepdims=True)
    return x * lax.rsqrt(ms + EPS) * g


def _inproj_kernel(x_ref, g_ref, w_ref, wz_ref, wdh_ref, wdl_ref, bdec_ref, hm_ref, qg_ref, kg_ref,
                   proj_ref, la_ref, lamin_ref):
    h = _rmsnorm(x_ref[...], g_ref[...]).astype(BF16)
    z = _dot(h, wz_ref[...])
    zh, zl = _split_bf16(z)
    y = _dot(zh, wdh_ref[...]) + _dot(zl, wdh_ref[...]) + _dot(zh, wdl_ref[...]) + bdec_ref[...]
    ls = jnp.minimum(y, 0.0) - jnp.log(1.0 + jnp.exp(-jnp.abs(y)))
    la = ls * (1.0 / GLA_TAU)
    la_ref[...] = la
    lamin_ref[0] = jnp.broadcast_to(jnp.min(la, axis=0, keepdims=True), lamin_ref.shape[1:])
    for c in range(PROJ_W // CW):
        acc = _dot(h, w_ref[:, c * CW:(c + 1) * CW])
        if c in (NQ_CHUNK, NK_CHUNK):
            gain = qg_ref if c == NQ_CHUNK else kg_ref
            ms = _dot((acc * acc).astype(BF16), hm_ref[...])
            acc = acc * lax.rsqrt(ms + EPS) * gain[...]
        proj_ref[:, c * CW:(c + 1) * CW] = acc.astype(BF16)


def _inproj(x, g, w_main, wz, wd_hi, wd_lo, bdec, hm, qg, kg, tm):
    n, d = x.shape
    return pl.pallas_call(
        _inproj_kernel,
        grid=(n // tm,),
        in_specs=[
            pl.BlockSpec((tm, d), lambda i: (i, 0)),
            _resident(g), _resident(w_main), _resident(wz), _resident(wd_hi), _resident(wd_lo),
            _resident(bdec), _resident(hm), _resident(qg), _resident(kg),
        ],
        out_specs=[
            pl.BlockSpec((tm, PROJ_W), lambda i: (i, 0)),
            pl.BlockSpec((tm, 2 * QK_W), lambda i: (i, 0)),
            pl.BlockSpec((1, 8, 2 * QK_W), lambda i: (i, 0, 0)),
        ],
        out_shape=[
            jax.ShapeDtypeStruct((n, PROJ_W), BF16),
            jax.ShapeDtypeStruct((n, 2 * QK_W), F32),
            jax.ShapeDtypeStruct((n // tm, 8, 2 * QK_W), F32),
        ],
        compiler_params=_params(("parallel",)),
        name="inproj",
    )(x, g, w_main, wz, wd_hi, wd_lo, bdec, hm, qg, kg)


N_LEVELS = 6
INTER_BLK = N_LEVELS
STATE_BLK = N_LEVELS + 1


def _decay_sum_matrix(reverse):
    c = GLA_CHUNK
    m_all = np.zeros((N_LEVELS + 2, c, c), np.float32)
    for l in range(N_LEVELS):
        m = 1 << l
        for p in range(c):
            mid = (p // (2 * m)) * 2 * m + m
            if p >= mid:
                m_all[l, p, mid + 1:p + 1] = 1.0
            else:
                m_all[l, p, p + 1:mid + 1] = 1.0
    for p in range(c):
        m_all[INTER_BLK, p, :p + 1] = 1.0
        m_all[STATE_BLK, p, p + 1:] = 1.0
    if reverse:
        m_all = m_all[:, ::-1, ::-1]
    return np.ascontiguousarray(m_all).reshape((N_LEVELS + 2) * c, c)


def _level_matrix(reverse):
    i = np.arange(GLA_CHUNK)[:, None]
    j = np.arange(GLA_CHUNK)[None, :]
    x = i ^ j
    lvl = np.where(x > 0, np.floor(np.log2(np.maximum(x, 1))), N_LEVELS).astype(np.int32)
    valid = (i <= j) if reverse else (i >= j)
    return np.where(valid, lvl, N_LEVELS + 1).astype(np.int32)


GLA_FAST_CHUNK = 128
GLA_FAST_MAX_DECAY = 60.0


def _tri_matrix(reverse, c):
    i = np.arange(c)[:, None]
    j = np.arange(c)[None, :]
    return ((j >= i) if reverse else (j <= i)).astype(np.float32)


def _gla_kernel(q_ref, k_ref, v_ref, la_ref, lamin_ref, m_ref, lvl_ref, tri_ref, o_ref,
                s_ref, qh_ref, kh_ref, ks_ref, sc_ref, dec_ref, *, reverse, tb):
    @pl.when(pl.program_id(1) == 0)
    def _():
        s_ref[...] = jnp.zeros_like(s_ref)

    fc = GLA_FAST_CHUNK
    n_fast = tb // fc
    total_row = 0 if reverse else fc - 1

    def fast_rows(ci):
        return pl.ds(pl.multiple_of(ci * fc, fc), fc)

    def fast_factors(ci, carry):
        rows = fast_rows(ci)
        la_hi, la_lo = _split_bf16(la_ref[0, rows, :])
        b = _dot(tri_ref[...], la_hi) + _dot(tri_ref[...], la_lo)
        total = b[total_row:total_row + 1, :]
        q = q_ref[0, rows, :].astype(F32)
        k = k_ref[0, rows, :].astype(F32)
        qh_ref[rows, :] = (q * jnp.exp(b)).astype(BF16)
        kh_ref[rows, :] = (k * jnp.exp(-b)).astype(BF16)
        ks_ref[rows, :] = (k * jnp.exp(total - b)).astype(BF16)
        dec_ref[pl.ds(pl.multiple_of(ci * 8, 8), 8), :] = jnp.broadcast_to(jnp.exp(total), (8, QK_W))
        return carry

    def fast_scores(ci, carry):
        rows = fast_rows(ci)
        i = lax.broadcasted_iota(jnp.int32, (fc, fc), 0)
        j = lax.broadcasted_iota(jnp.int32, (fc, fc), 1)
        causal = (i <= j) if reverse else (i >= j)
        for h in range(GLA_HEADS):
            hs = slice(h * GLA_DK, (h + 1) * GLA_DK)
            sc = _dot_nt(qh_ref[rows, hs], kh_ref[rows, hs])
            sc_ref[rows, hs] = jnp.where(causal, sc, 0.0).astype(BF16)
        return carry

    def fast_state(ci, carry):
        cc = (n_fast - 1 - ci) if reverse else ci
        rows = fast_rows(cc)
        dec = dec_ref[pl.ds(pl.multiple_of(cc * 8, 8), 8), :][:1, :]
        for h in range(GLA_HEADS):
            hs = slice(h * GLA_DK, (h + 1) * GLA_DK)
            vs = slice(h * GLA_DV, (h + 1) * GLA_DV)
            v = v_ref[0, rows, vs]
            st = s_ref[h]
            o = _dot(sc_ref[rows, hs], v) + _dot_nt(qh_ref[rows, hs], st.astype(BF16))
            o_ref[0, rows, vs] = o.astype(o_ref.dtype)
            s_ref[h] = st * dec[:, hs] + _dot_tn(v, ks_ref[rows, hs])
        return carry

    def robust_chunk(ci, carry):
        c = GLA_CHUNK
        n_chunks = tb // c
        lvl = lvl_ref[...]
        row = lax.broadcasted_iota(jnp.int32, (c, GLA_DK), 0)
        total_row = 0 if reverse else c - 1
        cc = (n_chunks - 1 - ci) if reverse else ci
        rows = pl.ds(pl.multiple_of(cc * c, c), c)
        la_hi, la_lo = _split_bf16(la_ref[0, rows, :])
        fac = jnp.exp(_dot(m_ref[...], la_hi) + _dot(m_ref[...], la_lo))
        for h in range(GLA_HEADS):
            hs = slice(h * GLA_DK, (h + 1) * GLA_DK)
            vs = slice(h * GLA_DV, (h + 1) * GLA_DV)
            q = q_ref[0, rows, hs].astype(F32)
            k = k_ref[0, rows, hs].astype(F32)
            v = v_ref[0, rows, vs]
            scores = jnp.zeros((c, c), F32)
            for l in range(N_LEVELS):
                is_query = ((row >> l) & 1) == (0 if reverse else 1)
                f_l = fac[l * c:(l + 1) * c, hs]
                g_l = (jnp.where(is_query, q, k) * f_l).astype(BF16)
                scores = jnp.where(lvl == l, _dot_nt(g_l, g_l), scores)
            scores = jnp.where(lvl == N_LEVELS, _dot_nt(q.astype(BF16), k.astype(BF16)), scores)
            f_in = fac[INTER_BLK * c:(INTER_BLK + 1) * c, hs]
            f_st = fac[STATE_BLK * c:(STATE_BLK + 1) * c, hs]
            st = s_ref[h]
            inter = _dot_nt((q * f_in).astype(BF16), st.astype(BF16))
            intra = _dot(scores.astype(BF16), v)
            o_ref[0, rows, vs] = (inter + intra).astype(o_ref.dtype)
            dec = f_in[total_row:total_row + 1, :]
            s_ref[h] = st * dec + _dot_tn(v, (k * f_st).astype(BF16))
        return carry

    bounded = jnp.min(lamin_ref[...]) >= -GLA_FAST_MAX_DECAY / GLA_FAST_CHUNK

    @pl.when(bounded)
    def _():
        lax.fori_loop(0, n_fast, fast_factors, 0, unroll=2)
        lax.fori_loop(0, n_fast, fast_scores, 0, unroll=2)
        lax.fori_loop(0, n_fast, fast_state, 0, unroll=2)

    @pl.when(jnp.logical_not(bounded))
    def _():
        lax.fori_loop(0, tb // GLA_CHUNK, robust_chunk, 0)


def _gla(proj3, la3, lamin, reverse, tb):
    b, l, _ = proj3.shape
    nb = l // tb
    tiles = lamin.shape[0] // (b * nb)
    blk = (lambda i: nb - 1 - i) if reverse else (lambda i: i)
    m_all = jnp.asarray(_decay_sum_matrix(reverse), BF16)
    lvl = jnp.asarray(_level_matrix(reverse))
    tri = jnp.asarray(_tri_matrix(reverse, GLA_FAST_CHUNK), BF16)
    kern = functools.partial(_gla_kernel, reverse=reverse, tb=tb)
    const = lambda s, i: (0, 0)
    return pl.pallas_call(
        kern,
        grid=(b, nb),
        in_specs=[
            pl.BlockSpec((1, tb, QK_W), lambda s, i: (s, blk(i), 0)),
            pl.BlockSpec((1, tb, QK_W), lambda s, i: (s, blk(i), 1)),
            pl.BlockSpec((1, tb, V_W), lambda s, i: (s, blk(i), 1)),
            pl.BlockSpec((1, tb, QK_W), lambda s, i: (s, blk(i), 1 if reverse else 0)),
            pl.BlockSpec((tiles, 8, QK_W), lambda s, i: (s * nb + blk(i), 0, 1 if reverse else 0)),
            _resident(m_all), _resident(lvl), _resident(tri),
        ],
        out_specs=pl.BlockSpec((1, tb, V_W), lambda s, i: (s, blk(i), 0)),
        out_shape=jax.ShapeDtypeStruct((b, l, V_W), BF16),
        scratch_shapes=[
            pltpu.VMEM((GLA_HEADS, GLA_DV, GLA_DK), F32),
            pltpu.VMEM((tb, QK_W), BF16),
            pltpu.VMEM((tb, QK_W), BF16),
            pltpu.VMEM((tb, QK_W), BF16),
            pltpu.VMEM((tb, QK_W), BF16),
            pltpu.VMEM((tb // GLA_FAST_CHUNK * 8, QK_W), F32),
        ],
        compiler_params=_params(("parallel", "arbitrary")),
        name="gla_bwd" if reverse else "gla_fwd",
    )(proj3, proj3, proj3, la3, lamin, m_all, lvl, tri)


NA_GROUP = 8
NA_BLK = NA_GROUP * GRID_W
NA_BAND = NA_KR * GRID_W


def _na_bias_table(rpb):
    qc = np.arange(GRID_W)[:, None]
    kc = np.arange(GRID_W)[None, :]
    col_start = np.clip(qc - NA_KC // 2, 0, GRID_W - NA_KC)
    valid = (kc >= col_start) & (kc < col_start + NA_KC)
    dc = np.clip(kc - qc, -(NA_KC - 1), NA_KC - 1) + NA_KC - 1
    onehot = ((dc[None] == np.arange(2 * NA_KC - 1)[:, None, None]) & valid[None]).astype(np.float32)
    mask = np.where(valid, 0.0, NEG).astype(np.float32)
    rows = jnp.stack([rpb[:, NA_KR - 1 - d:2 * NA_KR - 1 - d, :] for d in range(NA_KR)])
    t = jnp.einsum('dhkc,cqj->dhqkj', rows, jnp.asarray(onehot), precision=lax.Precision.HIGHEST)
    t = t + jnp.asarray(mask)[None, None, :, None, :]
    return t.reshape(NA_KR, NA_HEADS // 2, 2 * GRID_W, NA_BAND)


def _na_kernel(q_ref, kp_ref, kc_ref, kn_ref, vp_ref, vc_ref, vn_ref, bias_ref, o_ref,
               kwin_ref, vwin_ref, *, rows):
    g = pl.program_id(1)
    for j, (kr, vr) in enumerate(((kp_ref, vp_ref), (kc_ref, vc_ref), (kn_ref, vn_ref))):
        kwin_ref[j * NA_BLK:(j + 1) * NA_BLK, :] = kr[0]
        vwin_ref[j * NA_BLK:(j + 1) * NA_BLK, :] = vr[0]
    lane = lax.broadcasted_iota(jnp.int32, (GRID_W, 2 * NA_DH), 1)
    first = lane < NA_DH

    def one_row(rl, carry):
        r = g * NA_GROUP + rl
        rs = jnp.clip(r - NA_KR // 2, 0, rows - NA_KR)
        d = r - rs
        off = pl.multiple_of((rs - (g - 1) * NA_GROUP) * GRID_W, GRID_W)
        qrows = pl.ds(pl.multiple_of(rl * GRID_W, GRID_W), GRID_W)
        pairs = [slice(p * 2 * NA_DH, (p + 1) * 2 * NA_DH) for p in range(NA_HEADS // 2)]
        scores = []
        for p, ps in enumerate(pairs):
            qp = q_ref[0, qrows, ps]
            zero = jnp.zeros_like(qp)
            qs = jnp.concatenate([jnp.where(first, qp, zero), jnp.where(first, zero, qp)], axis=0)
            kb = kwin_ref[pl.ds(off, NA_BAND), ps]
            scores.append(_dot_nt(qs, kb) + bias_ref[d, p])
        probs = []
        for s in scores:
            e = jnp.exp(s - jnp.max(s, axis=-1, keepdims=True))
            probs.append((e.astype(BF16), jnp.sum(e, axis=-1, keepdims=True)))
        for (e, den), ps in zip(probs, pairs):
            vb = vwin_ref[pl.ds(off, NA_BAND), ps]
            pv = _dot(e, vb) / den
            o_ref[0, qrows, ps] = jnp.where(first, pv[:GRID_W], pv[GRID_W:]).astype(o_ref.dtype)
        return carry

    lax.fori_loop(0, NA_GROUP, one_row, 0, unroll=2)


def _natten(proj3, bias):
    b, l, _ = proj3.shape
    rows = l // GRID_W
    ng = rows // NA_GROUP
    qcol, kcol, vcol = NQ_CHUNK, NQ_CHUNK + 1, NQ_CHUNK + 2
    prev = lambda i: jnp.maximum(i - 1, 0)
    nxt = lambda i: jnp.minimum(i + 1, ng - 1)
    spec = lambda f, col: pl.BlockSpec((1, NA_BLK, NA_W), lambda s, i: (s, f(i), col))
    same = lambda i: i
    return pl.pallas_call(
        functools.partial(_na_kernel, rows=rows),
        grid=(b, ng),
        in_specs=[
            spec(same, qcol),
            spec(prev, kcol), spec(same, kcol), spec(nxt, kcol),
            spec(prev, vcol), spec(same, vcol), spec(nxt, vcol),
            _resident(bias),
        ],
        out_specs=pl.BlockSpec((1, NA_BLK, NA_W), lambda s, i: (s, i, 0)),
        out_shape=jax.ShapeDtypeStruct((b, l, NA_W), BF16),
        scratch_shapes=[pltpu.VMEM((3 * NA_BLK, NA_W), BF16), pltpu.VMEM((3 * NA_BLK, NA_W), BF16)],
        compiler_params=_params(("parallel", "parallel")),
        name="natten",
    )(proj3, proj3, proj3, proj3, proj3, proj3, proj3, bias)


def _postmix_kernel(*refs, with_router):
    if with_router:
        (of_ref, ob_ref, r_ref, ga_ref, gb_ref, na_ref, x_ref, gog_ref, wog_ref, won_ref, wout_ref,
         gf_ref, rh_ref, rl_ref, x1_ref, h2_ref, lg_ref) = refs
    else:
        (of_ref, ob_ref, r_ref, ga_ref, gb_ref, na_ref, x_ref, gog_ref, wog_ref, won_ref, wout_ref,
         gf_ref, x1_ref, h2_ref) = refs
    o = of_ref[...].astype(F32) + ob_ref[...].astype(F32)
    parts = []
    for h in range(GLA_HEADS):
        seg = o[:, h * GLA_DV:(h + 1) * GLA_DV]
        ms = jnp.mean(seg * seg, axis=-1, keepdims=True)
        parts.append(seg * lax.rsqrt(ms + EPS))
    r = r_ref[...].astype(F32)
    on = jnp.concatenate(parts, axis=-1) * gog_ref[...] * (r * _sigmoid(r))
    ya = _dot(on.astype(BF16), wog_ref[...])
    yb = _dot(na_ref[...], won_ref[...])
    merged = _sigmoid(ga_ref[...].astype(F32)) * ya + _sigmoid(gb_ref[...].astype(F32)) * yb
    x1 = x_ref[...] + _dot(merged.astype(BF16), wout_ref[...])
    x1_ref[...] = x1
    h2 = _rmsnorm(x1, gf_ref[...])
    h2_ref[...] = h2.astype(BF16)
    if with_router:
        hh, hl = _split_bf16(h2)
        lg_ref[...] = _dot(hh, rh_ref[...]) + _dot(hl, rh_ref[...]) + _dot(hh, rl_ref[...])


def _postmix(of, ob, proj, na, x, gog, wog, won, wout, gf, router, tm):
    n, d = x.shape
    const = lambda i: (0, 0)
    with_router = router is not None
    in_specs = [
        pl.BlockSpec((tm, V_W), lambda i: (i, 0)),
        pl.BlockSpec((tm, V_W), lambda i: (i, 0)),
        pl.BlockSpec((tm, 1024), lambda i: (i, 2)),
        pl.BlockSpec((tm, 1024), lambda i: (i, 3)),
        pl.BlockSpec((tm, 1024), lambda i: (i, 4)),
        pl.BlockSpec((tm, NA_W), lambda i: (i, 0)),
        pl.BlockSpec((tm, d), lambda i: (i, 0)),
        _resident(gog), _resident(wog), _resident(won), _resident(wout), _resident(gf),
    ]
    args = [of, ob, proj, proj, proj, na, x, gog, wog, won, wout, gf]
    out_specs = [pl.BlockSpec((tm, d), lambda i: (i, 0)), pl.BlockSpec((tm, d), lambda i: (i, 0))]
    out_shape = [jax.ShapeDtypeStruct((n, d), F32), jax.ShapeDtypeStruct((n, d), BF16)]
    if with_router:
        in_specs += [_resident(router[0]), _resident(router[1])]
        args += list(router)
        out_specs.append(pl.BlockSpec((tm, N_EXPERTS), lambda i: (i, 0)))
        out_shape.append(jax.ShapeDtypeStruct((n, N_EXPERTS), F32))
    return pl.pallas_call(
        functools.partial(_postmix_kernel, with_router=with_router),
        grid=(n // tm,),
        in_specs=in_specs,
        out_specs=out_specs,
        out_shape=out_shape,
        compiler_params=_params(("parallel",)),
        name="postmix_router" if with_router else "postmix",
    )(*args)


def _ffn_kernel(x1_ref, h_ref, wg_ref, wu_ref, wd_ref, o_ref, acc_ref):
    j = pl.program_id(1)

    @pl.when(j == 0)
    def _():
        acc_ref[...] = jnp.zeros_like(acc_ref)

    h = h_ref[...]
    g = _dot(h, wg_ref[...])
    u = _dot(h, wu_ref[...])
    a = (g * _sigmoid(g) * u).astype(BF16)
    acc_ref[...] += _dot(a, wd_ref[...])

    @pl.when(j == pl.num_programs(1) - 1)
    def _():
        o_ref[...] = x1_ref[...] + acc_ref[...]


def _ffn(x1, h2, wg, wu, wd, tm, tf):
    n, d = x1.shape
    dff = wg.shape[1]
    return pl.pallas_call(
        _ffn_kernel,
        grid=(n // tm, dff // tf),
        in_specs=[
            pl.BlockSpec((tm, d), lambda i, j: (i, 0)),
            pl.BlockSpec((tm, d), lambda i, j: (i, 0)),
            pl.BlockSpec((d, tf), lambda i, j: (0, j)),
            pl.BlockSpec((d, tf), lambda i, j: (0, j)),
            pl.BlockSpec((tf, d), lambda i, j: (j, 0)),
        ],
        out_specs=pl.BlockSpec((tm, d), lambda i, j: (i, 0)),
        out_shape=jax.ShapeDtypeStruct((n, d), F32),
        scratch_shapes=[pltpu.VMEM((tm, d), F32)],
        compiler_params=_params(("parallel", "arbitrary")),
        name="ffn",
    )(x1, h2, wg, wu, wd)


def _moe_kernel(x1_ref, h_ref, lg_ref, wg_ref, wu_ref, wd_ref, o_ref, acc_ref, comb_ref):
    e = pl.program_id(1)
    lane = lax.broadcasted_iota(jnp.int32, lg_ref.shape, 1).astype(F32)

    @pl.when(e == 0)
    def _():
        acc_ref[...] = jnp.zeros_like(acc_ref)
        lg = lg_ref[...]
        m1 = jnp.max(lg, axis=-1, keepdims=True)
        i1 = jnp.min(jnp.where(lg == m1, lane, float(N_EXPERTS)), axis=-1, keepdims=True)
        lg2 = jnp.where(lane == i1, -jnp.inf, lg)
        m2 = jnp.max(lg2, axis=-1, keepdims=True)
        i2 = jnp.min(jnp.where(lg2 == m2, lane, float(N_EXPERTS)), axis=-1, keepdims=True)
        t = jnp.exp(m2 - m1)
        w1 = 1.0 / (1.0 + t)
        comb_ref[...] = jnp.where(lane == i1, w1, 0.0) + jnp.where(lane == i2, t * w1, 0.0)

    ce = jnp.sum(jnp.where(lane == e.astype(F32), comb_ref[...], 0.0), axis=-1, keepdims=True)
    h = h_ref[...]
    g = _dot(h, wg_ref[0])
    u = _dot(h, wu_ref[0])
    a = (g * _sigmoid(g) * u * ce).astype(BF16)
    acc_ref[...] += _dot(a, wd_ref[0])

    @pl.when(e == pl.num_programs(1) - 1)
    def _():
        o_ref[...] = x1_ref[...] + acc_ref[...]


def _moe(x1, h2, logits, wg, wu, wd, tm):
    n, d = x1.shape
    ne, _, dfe = wg.shape
    return pl.pallas_call(
        _moe_kernel,
        grid=(n // tm, ne),
        in_specs=[
            pl.BlockSpec((tm, d), lambda i, e: (i, 0)),
            pl.BlockSpec((tm, d), lambda i, e: (i, 0)),
            pl.BlockSpec((tm, ne), lambda i, e: (i, 0)),
            pl.BlockSpec((1, d, dfe), lambda i, e: (e, 0, 0)),
            pl.BlockSpec((1, d, dfe), lambda i, e: (e, 0, 0)),
            pl.BlockSpec((1, dfe, d), lambda i, e: (e, 0, 0)),
        ],
        out_specs=pl.BlockSpec((tm, d), lambda i, e: (i, 0)),
        out_shape=jax.ShapeDtypeStruct((n, d), F32),
        scratch_shapes=[pltpu.VMEM((tm, d), F32), pltpu.VMEM((tm, ne), F32)],
        compiler_params=_params(("parallel", "arbitrary")),
        name="moe",
    )(x1, h2, logits, wg, wu, wd)


def _row_tile(n, want):
    t = min(n, want)
    assert n % t == 0
    return t


def _prep_layer(l, norm_mix, w_in, w_decay_f, b_decay_f, w_decay_b, b_decay_b, gla_out_norm, q_norm,
                k_norm, rpb, w_o_gla, w_o_na, w_out, norm_ffn):
    w = w_in[l]
    c0 = 2 * QK_W + 2 * V_W
    c1 = c0 + 2 * GLA_RANK
    c2 = c1 + 3 * NA_W
    w_main = jnp.concatenate([w[:, :QK_W] * (GLA_DK ** -0.5), w[:, QK_W:c0], w[:, c2:], w[:, c1:c2]],
                             axis=1).astype(BF16)
    wz = w[:, c0:c1].astype(BF16)
    zero = jnp.zeros((GLA_RANK, QK_W), F32)
    wdec = jnp.concatenate([jnp.concatenate([w_decay_f[l], zero], axis=1),
                            jnp.concatenate([zero, w_decay_b[l]], axis=1)], axis=0)
    wd_hi, wd_lo = _split_bf16(wdec)
    bdec = jnp.concatenate([b_decay_f[l], b_decay_b[l]])[None, :]
    qg = jnp.tile(q_norm[l] * (NA_DH ** -0.5), NA_HEADS)[None, :]
    kg = jnp.tile(k_norm[l], NA_HEADS)[None, :]
    return dict(
        g_mix=norm_mix[l][None, :], w_main=w_main, wz=wz, wd_hi=wd_hi, wd_lo=wd_lo, bdec=bdec,
        qg=qg, kg=kg, bias=_na_bias_table(rpb[l]),
        gog=gla_out_norm[l].reshape(1, V_W), wog=w_o_gla[l].astype(BF16),
        won=w_o_na[l].astype(BF16), wout=w_out[l].astype(BF16), g_ffn=norm_ffn[l][None, :])


def _trunk(x, layers, dense, moe):
    b, l, d = x.shape
    n = b * l
    xf = x.reshape(n, d)
    head_mean = jnp.asarray(np.kron(np.eye(NA_HEADS), np.full((NA_DH, NA_DH), 1.0 / NA_DH)), BF16)
    tb = _row_tile(l, 1024)
    for li, p in enumerate(layers):
        proj, la, lamin = _inproj(xf, p["g_mix"], p["w_main"], p["wz"], p["wd_hi"], p["wd_lo"], p["bdec"],
                           head_mean, p["qg"], p["kg"], _row_tile(n, 256))
        proj3 = proj.reshape(b, l, PROJ_W)
        la3 = la.reshape(b, l, 2 * QK_W)
        of = _gla(proj3, la3, lamin, False, tb).reshape(n, V_W)
        ob = _gla(proj3, la3, lamin, True, tb).reshape(n, V_W)
        na = _natten(proj3, p["bias"]).reshape(n, NA_W)
        if li % 2 == 0:
            wg, wu, wd = dense[li // 2]
            x1, h2 = _postmix(of, ob, proj, na, xf, p["gog"], p["wog"], p["won"], p["wout"],
                              p["g_ffn"], None, _row_tile(n, 512))
            xf = _ffn(x1, h2, wg, wu, wd, _row_tile(n, 1024), wg.shape[1] // 2)
        else:
            router, wg, wu, wd = moe[li // 2]
            x1, h2, logits = _postmix(of, ob, proj, na, xf, p["gog"], p["wog"], p["won"], p["wout"],
                                      p["g_ffn"], router, _row_tile(n, 512))
            xf = _moe(x1, h2, logits, wg, wu, wd, _row_tile(n, 512))
    return xf.reshape(b, l, d)


def kernel(x_prompt, x_sample, norm_mix, w_in, w_decay_f, b_decay_f, w_decay_b, b_decay_b, gla_out_norm,
           q_norm, k_norm, rpb, w_o_gla, w_o_na, w_out, norm_ffn, ffn_w_gate, ffn_w_up, ffn_w_down,
           moe_router, moe_w_gate, moe_w_up, moe_w_down):
    depth = w_in.shape[0]
    layers = [_prep_layer(l, norm_mix, w_in, w_decay_f, b_decay_f, w_decay_b, b_decay_b, gla_out_norm,
                          q_norm, k_norm, rpb, w_o_gla, w_o_na, w_out, norm_ffn) for l in range(depth)]
    dense = [(ffn_w_gate[j].astype(BF16), ffn_w_up[j].astype(BF16), ffn_w_down[j].astype(BF16))
             for j in range(ffn_w_gate.shape[0])]
    moe = [(_split_bf16(moe_router[j]), moe_w_gate[j].astype(BF16), moe_w_up[j].astype(BF16),
            moe_w_down[j].astype(BF16)) for j in range(moe_router.shape[0])]
    return (_trunk(x_prompt, layers, dense, moe), _trunk(x_sample, layers, dense, moe))
```

```python
import functools

import numpy as np
import jax
import jax.numpy as jnp
from jax import lax
from jax.experimental import pallas as pl
from jax.experimental.pallas import tpu as pltpu
from jax.experimental.pallas import tpu_sc as plsc

F32 = jnp.float32
BF16 = jnp.bfloat16

EPS = 1e-6
GRID_W = 64
GLA_HEADS = 4
GLA_DK = 128
GLA_DV = 256
GLA_CHUNK = 64
GLA_RANK = 16
GLA_TAU = 16.0
NA_HEADS = 8
NA_DH = 64
NA_KR = 8
NA_KC = 16
N_EXPERTS = 8
NEG = -1e30

VMEM_LIMIT = 56 * 1024 * 1024

QK_W = GLA_HEADS * GLA_DK
V_W = GLA_HEADS * GLA_DV
NA_W = NA_HEADS * NA_DH
PROJ_W = 2 * QK_W + 2 * V_W + 2 * 1024 + 3 * NA_W
CW = 512
NQ_CHUNK = (2 * QK_W + 2 * V_W + 2048) // CW
NK_CHUNK = NQ_CHUNK + 1


def _params(sem):
    return pltpu.CompilerParams(dimension_semantics=sem, vmem_limit_bytes=VMEM_LIMIT)


def _resident(a):
    nd = a.ndim
    return pl.BlockSpec(a.shape, lambda *_: (0,) * nd, pipeline_mode=pl.Buffered(1))


def _split_bf16(a):
    hi = a.astype(BF16)
    lo = (a - hi.astype(F32)).astype(BF16)
    return hi, lo


def _dot(a, b):
    return jnp.dot(a, b, preferred_element_type=F32)


def _dot_nt(a, b):
    return lax.dot_general(a, b, (((1,), (1,)), ((), ())), preferred_element_type=F32)


def _dot_tn(a, b):
    return lax.dot_general(a, b, (((0,), (0,)), ((), ())), preferred_element_type=F32)


def _sigmoid(x):
    return 1.0 / (1.0 + jnp.exp(-x))


def _rmsnorm(x, g):
    ms = jnp.mean(x * x, axis=-1, keepdims=True)
    return x * lax.rsqrt(ms + EPS) * g


def _inproj_kernel(x_ref, g_ref, w_ref, wz_ref, wdh_ref, wdl_ref, bdec_ref, hm_ref, qg_ref, kg_ref,
                   proj_ref, la_ref, lamin_ref):
    h = _rmsnorm(x_ref[...], g_ref[...]).astype(BF16)
    z = _dot(h, wz_ref[...])
    zh, zl = _split_bf16(z)
    y = _dot(zh, wdh_ref[...]) + _dot(zl, wdh_ref[...]) + _dot(zh, wdl_ref[...]) + bdec_ref[...]
    ls = jnp.minimum(y, 0.0) - jnp.log(1.0 + jnp.exp(-jnp.abs(y)))
    la = ls * (1.0 / GLA_TAU)
    la_ref[...] = la
    lamin_ref[0] = jnp.broadcast_to(jnp.min(la, axis=0, keepdims=True), lamin_ref.shape[1:])
    for c in range(PROJ_W // CW):
        acc = _dot(h, w_ref[:, c * CW:(c + 1) * CW])
        if c in (NQ_CHUNK, NK_CHUNK):
            gain = qg_ref if c == NQ_CHUNK else kg_ref
            ms = _dot((acc * acc).astype(BF16), hm_ref[...])
            acc = acc * lax.rsqrt(ms + EPS) * gain[...]
        proj_ref[:, c * CW:(c + 1) * CW] = acc.astype(BF16)


def _inproj(x, g, w_main, wz, wd_hi, wd_lo, bdec, hm, qg, kg, tm):
    n, d = x.shape
    return pl.pallas_call(
        _inproj_kernel,
        grid=(n // tm,),
        in_specs=[
            pl.BlockSpec((tm, d), lambda i: (i, 0)),
            _resident(g), _resident(w_main), _resident(wz), _resident(wd_hi), _resident(wd_lo),
            _resident(bdec), _resident(hm), _resident(qg), _resident(kg),
        ],
        out_specs=[
            pl.BlockSpec((tm, PROJ_W), lambda i: (i, 0)),
            pl.BlockSpec((tm, 2 * QK_W), lambda i: (i, 0)),
            pl.BlockSpec((1, 8, 2 * QK_W), lambda i: (i, 0, 0)),
        ],
        out_shape=[
            jax.ShapeDtypeStruct((n, PROJ_W), BF16),
            jax.ShapeDtypeStruct((n, 2 * QK_W), F32),
            jax.ShapeDtypeStruct((n // tm, 8, 2 * QK_W), F32),
        ],
        compiler_params=_params(("parallel",)),
        name="inproj",
    )(x, g, w_main, wz, wd_hi, wd_lo, bdec, hm, qg, kg)


N_LEVELS = 6
INTER_BLK = N_LEVELS
STATE_BLK = N_LEVELS + 1


def _decay_sum_matrix(reverse):
    c = GLA_CHUNK
    m_all = np.zeros((N_LEVELS + 2, c, c), np.float32)
    for l in range(N_LEVELS):
        m = 1 << l
        for p in range(c):
            mid = (p // (2 * m)) * 2 * m + m
            if p >= mid:
                m_all[l, p, mid + 1:p + 1] = 1.0
            else:
                m_all[l, p, p + 1:mid + 1] = 1.0
    for p in range(c):
        m_all[INTER_BLK, p, :p + 1] = 1.0
        m_all[STATE_BLK, p, p + 1:] = 1.0
    if reverse:
        m_all = m_all[:, ::-1, ::-1]
    return np.ascontiguousarray(m_all).reshape((N_LEVELS + 2) * c, c)


def _level_matrix(reverse):
    i = np.arange(GLA_CHUNK)[:, None]
    j = np.arange(GLA_CHUNK)[None, :]
    x = i ^ j
    lvl = np.where(x > 0, np.floor(np.log2(np.maximum(x, 1))), N_LEVELS).astype(np.int32)
    valid = (i <= j) if reverse else (i >= j)
    return np.where(valid, lvl, N_LEVELS + 1).astype(np.int32)


GLA_FAST_CHUNK = 128
GLA_FAST_MAX_DECAY = 60.0


def _tri_matrix(reverse, c):
    i = np.arange(c)[:, None]
    j = np.arange(c)[None, :]
    return ((j >= i) if reverse else (j <= i)).astype(np.float32)


def _gla_kernel(q_ref, k_ref, v_ref, la_ref, lamin_ref, m_ref, lvl_ref, tri_ref, o_ref,
                s_ref, qh_ref, kh_ref, ks_ref, sc_ref, dec_ref, *, reverse, tb):
    @pl.when(pl.program_id(1) == 0)
    def _():
        s_ref[...] = jnp.zeros_like(s_ref)

    fc = GLA_FAST_CHUNK
    n_fast = tb // fc
    total_row = 0 if reverse else fc - 1

    def fast_rows(ci):
        return pl.ds(pl.multiple_of(ci * fc, fc), fc)

    def fast_factors(ci, carry):
        rows = fast_rows(ci)
        la_hi, la_lo = _split_bf16(la_ref[0, rows, :])
        b = _dot(tri_ref[...], la_hi) + _dot(tri_ref[...], la_lo)
        total = b[total_row:total_row + 1, :]
        q = q_ref[0, rows, :].astype(F32)
        k = k_ref[0, rows, :].astype(F32)
        qh_ref[rows, :] = (q * jnp.exp(b)).astype(BF16)
        kh_ref[rows, :] = (k * jnp.exp(-b)).astype(BF16)
        ks_ref[rows, :] = (k * jnp.exp(total - b)).astype(BF16)
        dec_ref[pl.ds(pl.multiple_of(ci * 8, 8), 8), :] = jnp.broadcast_to(jnp.exp(total), (8, QK_W))
        return carry

    def fast_scores(ci, carry):
        rows = fast_rows(ci)
        i = lax.broadcasted_iota(jnp.int32, (fc, fc), 0)
        j = lax.broadcasted_iota(jnp.int32, (fc, fc), 1)
        causal = (i <= j) if reverse else (i >= j)
        for h in range(GLA_HEADS):
            hs = slice(h * GLA_DK, (h + 1) * GLA_DK)
            sc = _dot_nt(qh_ref[rows, hs], kh_ref[rows, hs])
            sc_ref[rows, hs] = jnp.where(causal, sc, 0.0).astype(BF16)
        return carry

    def fast_state(ci, carry):
        cc = (n_fast - 1 - ci) if reverse else ci
        rows = fast_rows(cc)
        dec = dec_ref[pl.ds(pl.multiple_of(cc * 8, 8), 8), :][:1, :]
        for h in range(GLA_HEADS):
            hs = slice(h * GLA_DK, (h + 1) * GLA_DK)
            vs = slice(h * GLA_DV, (h + 1) * GLA_DV)
            v = v_ref[0, rows, vs]
            st = s_ref[h]
            o = _dot(sc_ref[rows, hs], v) + _dot_nt(qh_ref[rows, hs], st.astype(BF16))
            o_ref[0, rows, vs] = o.astype(o_ref.dtype)
            s_ref[h] = st * dec[:, hs] + _dot_tn(v, ks_ref[rows, hs])
        return carry

    def robust_chunk(ci, carry):
        c = GLA_CHUNK
        n_chunks = tb // c
        lvl = lvl_ref[...]
        row = lax.broadcasted_iota(jnp.int32, (c, GLA_DK), 0)
        total_row = 0 if reverse else c - 1
        cc = (n_chunks - 1 - ci) if reverse else ci
        rows = pl.ds(pl.multiple_of(cc * c, c), c)
        la_hi, la_lo = _split_bf16(la_ref[0, rows, :])
        fac = jnp.exp(_dot(m_ref[...], la_hi) + _dot(m_ref[...], la_lo))
        for h in range(GLA_HEADS):
            hs = slice(h * GLA_DK, (h + 1) * GLA_DK)
            vs = slice(h * GLA_DV, (h + 1) * GLA_DV)
            q = q_ref[0, rows, hs].astype(F32)
            k = k_ref[0, rows, hs].astype(F32)
            v = v_ref[0, rows, vs]
            scores = jnp.zeros((c, c), F32)
            for l in range(N_LEVELS):
                is_query = ((row >> l) & 1) == (0 if reverse else 1)
                f_l = fac[l * c:(l + 1) * c, hs]
                g_l = (jnp.where(is_query, q, k) * f_l).astype(BF16)
                scores = jnp.where(lvl == l, _dot_nt(g_l, g_l), scores)
            scores = jnp.where(lvl == N_LEVELS, _dot_nt(q.astype(BF16), k.astype(BF16)), scores)
            f_in = fac[INTER_BLK * c:(INTER_BLK + 1) * c, hs]
            f_st = fac[STATE_BLK * c:(STATE_BLK + 1) * c, hs]
            st = s_ref[h]
            inter = _dot_nt((q * f_in).astype(BF16), st.astype(BF16))
            intra = _dot(scores.astype(BF16), v)
            o_ref[0, rows, vs] = (inter + intra).astype(o_ref.dtype)
            dec = f_in[total_row:total_row + 1, :]
            s_ref[h] = st * dec + _dot_tn(v, (k * f_st).astype(BF16))
        return carry

    bounded = jnp.min(lamin_ref[...]) >= -GLA_FAST_MAX_DECAY / GLA_FAST_CHUNK

    @pl.when(bounded)
    def _():
        lax.fori_loop(0, n_fast, fast_factors, 0, unroll=2)
        lax.fori_loop(0, n_fast, fast_scores, 0, unroll=2)
        lax.fori_loop(0, n_fast, fast_state, 0, unroll=2)

    @pl.when(jnp.logical_not(bounded))
    def _():
        lax.fori_loop(0, tb // GLA_CHUNK, robust_chunk, 0)


def _gla(proj3, la3, lamin, reverse, tb):
    b, l, _ = proj3.shape
    nb = l // tb
    tiles = lamin.shape[0] // (b * nb)
    blk = (lambda i: nb - 1 - i) if reverse else (lambda i: i)
    m_all = jnp.asarray(_decay_sum_matrix(reverse), BF16)
    lvl = jnp.asarray(_level_matrix(reverse))
    tri = jnp.asarray(_tri_matrix(reverse, GLA_FAST_CHUNK), BF16)
    kern = functools.partial(_gla_kernel, reverse=reverse, tb=tb)
    const = lambda s, i: (0, 0)
    return pl.pallas_call(
        kern,
        grid=(b, nb),
        in_specs=[
            pl.BlockSpec((1, tb, QK_W), lambda s, i: (s, blk(i), 0)),
            pl.BlockSpec((1, tb, QK_W), lambda s, i: (s, blk(i), 1)),
            pl.BlockSpec((1, tb, V_W), lambda s, i: (s, blk(i), 1)),
            pl.BlockSpec((1, tb, QK_W), lambda s, i: (s, blk(i), 1 if reverse else 0)),
            pl.BlockSpec((tiles, 8, QK_W), lambda s, i: (s * nb + blk(i), 0, 1 if reverse else 0)),
            _resident(m_all), _resident(lvl), _resident(tri),
        ],
        out_specs=pl.BlockSpec((1, tb, V_W), lambda s, i: (s, blk(i), 0)),
        out_shape=jax.ShapeDtypeStruct((b, l, V_W), BF16),
        scratch_shapes=[
            pltpu.VMEM((GLA_HEADS, GLA_DV, GLA_DK), F32),
            pltpu.VMEM((tb, QK_W), BF16),
            pltpu.VMEM((tb, QK_W), BF16),
            pltpu.VMEM((tb, QK_W), BF16),
            pltpu.VMEM((tb, QK_W), BF16),
            pltpu.VMEM((tb // GLA_FAST_CHUNK * 8, QK_W), F32),
        ],
        compiler_params=_params(("parallel", "arbitrary")),
        name="gla_bwd" if reverse else "gla_fwd",
    )(proj3, proj3, proj3, la3, lamin, m_all, lvl, tri)


NA_GROUP = 8
NA_BLK = NA_GROUP * GRID_W
NA_BAND = NA_KR * GRID_W


def _na_bias_table(rpb):
    qc = np.arange(GRID_W)[:, None]
    kc = np.arange(GRID_W)[None, :]
    col_start = np.clip(qc - NA_KC // 2, 0, GRID_W - NA_KC)
    valid = (kc >= col_start) & (kc < col_start + NA_KC)
    dc = np.clip(kc - qc, -(NA_KC - 1), NA_KC - 1) + NA_KC - 1
    onehot = ((dc[None] == np.arange(2 * NA_KC - 1)[:, None, None]) & valid[None]).astype(np.float32)
    mask = np.where(valid, 0.0, NEG).astype(np.float32)
    rows = jnp.stack([rpb[:, NA_KR - 1 - d:2 * NA_KR - 1 - d, :] for d in range(NA_KR)])
    t = jnp.einsum('dhkc,cqj->dhqkj', rows, jnp.asarray(onehot), precision=lax.Precision.HIGHEST)
    t = t + jnp.asarray(mask)[None, None, :, None, :]
    return t.reshape(NA_KR, NA_HEADS // 2, 2 * GRID_W, NA_BAND)


def _na_kernel(q_ref, kp_ref, kc_ref, kn_ref, vp_ref, vc_ref, vn_ref, bias_ref, o_ref,
               kwin_ref, vwin_ref, *, rows):
    g = pl.program_id(1)
    for j, (kr, vr) in enumerate(((kp_ref, vp_ref), (kc_ref, vc_ref), (kn_ref, vn_ref))):
        kwin_ref[j * NA_BLK:(j + 1) * NA_BLK, :] = kr[0]
        vwin_ref[j * NA_BLK:(j + 1) * NA_BLK, :] = vr[0]
    lane = lax.broadcasted_iota(jnp.int32, (GRID_W, 2 * NA_DH), 1)
    first = lane < NA_DH

    def one_row(rl, carry):
        r = g * NA_GROUP + rl
        rs = jnp.clip(r - NA_KR // 2, 0, rows - NA_KR)
        d = r - rs
        off = pl.multiple_of((rs - (g - 1) * NA_GROUP) * GRID_W, GRID_W)
        qrows = pl.ds(pl.multiple_of(rl * GRID_W, GRID_W), GRID_W)
        pairs = [slice(p * 2 * NA_DH, (p + 1) * 2 * NA_DH) for p in range(NA_HEADS // 2)]
        scores = []
        for p, ps in enumerate(pairs):
            qp = q_ref[0, qrows, ps]
            zero = jnp.zeros_like(qp)
            qs = jnp.concatenate([jnp.where(first, qp, zero), jnp.where(first, zero, qp)], axis=0)
            kb = kwin_ref[pl.ds(off, NA_BAND), ps]
            scores.append(_dot_nt(qs, kb) + bias_ref[d, p])
        probs = []
        for s in scores:
            e = jnp.exp(s - jnp.max(s, axis=-1, keepdims=True))
            probs.append((e.astype(BF16), jnp.sum(e, axis=-1, keepdims=True)))
        for (e, den), ps in zip(probs, pairs):
            vb = vwin_ref[pl.ds(off, NA_BAND), ps]
            pv = _dot(e, vb) / den
            o_ref[0, qrows, ps] = jnp.where(first, pv[:GRID_W], pv[GRID_W:]).astype(o_ref.dtype)
        return carry

    lax.fori_loop(0, NA_GROUP, one_row, 0, unroll=2)


def _natten(proj3, bias):
    b, l, _ = proj3.shape
    rows = l // GRID_W
    ng = rows // NA_GROUP
    qcol, kcol, vcol = NQ_CHUNK, NQ_CHUNK + 1, NQ_CHUNK + 2
    prev = lambda i: jnp.maximum(i - 1, 0)
    nxt = lambda i: jnp.minimum(i + 1, ng - 1)
    spec = lambda f, col: pl.BlockSpec((1, NA_BLK, NA_W), lambda s, i: (s, f(i), col))
    same = lambda i: i
    return pl.pallas_call(
        functools.partial(_na_kernel, rows=rows),
        grid=(b, ng),
        in_specs=[
            spec(same, qcol),
            spec(prev, kcol), spec(same, kcol), spec(nxt, kcol),
            spec(prev, vcol), spec(same, vcol), spec(nxt, vcol),
            _resident(bias),
        ],
        out_specs=pl.BlockSpec((1, NA_BLK, NA_W), lambda s, i: (s, i, 0)),
        out_shape=jax.ShapeDtypeStruct((b, l, NA_W), BF16),
        scratch_shapes=[pltpu.VMEM((3 * NA_BLK, NA_W), BF16), pltpu.VMEM((3 * NA_BLK, NA_W), BF16)],
        compiler_params=_params(("parallel", "parallel")),
        name="natten",
    )(proj3, proj3, proj3, proj3, proj3, proj3, proj3, bias)


def _postmix_kernel(*refs, with_router):
    if with_router:
        (of_ref, ob_ref, r_ref, ga_ref, gb_ref, na_ref, x_ref, gog_ref, wog_ref, won_ref, wout_ref,
         gf_ref, rh_ref, rl_ref, x1_ref, h2_ref, comb_ref, slot_ref) = refs
    else:
        (of_ref, ob_ref, r_ref, ga_ref, gb_ref, na_ref, x_ref, gog_ref, wog_ref, won_ref, wout_ref,
         gf_ref, x1_ref, h2_ref) = refs
    o = of_ref[...].astype(F32) + ob_ref[...].astype(F32)
    parts = []
    for h in range(GLA_HEADS):
        seg = o[:, h * GLA_DV:(h + 1) * GLA_DV]
        ms = jnp.mean(seg * seg, axis=-1, keepdims=True)
        parts.append(seg * lax.rsqrt(ms + EPS))
    r = r_ref[...].astype(F32)
    on = jnp.concatenate(parts, axis=-1) * gog_ref[...] * (r * _sigmoid(r))
    ya = _dot(on.astype(BF16), wog_ref[...])
    yb = _dot(na_ref[...], won_ref[...])
    merged = _sigmoid(ga_ref[...].astype(F32)) * ya + _sigmoid(gb_ref[...].astype(F32)) * yb
    x1 = x_ref[...] + _dot(merged.astype(BF16), wout_ref[...])
    x1_ref[...] = x1
    h2 = _rmsnorm(x1, gf_ref[...])
    h2_ref[...] = h2.astype(h2_ref.dtype)
    if with_router:
        hh, hl = _split_bf16(h2)
        lg = _dot(hh, rh_ref[...]) + _dot(hl, rh_ref[...]) + _dot(hh, rl_ref[...])
        lane = lax.broadcasted_iota(jnp.int32, lg.shape, 1).astype(F32)
        m1 = jnp.max(lg, axis=-1, keepdims=True)
        i1 = jnp.min(jnp.where(lg == m1, lane, float(N_EXPERTS)), axis=-1, keepdims=True)
        lg2 = jnp.where(lane == i1, -jnp.inf, lg)
        m2 = jnp.max(lg2, axis=-1, keepdims=True)
        i2 = jnp.min(jnp.where(lg2 == m2, lane, float(N_EXPERTS)), axis=-1, keepdims=True)
        t = jnp.exp(m2 - m1)
        w1 = 1.0 / (1.0 + t)
        comb_ref[...] = jnp.where(lane == i1, w1, 0.0) + jnp.where(lane == i2, t * w1, 0.0)
        slot_ref[...] = jnp.where(lane == i1, 1.0, 0.0) + jnp.where(lane == i2, 2.0, 0.0)


def _postmix(of, ob, proj, na, x, gog, wog, won, wout, gf, router, tm):
    n, d = x.shape
    with_router = router is not None
    in_specs = [
        pl.BlockSpec((tm, V_W), lambda i: (i, 0)),
        pl.BlockSpec((tm, V_W), lambda i: (i, 0)),
        pl.BlockSpec((tm, 1024), lambda i: (i, 2)),
        pl.BlockSpec((tm, 1024), lambda i: (i, 3)),
        pl.BlockSpec((tm, 1024), lambda i: (i, 4)),
        pl.BlockSpec((tm, NA_W), lambda i: (i, 0)),
        pl.BlockSpec((tm, d), lambda i: (i, 0)),
        _resident(gog), _resident(wog), _resident(won), _resident(wout), _resident(gf),
    ]
    args = [of, ob, proj, proj, proj, na, x, gog, wog, won, wout, gf]
    out_specs = [pl.BlockSpec((tm, d), lambda i: (i, 0)), pl.BlockSpec((tm, d), lambda i: (i, 0))]
    out_shape = [jax.ShapeDtypeStruct((n, d), F32), jax.ShapeDtypeStruct((n, d), F32 if with_router else BF16)]
    if with_router:
        in_specs += [_resident(router[0]), _resident(router[1])]
        args += list(router)
        for _ in range(2):
            out_specs.append(pl.BlockSpec((tm, N_EXPERTS), lambda i: (i, 0)))
            out_shape.append(jax.ShapeDtypeStruct((n, N_EXPERTS), F32))
    return pl.pallas_call(
        functools.partial(_postmix_kernel, with_router=with_router),
        grid=(n // tm,),
        in_specs=in_specs,
        out_specs=out_specs,
        out_shape=out_shape,
        compiler_params=_params(("parallel",)),
        name="postmix_router" if with_router else "postmix",
    )(*args)


def _ffn_kernel(x1_ref, h_ref, wg_ref, wu_ref, wd_ref, o_ref, acc_ref):
    j = pl.program_id(1)

    @pl.when(j == 0)
    def _():
        acc_ref[...] = jnp.zeros_like(acc_ref)

    h = h_ref[...]
    g = _dot(h, wg_ref[...])
    u = _dot(h, wu_ref[...])
    a = (g * _sigmoid(g) * u).astype(BF16)
    acc_ref[...] += _dot(a, wd_ref[...])

    @pl.when(j == pl.num_programs(1) - 1)
    def _():
        o_ref[...] = x1_ref[...] + acc_ref[...]


def _ffn(x1, h2, wg, wu, wd, tm, tf):
    n, d = x1.shape
    dff = wg.shape[1]
    return pl.pallas_call(
        _ffn_kernel,
        grid=(n // tm, dff // tf),
        in_specs=[
            pl.BlockSpec((tm, d), lambda i, j: (i, 0)),
            pl.BlockSpec((tm, d), lambda i, j: (i, 0)),
            pl.BlockSpec((d, tf), lambda i, j: (0, j)),
            pl.BlockSpec((d, tf), lambda i, j: (0, j)),
            pl.BlockSpec((tf, d), lambda i, j: (j, 0)),
        ],
        out_specs=pl.BlockSpec((tm, d), lambda i, j: (i, 0)),
        out_shape=jax.ShapeDtypeStruct((n, d), F32),
        scratch_shapes=[pltpu.VMEM((tm, d), F32)],
        compiler_params=_params(("parallel", "arbitrary")),
        name="ffn",
    )(x1, h2, wg, wu, wd)


SC_CORES = 2
SC_SUBCORES = 16
SC_WORKERS = SC_CORES * SC_SUBCORES
SC_CHUNK = 32
MOE_TM = 512


def _sc_gather_rows(table, idx):
    nrow, d = idx.shape[0], table.shape[1]
    assert nrow % (SC_WORKERS * SC_CHUNK) == 0
    per_worker = nrow // SC_WORKERS
    mesh = plsc.VectorSubcoreMesh(core_axis_name="c", subcore_axis_name="s",
                                  num_cores=SC_CORES, num_subcores=SC_SUBCORES)

    @functools.partial(
        pl.kernel, mesh=mesh, out_type=jax.ShapeDtypeStruct((nrow, d), table.dtype),
        scratch_types=[pltpu.VMEM((SC_CHUNK,), jnp.int32), pltpu.VMEM((SC_CHUNK, d), table.dtype),
                       pltpu.SemaphoreType.DMA])
    def gather(table_hbm, idx_hbm, out_hbm, idx_v, rows_v, sem):
        base = (lax.axis_index("s") * SC_CORES + lax.axis_index("c")) * per_worker

        @pl.loop(0, per_worker // SC_CHUNK)
        def _(j):
            off = base + j * SC_CHUNK
            pltpu.sync_copy(idx_hbm.at[pl.ds(off, SC_CHUNK)], idx_v)
            pltpu.async_copy(table_hbm.at[idx_v], rows_v, sem).wait()
            pltpu.sync_copy(rows_v, out_hbm.at[pl.ds(off, SC_CHUNK)])

    return gather(table, idx)


def _moe_plan(comb, slot):
    n = slot.shape[0]
    sel = (slot > 0).astype(jnp.int32)
    count = jnp.sum(sel, axis=0)
    group = (count + MOE_TM - 1) // MOE_TM * MOE_TM
    ends = jnp.cumsum(group)
    pos = (ends - group)[None, :] + jnp.cumsum(sel, axis=0) - sel
    pos12 = jnp.concatenate([jnp.sum(jnp.where(slot == 1.0, pos, 0), axis=1),
                             jnp.sum(jnp.where(slot == 2.0, pos, 0), axis=1)]).astype(jnp.int32)
    n_rows = 2 * n + N_EXPERTS * MOE_TM
    tok = jnp.tile(jnp.arange(n, dtype=jnp.int32), 2)
    src = jnp.zeros((n_rows,), jnp.int32).at[pos12].set(tok, unique_indices=True)
    tile_start = jnp.arange(n_rows // MOE_TM, dtype=jnp.int32) * MOE_TM
    tile_expert = jnp.minimum(jnp.sum(tile_start[:, None] >= ends[None, :], axis=1), N_EXPERTS - 1)
    n_used = (ends[-1] // MOE_TM).astype(jnp.int32).reshape(1)
    w12 = jnp.stack([jnp.sum(jnp.where(slot == 1.0, comb, 0.0), axis=1),
                     jnp.sum(jnp.where(slot == 2.0, comb, 0.0), axis=1)], axis=1)
    return src, pos12, tile_expert.astype(jnp.int32), n_used, w12


def _moe_group_kernel(te_ref, nu_ref, xs_ref, wg_ref, wu_ref, wd_ref, ys_ref):
    used = pl.program_id(0) < nu_ref[0]

    @pl.when(used)
    def _():
        x = xs_ref[...].astype(BF16)
        g = _dot(x, wg_ref[0])
        u = _dot(x, wu_ref[0])
        ys_ref[...] = _dot((g * _sigmoid(g) * u).astype(BF16), wd_ref[0])

    @pl.when(jnp.logical_not(used))
    def _():
        ys_ref[...] = jnp.zeros_like(ys_ref)


def _moe_grouped(xs, tile_expert, n_used, wg, wu, wd):
    p, d = xs.shape
    _, _, dfe = wg.shape
    grid_spec = pltpu.PrefetchScalarGridSpec(
        num_scalar_prefetch=2,
        grid=(p // MOE_TM,),
        in_specs=[
            pl.BlockSpec((MOE_TM, d), lambda t, te, nu: (t, 0)),
            pl.BlockSpec((1, d, dfe), lambda t, te, nu: (te[t], 0, 0)),
            pl.BlockSpec((1, d, dfe), lambda t, te, nu: (te[t], 0, 0)),
            pl.BlockSpec((1, dfe, d), lambda t, te, nu: (te[t], 0, 0)),
        ],
        out_specs=pl.BlockSpec((MOE_TM, d), lambda t, te, nu: (t, 0)),
    )
    return pl.pallas_call(
        _moe_group_kernel,
        grid_spec=grid_spec,
        out_shape=jax.ShapeDtypeStruct((p, d), F32),
        compiler_params=_params(("arbitrary",)),
        name="moe_grouped",
    )(tile_expert, n_used, xs, wg, wu, wd)


def _moe_combine_kernel(x1_ref, y1_ref, y2_ref, w_ref, o_ref):
    w = w_ref[...]
    o_ref[...] = x1_ref[...] + w[:, 0:1] * y1_ref[...] + w[:, 1:2] * y2_ref[...]


def _moe_combine(x1, y, w12, tm):
    n, d = x1.shape
    nt = n // tm
    return pl.pallas_call(
        _moe_combine_kernel,
        grid=(nt,),
        in_specs=[
            pl.BlockSpec((tm, d), lambda i: (i, 0)),
            pl.BlockSpec((tm, d), lambda i: (i, 0)),
            pl.BlockSpec((tm, d), lambda i: (i + nt, 0)),
            pl.BlockSpec((tm, 2), lambda i: (i, 0)),
        ],
        out_specs=pl.BlockSpec((tm, d), lambda i: (i, 0)),
        out_shape=jax.ShapeDtypeStruct((n, d), F32),
        compiler_params=_params(("parallel",)),
        name="moe_combine",
    )(x1, y, y, w12)


def _moe(x1, h2, comb, slot, wg, wu, wd):
    src, pos12, tile_expert, n_used, w12 = _moe_plan(comb, slot)
    xs = _sc_gather_rows(h2, src)
    ys = _moe_grouped(xs, tile_expert, n_used, wg, wu, wd)
    y = _sc_gather_rows(ys, pos12)
    return _moe_combine(x1, y, w12, _row_tile(x1.shape[0], 1024))


def _row_tile(n, want):
    t = min(n, want)
    assert n % t == 0
    return t


def _prep_layer(l, norm_mix, w_in, w_decay_f, b_decay_f, w_decay_b, b_decay_b, gla_out_norm, q_norm,
                k_norm, rpb, w_o_gla, w_o_na, w_out, norm_ffn):
    w = w_in[l]
    c0 = 2 * QK_W + 2 * V_W
    c1 = c0 + 2 * GLA_RANK
    c2 = c1 + 3 * NA_W
    w_main = jnp.concatenate([w[:, :QK_W] * (GLA_DK ** -0.5), w[:, QK_W:c0], w[:, c2:], w[:, c1:c2]],
                             axis=1).astype(BF16)
    wz = w[:, c0:c1].astype(BF16)
    zero = jnp.zeros((GLA_RANK, QK_W), F32)
    wdec = jnp.concatenate([jnp.concatenate([w_decay_f[l], zero], axis=1),
                            jnp.concatenate([zero, w_decay_b[l]], axis=1)], axis=0)
    wd_hi, wd_lo = _split_bf16(wdec)
    bdec = jnp.concatenate([b_decay_f[l], b_decay_b[l]])[None, :]
    qg = jnp.tile(q_norm[l] * (NA_DH ** -0.5), NA_HEADS)[None, :]
    kg = jnp.tile(k_norm[l], NA_HEADS)[None, :]
    return dict(
        g_mix=norm_mix[l][None, :], w_main=w_main, wz=wz, wd_hi=wd_hi, wd_lo=wd_lo, bdec=bdec,
        qg=qg, kg=kg, bias=_na_bias_table(rpb[l]),
        gog=gla_out_norm[l].reshape(1, V_W), wog=w_o_gla[l].astype(BF16),
        won=w_o_na[l].astype(BF16), wout=w_out[l].astype(BF16), g_ffn=norm_ffn[l][None, :])


def _trunk(x, layers, dense, moe):
    b, l, d = x.shape
    n = b * l
    xf = x.reshape(n, d)
    head_mean = jnp.asarray(np.kron(np.eye(NA_HEADS), np.full((NA_DH, NA_DH), 1.0 / NA_DH)), BF16)
    tb = _row_tile(l, 1024)
    for li, p in enumerate(layers):
        proj, la, lamin = _inproj(xf, p["g_mix"], p["w_main"], p["wz"], p["wd_hi"], p["wd_lo"], p["bdec"],
                           head_mean, p["qg"], p["kg"], _row_tile(n, 256))
        proj3 = proj.reshape(b, l, PROJ_W)
        la3 = la.reshape(b, l, 2 * QK_W)
        of = _gla(proj3, la3, lamin, False, tb).reshape(n, V_W)
        ob = _gla(proj3, la3, lamin, True, tb).reshape(n, V_W)
        na = _natten(proj3, p["bias"]).reshape(n, NA_W)
        if li % 2 == 0:
            wg, wu, wd = dense[li // 2]
            x1, h2 = _postmix(of, ob, proj, na, xf, p["gog"], p["wog"], p["won"], p["wout"],
                              p["g_ffn"], None, _row_tile(n, 512))
            xf = _ffn(x1, h2, wg, wu, wd, _row_tile(n, 1024), wg.shape[1] // 2)
        else:
            router, wg, wu, wd = moe[li // 2]
            x1, h2, comb, slot = _postmix(of, ob, proj, na, xf, p["gog"], p["wog"], p["won"], p["wout"],
                                          p["g_ffn"], router, _row_tile(n, 512))
            xf = _moe(x1, h2, comb, slot, wg, wu, wd)
    return xf.reshape(b, l, d)


def kernel(x_prompt, x_sample, norm_mix, w_in, w_decay_f, b_decay_f, w_decay_b, b_decay_b, gla_out_norm,
           q_norm, k_norm, rpb, w_o_gla, w_o_na, w_out, norm_ffn, ffn_w_gate, ffn_w_up, ffn_w_down,
           moe_router, moe_w_gate, moe_w_up, moe_w_down):
    depth = w_in.shape[0]
    layers = [_prep_layer(l, norm_mix, w_in, w_decay_f, b_decay_f, w_decay_b, b_decay_b, gla_out_norm,
                          q_norm, k_norm, rpb, w_o_gla, w_o_na, w_out, norm_ffn) for l in range(depth)]
    dense = [(ffn_w_gate[j].astype(BF16), ffn_w_up[j].astype(BF16), ffn_w_down[j].astype(BF16))
             for j in range(ffn_w_gate.shape[0])]
    moe = [(_split_bf16(moe_router[j]), moe_w_gate[j].astype(BF16), moe_w_up[j].astype(BF16),
            moe_w_down[j].astype(BF16)) for j in range(moe_router.shape[0])]
    return (_trunk(x_prompt, layers, dense, moe), _trunk(x_sample, layers, dense, moe))
```

```python
import functools

import numpy as np
import jax
import jax.numpy as jnp
from jax import lax
from jax.experimental import pallas as pl
from jax.experimental.pallas import tpu as pltpu
from jax.experimental.pallas import tpu_sc as plsc

F32 = jnp.float32
BF16 = jnp.bfloat16

EPS = 1e-6
GRID_W = 64
GLA_HEADS = 4
GLA_DK = 128
GLA_DV = 256
GLA_CHUNK = 64
GLA_RANK = 16
GLA_TAU = 16.0
NA_HEADS = 8
NA_DH = 64
NA_KR = 8
NA_KC = 16
N_EXPERTS = 8
NEG = -1e30

VMEM_LIMIT = 56 * 1024 * 1024

QK_W = GLA_HEADS * GLA_DK
V_W = GLA_HEADS * GLA_DV
NA_W = NA_HEADS * NA_DH
PROJ_W = 2 * QK_W + 2 * V_W + 2 * 1024 + 3 * NA_W
CW = 512
NQ_CHUNK = (2 * QK_W + 2 * V_W + 2048) // CW
NK_CHUNK = NQ_CHUNK + 1


def _params(sem):
    return pltpu.CompilerParams(dimension_semantics=sem, vmem_limit_bytes=VMEM_LIMIT)


def _resident(a):
    nd = a.ndim
    return pl.BlockSpec(a.shape, lambda *_: (0,) * nd, pipeline_mode=pl.Buffered(1))


def _split_bf16(a):
    hi = a.astype(BF16)
    lo = (a - hi.astype(F32)).astype(BF16)
    return hi, lo


def _dot(a, b):
    return jnp.dot(a, b, preferred_element_type=F32)


def _dot_nt(a, b):
    return lax.dot_general(a, b, (((1,), (1,)), ((), ())), preferred_element_type=F32)


def _dot_tn(a, b):
    return lax.dot_general(a, b, (((0,), (0,)), ((), ())), preferred_element_type=F32)


def _sigmoid(x):
    return 1.0 / (1.0 + jnp.exp(-x))


def _rmsnorm(x, g):
    ms = jnp.mean(x * x, axis=-1, keepdims=True)
    return x * lax.rsqrt(ms + EPS) * g


def _inproj_kernel(x_ref, g_ref, w_ref, wz_ref, wdh_ref, wdl_ref, bdec_ref, hm_ref, qg_ref, kg_ref,
                   proj_ref, la_ref, lamin_ref):
    h = _rmsnorm(x_ref[...], g_ref[...]).astype(BF16)
    z = _dot(h, wz_ref[...])
    zh, zl = _split_bf16(z)
    y = _dot(zh, wdh_ref[...]) + _dot(zl, wdh_ref[...]) + _dot(zh, wdl_ref[...]) + bdec_ref[...]
    ls = jnp.minimum(y, 0.0) - jnp.log(1.0 + jnp.exp(-jnp.abs(y)))
    la = ls * (1.0 / GLA_TAU)
    la_ref[...] = la
    lamin_ref[0] = jnp.broadcast_to(jnp.min(la, axis=0, keepdims=True), lamin_ref.shape[1:])
    for c in range(PROJ_W // CW):
        acc = _dot(h, w_ref[:, c * CW:(c + 1) * CW])
        if c in (NQ_CHUNK, NK_CHUNK):
            gain = qg_ref if c == NQ_CHUNK else kg_ref
            ms = _dot((acc * acc).astype(BF16), hm_ref[...])
            acc = acc * lax.rsqrt(ms + EPS) * gain[...]
        proj_ref[:, c * CW:(c + 1) * CW] = acc.astype(BF16)


def _inproj(x, g, w_main, wz, wd_hi, wd_lo, bdec, hm, qg, kg, tm):
    n, d = x.shape
    return pl.pallas_call(
        _inproj_kernel,
        grid=(n // tm,),
        in_specs=[
            pl.BlockSpec((tm, d), lambda i: (i, 0)),
            _resident(g), _resident(w_main), _resident(wz), _resident(wd_hi), _resident(wd_lo),
            _resident(bdec), _resident(hm), _resident(qg), _resident(kg),
        ],
        out_specs=[
            pl.BlockSpec((tm, PROJ_W), lambda i: (i, 0)),
            pl.BlockSpec((tm, 2 * QK_W), lambda i: (i, 0)),
            pl.BlockSpec((1, 8, 2 * QK_W), lambda i: (i, 0, 0)),
        ],
        out_shape=[
            jax.ShapeDtypeStruct((n, PROJ_W), BF16),
            jax.ShapeDtypeStruct((n, 2 * QK_W), F32),
            jax.ShapeDtypeStruct((n // tm, 8, 2 * QK_W), F32),
        ],
        compiler_params=_params(("parallel",)),
        name="inproj",
    )(x, g, w_main, wz, wd_hi, wd_lo, bdec, hm, qg, kg)


N_LEVELS = 6
INTER_BLK = N_LEVELS
STATE_BLK = N_LEVELS + 1


def _decay_sum_matrix(reverse):
    c = GLA_CHUNK
    m_all = np.zeros((N_LEVELS + 2, c, c), np.float32)
    for l in range(N_LEVELS):
        m = 1 << l
        for p in range(c):
            mid = (p // (2 * m)) * 2 * m + m
            if p >= mid:
                m_all[l, p, mid + 1:p + 1] = 1.0
            else:
                m_all[l, p, p + 1:mid + 1] = 1.0
    for p in range(c):
        m_all[INTER_BLK, p, :p + 1] = 1.0
        m_all[STATE_BLK, p, p + 1:] = 1.0
    if reverse:
        m_all = m_all[:, ::-1, ::-1]
    return np.ascontiguousarray(m_all).reshape((N_LEVELS + 2) * c, c)


def _level_matrix(reverse):
    i = np.arange(GLA_CHUNK)[:, None]
    j = np.arange(GLA_CHUNK)[None, :]
    x = i ^ j
    lvl = np.where(x > 0, np.floor(np.log2(np.maximum(x, 1))), N_LEVELS).astype(np.int32)
    valid = (i <= j) if reverse else (i >= j)
    return np.where(valid, lvl, N_LEVELS + 1).astype(np.int32)


GLA_FAST_CHUNK = 128
GLA_FAST_MAX_DECAY = 60.0


def _tri_matrix(reverse, c):
    i = np.arange(c)[:, None]
    j = np.arange(c)[None, :]
    return ((j >= i) if reverse else (j <= i)).astype(np.float32)


def _decay_columns(dec_row):
    t = jnp.broadcast_to(dec_row, (GLA_DK, GLA_DK)).T
    return jnp.concatenate([t, t], axis=1)


def _gla_kernel(q_ref, k_ref, v_ref, la_ref, lamin_ref, m_ref, lvl_ref, tri_ref, o_ref,
                s_ref, lr_ref, qh_ref, kh_ref, ks_ref, dec_ref, *, reverse, tb):
    @pl.when(pl.program_id(1) == 0)
    def _():
        s_ref[...] = jnp.zeros_like(s_ref)

    fc = GLA_FAST_CHUNK
    n_fast = tb // fc
    total_row = 0 if reverse else fc - 1

    def fast_rows(ci):
        return pl.ds(pl.multiple_of(ci * fc, fc), fc)

    def sc_lanes(h):
        return slice(h * 2 * GLA_DK, h * 2 * GLA_DK + GLA_DK)

    def qh_lanes(h):
        return slice(h * 2 * GLA_DK + GLA_DK, (h + 1) * 2 * GLA_DK)

    def fast_factors(ci, carry):
        rows = fast_rows(ci)
        la_hi, la_lo = _split_bf16(la_ref[0, rows, :])
        b = _dot(tri_ref[...], la_hi) + _dot(tri_ref[...], la_lo)
        dec = jnp.exp(b[total_row:total_row + 1, :])
        f_q = jnp.exp(b)
        qh = (q_ref[0, rows, :].astype(F32) * f_q).astype(BF16)
        kh = k_ref[0, rows, :].astype(F32) / f_q
        qh_ref[rows, :] = qh
        for h in range(GLA_HEADS):
            lr_ref[rows, qh_lanes(h)] = qh[:, h * GLA_DK:(h + 1) * GLA_DK]
        kh_ref[rows, :] = kh.astype(BF16)
        ks_ref[rows, :] = (kh * dec).astype(BF16)
        dec_ref[pl.ds(pl.multiple_of(ci * 8, 8), 8), :] = jnp.broadcast_to(dec, (8, QK_W))
        return carry

    def fast_scores(ci, carry):
        rows = fast_rows(ci)
        i = lax.broadcasted_iota(jnp.int32, (fc, fc), 0)
        j = lax.broadcasted_iota(jnp.int32, (fc, fc), 1)
        causal = (i <= j) if reverse else (i >= j)
        for h in range(GLA_HEADS):
            hs = slice(h * GLA_DK, (h + 1) * GLA_DK)
            sc = _dot_nt(qh_ref[rows, hs], kh_ref[rows, hs])
            lr_ref[rows, sc_lanes(h)] = jnp.where(causal, sc, 0.0).astype(BF16)
        return carry

    def fast_state(ci, carry):
        cc = (n_fast - 1 - ci) if reverse else ci
        rows = fast_rows(cc)
        dec = dec_ref[pl.ds(pl.multiple_of(cc * 8, 8), 8), :][:1, :]
        vals = [v_ref[0, rows, h * GLA_DV:(h + 1) * GLA_DV] for h in range(GLA_HEADS)]
        kv = [_dot_tn(ks_ref[rows, h * GLA_DK:(h + 1) * GLA_DK], vals[h]) for h in range(GLA_HEADS)]
        for h in range(GLA_HEADS):
            hs = slice(h * GLA_DK, (h + 1) * GLA_DK)
            st = s_ref[h]
            o = _dot(lr_ref[rows, h * 2 * GLA_DK:(h + 1) * 2 * GLA_DK],
                     jnp.concatenate([vals[h], st.astype(BF16)], axis=0))
            o_ref[0, rows, h * GLA_DV:(h + 1) * GLA_DV] = o.astype(o_ref.dtype)
            s_ref[h] = st * _decay_columns(dec[:, hs]) + kv[h]
        return carry

    def robust_chunk(ci, carry):
        c = GLA_CHUNK
        n_chunks = tb // c
        lvl = lvl_ref[...]
        row = lax.broadcasted_iota(jnp.int32, (c, GLA_DK), 0)
        total_row = 0 if reverse else c - 1
        cc = (n_chunks - 1 - ci) if reverse else ci
        rows = pl.ds(pl.multiple_of(cc * c, c), c)
        la_hi, la_lo = _split_bf16(la_ref[0, rows, :])
        fac = jnp.exp(_dot(m_ref[...], la_hi) + _dot(m_ref[...], la_lo))
        for h in range(GLA_HEADS):
            hs = slice(h * GLA_DK, (h + 1) * GLA_DK)
            vs = slice(h * GLA_DV, (h + 1) * GLA_DV)
            q = q_ref[0, rows, hs].astype(F32)
            k = k_ref[0, rows, hs].astype(F32)
            v = v_ref[0, rows, vs]
            scores = jnp.zeros((c, c), F32)
            for l in range(N_LEVELS):
                is_query = ((row >> l) & 1) == (0 if reverse else 1)
                f_l = fac[l * c:(l + 1) * c, hs]
                g_l = (jnp.where(is_query, q, k) * f_l).astype(BF16)
                scores = jnp.where(lvl == l, _dot_nt(g_l, g_l), scores)
            scores = jnp.where(lvl == N_LEVELS, _dot_nt(q.astype(BF16), k.astype(BF16)), scores)
            f_in = fac[INTER_BLK * c:(INTER_BLK + 1) * c, hs]
            f_st = fac[STATE_BLK * c:(STATE_BLK + 1) * c, hs]
            st = s_ref[h]
            inter = _dot((q * f_in).astype(BF16), st.astype(BF16))
            intra = _dot(scores.astype(BF16), v)
            o_ref[0, rows, vs] = (inter + intra).astype(o_ref.dtype)
            dec = f_in[total_row:total_row + 1, :]
            s_ref[h] = st * _decay_columns(dec) + _dot_tn((k * f_st).astype(BF16), v)
        return carry

    bounded = jnp.min(lamin_ref[...]) >= -GLA_FAST_MAX_DECAY / GLA_FAST_CHUNK

    @pl.when(bounded)
    def _():
        lax.fori_loop(0, n_fast, fast_factors, 0, unroll=2)
        lax.fori_loop(0, n_fast, fast_scores, 0, unroll=4)
        lax.fori_loop(0, n_fast, fast_state, 0, unroll=2)

    @pl.when(jnp.logical_not(bounded))
    def _():
        lax.fori_loop(0, tb // GLA_CHUNK, robust_chunk, 0)


def _gla(proj3, la3, lamin, reverse, tb):
    b, l, _ = proj3.shape
    nb = l // tb
    tiles = lamin.shape[0] // (b * nb)
    blk = (lambda i: nb - 1 - i) if reverse else (lambda i: i)
    m_all = jnp.asarray(_decay_sum_matrix(reverse), BF16)
    lvl = jnp.asarray(_level_matrix(reverse))
    tri = jnp.asarray(_tri_matrix(reverse, GLA_FAST_CHUNK), BF16)
    kern = functools.partial(_gla_kernel, reverse=reverse, tb=tb)
    const = lambda s, i: (0, 0)
    return pl.pallas_call(
        kern,
        grid=(b, nb),
        in_specs=[
            pl.BlockSpec((1, tb, QK_W), lambda s, i: (s, blk(i), 0)),
            pl.BlockSpec((1, tb, QK_W), lambda s, i: (s, blk(i), 1)),
            pl.BlockSpec((1, tb, V_W), lambda s, i: (s, blk(i), 1)),
            pl.BlockSpec((1, tb, QK_W), lambda s, i: (s, blk(i), 1 if reverse else 0)),
            pl.BlockSpec((tiles, 8, QK_W), lambda s, i: (s * nb + blk(i), 0, 1 if reverse else 0)),
            _resident(m_all), _resident(lvl), _resident(tri),
        ],
        out_specs=pl.BlockSpec((1, tb, V_W), lambda s, i: (s, blk(i), 0)),
        out_shape=jax.ShapeDtypeStruct((b, l, V_W), BF16),
        scratch_shapes=[
            pltpu.VMEM((GLA_HEADS, GLA_DK, GLA_DV), F32),
            pltpu.VMEM((tb, 2 * QK_W), BF16),
            pltpu.VMEM((tb, QK_W), BF16),
            pltpu.VMEM((tb, QK_W), BF16),
            pltpu.VMEM((tb, QK_W), BF16),
            pltpu.VMEM((tb // GLA_FAST_CHUNK * 8, QK_W), F32),
        ],
        compiler_params=_params(("parallel", "arbitrary")),
        name="gla_bwd" if reverse else "gla_fwd",
    )(proj3, proj3, proj3, la3, lamin, m_all, lvl, tri)


NA_GROUP = 8
NA_BLK = NA_GROUP * GRID_W
NA_BAND = NA_KR * GRID_W


def _na_bias_table(rpb):
    qc = np.arange(GRID_W)[:, None]
    kc = np.arange(GRID_W)[None, :]
    col_start = np.clip(qc - NA_KC // 2, 0, GRID_W - NA_KC)
    valid = (kc >= col_start) & (kc < col_start + NA_KC)
    dc = np.clip(kc - qc, -(NA_KC - 1), NA_KC - 1) + NA_KC - 1
    onehot = ((dc[None] == np.arange(2 * NA_KC - 1)[:, None, None]) & valid[None]).astype(np.float32)
    mask = np.where(valid, 0.0, NEG).astype(np.float32)
    rows = jnp.stack([rpb[:, NA_KR - 1 - d:2 * NA_KR - 1 - d, :] for d in range(NA_KR)])
    t = jnp.einsum('dhkc,cqj->dhqkj', rows, jnp.asarray(onehot), precision=lax.Precision.HIGHEST)
    t = t + jnp.asarray(mask)[None, None, :, None, :]
    return t.reshape(NA_KR, NA_HEADS // 2, 2 * GRID_W, NA_BAND)


def _na_kernel(q_ref, kp_ref, kc_ref, kn_ref, vp_ref, vc_ref, vn_ref, bias_ref, o_ref,
               kwin_ref, vwin_ref, *, rows):
    g = pl.program_id(1)
    for j, (kr, vr) in enumerate(((kp_ref, vp_ref), (kc_ref, vc_ref), (kn_ref, vn_ref))):
        kwin_ref[j * NA_BLK:(j + 1) * NA_BLK, :] = kr[0]
        vwin_ref[j * NA_BLK:(j + 1) * NA_BLK, :] = vr[0]
    lane = lax.broadcasted_iota(jnp.int32, (GRID_W, 2 * NA_DH), 1)
    first = lane < NA_DH

    def one_row(rl, carry):
        r = g * NA_GROUP + rl
        rs = jnp.clip(r - NA_KR // 2, 0, rows - NA_KR)
        d = r - rs
        off = pl.multiple_of((rs - (g - 1) * NA_GROUP) * GRID_W, GRID_W)
        qrows = pl.ds(pl.multiple_of(rl * GRID_W, GRID_W), GRID_W)
        pairs = [slice(p * 2 * NA_DH, (p + 1) * 2 * NA_DH) for p in range(NA_HEADS // 2)]
        scores = []
        for p, ps in enumerate(pairs):
            qp = q_ref[0, qrows, ps]
            zero = jnp.zeros_like(qp)
            qs = jnp.concatenate([jnp.where(first, qp, zero), jnp.where(first, zero, qp)], axis=0)
            kb = kwin_ref[pl.ds(off, NA_BAND), ps]
            scores.append(_dot_nt(qs, kb) + bias_ref[d, p])
        probs = []
        for s in scores:
            e = jnp.exp(s - jnp.max(s, axis=-1, keepdims=True))
            probs.append((e.astype(BF16), jnp.sum(e, axis=-1, keepdims=True)))
        for (e, den), ps in zip(probs, pairs):
            vb = vwin_ref[pl.ds(off, NA_BAND), ps]
            pv = _dot(e, vb) / den
            o_ref[0, qrows, ps] = jnp.where(first, pv[:GRID_W], pv[GRID_W:]).astype(o_ref.dtype)
        return carry

    lax.fori_loop(0, NA_GROUP, one_row, 0, unroll=4)


def _natten(proj3, bias):
    b, l, _ = proj3.shape
    rows = l // GRID_W
    ng = rows // NA_GROUP
    qcol, kcol, vcol = NQ_CHUNK, NQ_CHUNK + 1, NQ_CHUNK + 2
    prev = lambda i: jnp.maximum(i - 1, 0)
    nxt = lambda i: jnp.minimum(i + 1, ng - 1)
    spec = lambda f, col: pl.BlockSpec((1, NA_BLK, NA_W), lambda s, i: (s, f(i), col))
    same = lambda i: i
    return pl.pallas_call(
        functools.partial(_na_kernel, rows=rows),
        grid=(b, ng),
        in_specs=[
            spec(same, qcol),
            spec(prev, kcol), spec(same, kcol), spec(nxt, kcol),
            spec(prev, vcol), spec(same, vcol), spec(nxt, vcol),
            _resident(bias),
        ],
        out_specs=pl.BlockSpec((1, NA_BLK, NA_W), lambda s, i: (s, i, 0)),
        out_shape=jax.ShapeDtypeStruct((b, l, NA_W), BF16),
        scratch_shapes=[pltpu.VMEM((3 * NA_BLK, NA_W), BF16), pltpu.VMEM((3 * NA_BLK, NA_W), BF16)],
        compiler_params=_params(("parallel", "parallel")),
        name="natten",
    )(proj3, proj3, proj3, proj3, proj3, proj3, proj3, bias)


def _postmix_kernel(*refs, with_router):
    if with_router:
        (of_ref, ob_ref, r_ref, ga_ref, gb_ref, na_ref, x_ref, gog_ref, wog_ref, won_ref, wout_ref,
         gf_ref, rh_ref, rl_ref, x1_ref, h2_ref, comb_ref, slot_ref) = refs
    else:
        (of_ref, ob_ref, r_ref, ga_ref, gb_ref, na_ref, x_ref, gog_ref, wog_ref, won_ref, wout_ref,
         gf_ref, x1_ref, h2_ref) = refs
    o = of_ref[...].astype(F32) + ob_ref[...].astype(F32)
    parts = []
    for h in range(GLA_HEADS):
        seg = o[:, h * GLA_DV:(h + 1) * GLA_DV]
        ms = jnp.mean(seg * seg, axis=-1, keepdims=True)
        parts.append(seg * lax.rsqrt(ms + EPS))
    r = r_ref[...].astype(F32)
    on = jnp.concatenate(parts, axis=-1) * gog_ref[...] * (r * _sigmoid(r))
    ya = _dot(on.astype(BF16), wog_ref[...])
    yb = _dot(na_ref[...], won_ref[...])
    merged = _sigmoid(ga_ref[...].astype(F32)) * ya + _sigmoid(gb_ref[...].astype(F32)) * yb
    x1 = x_ref[...] + _dot(merged.astype(BF16), wout_ref[...])
    x1_ref[...] = x1
    h2 = _rmsnorm(x1, gf_ref[...])
    h2_ref[...] = h2.astype(h2_ref.dtype)
    if with_router:
        hh, hl = _split_bf16(h2)
        lg = _dot(hh, rh_ref[...]) + _dot(hl, rh_ref[...]) + _dot(hh, rl_ref[...])
        lane = lax.broadcasted_iota(jnp.int32, lg.shape, 1).astype(F32)
        m1 = jnp.max(lg, axis=-1, keepdims=True)
        i1 = jnp.min(jnp.where(lg == m1, lane, float(N_EXPERTS)), axis=-1, keepdims=True)
        lg2 = jnp.where(lane == i1, -jnp.inf, lg)
        m2 = jnp.max(lg2, axis=-1, keepdims=True)
        i2 = jnp.min(jnp.where(lg2 == m2, lane, float(N_EXPERTS)), axis=-1, keepdims=True)
        t = jnp.exp(m2 - m1)
        w1 = 1.0 / (1.0 + t)
        comb_ref[...] = jnp.where(lane == i1, w1, 0.0) + jnp.where(lane == i2, t * w1, 0.0)
        slot_ref[...] = jnp.where(lane == i1, 1.0, 0.0) + jnp.where(lane == i2, 2.0, 0.0)


def _postmix(of, ob, proj, na, x, gog, wog, won, wout, gf, router, tm):
    n, d = x.shape
    with_router = router is not None
    in_specs = [
        pl.BlockSpec((tm, V_W), lambda i: (i, 0)),
        pl.BlockSpec((tm, V_W), lambda i: (i, 0)),
        pl.BlockSpec((tm, 1024), lambda i: (i, 2)),
        pl.BlockSpec((tm, 1024), lambda i: (i, 3)),
        pl.BlockSpec((tm, 1024), lambda i: (i, 4)),
        pl.BlockSpec((tm, NA_W), lambda i: (i, 0)),
        pl.BlockSpec((tm, d), lambda i: (i, 0)),
        _resident(gog), _resident(wog), _resident(won), _resident(wout), _resident(gf),
    ]
    args = [of, ob, proj, proj, proj, na, x, gog, wog, won, wout, gf]
    out_specs = [pl.BlockSpec((tm, d), lambda i: (i, 0)), pl.BlockSpec((tm, d), lambda i: (i, 0))]
    out_shape = [jax.ShapeDtypeStruct((n, d), F32), jax.ShapeDtypeStruct((n, d), F32 if with_router else BF16)]
    if with_router:
        in_specs += [_resident(router[0]), _resident(router[1])]
        args += list(router)
        for _ in range(2):
            out_specs.append(pl.BlockSpec((tm, N_EXPERTS), lambda i: (i, 0)))
            out_shape.append(jax.ShapeDtypeStruct((n, N_EXPERTS), F32))
    return pl.pallas_call(
        functools.partial(_postmix_kernel, with_router=with_router),
        grid=(n // tm,),
        in_specs=in_specs,
        out_specs=out_specs,
        out_shape=out_shape,
        compiler_params=_params(("parallel",)),
        name="postmix_router" if with_router else "postmix",
    )(*args)


def _ffn_kernel(x1_ref, h_ref, wg_ref, wu_ref, wd_ref, o_ref, acc_ref):
    j = pl.program_id(1)

    @pl.when(j == 0)
    def _():
        acc_ref[...] = jnp.zeros_like(acc_ref)

    h = h_ref[...]
    g = _dot(h, wg_ref[...])
    u = _dot(h, wu_ref[...])
    a = (g * _sigmoid(g) * u).astype(BF16)
    acc_ref[...] += _dot(a, wd_ref[...])

    @pl.when(j == pl.num_programs(1) - 1)
    def _():
        o_ref[...] = x1_ref[...] + acc_ref[...]


def _ffn(x1, h2, wg, wu, wd, tm, tf):
    n, d = x1.shape
    dff = wg.shape[1]
    return pl.pallas_call(
        _ffn_kernel,
        grid=(n // tm, dff // tf),
        in_specs=[
            pl.BlockSpec((tm, d), lambda i, j: (i, 0)),
            pl.BlockSpec((tm, d), lambda i, j: (i, 0)),
            pl.BlockSpec((d, tf), lambda i, j: (0, j)),
            pl.BlockSpec((d, tf), lambda i, j: (0, j)),
            pl.BlockSpec((tf, d), lambda i, j: (j, 0)),
        ],
        out_specs=pl.BlockSpec((tm, d), lambda i, j: (i, 0)),
        out_shape=jax.ShapeDtypeStruct((n, d), F32),
        scratch_shapes=[pltpu.VMEM((tm, d), F32)],
        compiler_params=_params(("parallel", "arbitrary")),
        name="ffn",
    )(x1, h2, wg, wu, wd)


SC_CORES = 2
SC_SUBCORES = 16
SC_WORKERS = SC_CORES * SC_SUBCORES
SC_CHUNK = 32
MOE_TM = 512


def _sc_gather_rows(table, idx):
    nrow, d = idx.shape[0], table.shape[1]
    assert nrow % (SC_WORKERS * SC_CHUNK) == 0
    per_worker = nrow // SC_WORKERS
    mesh = plsc.VectorSubcoreMesh(core_axis_name="c", subcore_axis_name="s",
                                  num_cores=SC_CORES, num_subcores=SC_SUBCORES)

    @functools.partial(
        pl.kernel, mesh=mesh, out_type=jax.ShapeDtypeStruct((nrow, d), table.dtype),
        scratch_types=[pltpu.VMEM((SC_CHUNK,), jnp.int32), pltpu.VMEM((SC_CHUNK, d), table.dtype),
                       pltpu.SemaphoreType.DMA])
    def gather(table_hbm, idx_hbm, out_hbm, idx_v, rows_v, sem):
        base = (lax.axis_index("s") * SC_CORES + lax.axis_index("c")) * per_worker

        @pl.loop(0, per_worker // SC_CHUNK)
        def _(j):
            off = base + j * SC_CHUNK
            pltpu.sync_copy(idx_hbm.at[pl.ds(off, SC_CHUNK)], idx_v)
            pltpu.async_copy(table_hbm.at[idx_v], rows_v, sem).wait()
            pltpu.sync_copy(rows_v, out_hbm.at[pl.ds(off, SC_CHUNK)])

    return gather(table, idx)


def _sc_scatter_rows(rows, pos12, n_out):
    n, d = rows.shape
    assert n % (SC_WORKERS * SC_CHUNK) == 0
    per_worker = n // SC_WORKERS
    idx = pos12.reshape(2, n)
    mesh = plsc.VectorSubcoreMesh(core_axis_name="c", subcore_axis_name="s",
                                  num_cores=SC_CORES, num_subcores=SC_SUBCORES)

    @functools.partial(
        pl.kernel, mesh=mesh, out_type=jax.ShapeDtypeStruct((n_out, d), rows.dtype),
        scratch_types=[pltpu.VMEM((2, SC_CHUNK), jnp.int32), pltpu.VMEM((SC_CHUNK, d), rows.dtype)])
    def scatter(rows_hbm, idx_hbm, out_hbm, idx_v, rows_v):
        base = (lax.axis_index("s") * SC_CORES + lax.axis_index("c")) * per_worker

        @pl.loop(0, per_worker // SC_CHUNK)
        def _(j):
            off = base + j * SC_CHUNK
            pltpu.sync_copy(rows_hbm.at[pl.ds(off, SC_CHUNK)], rows_v)
            for k in range(2):
                pltpu.sync_copy(idx_hbm.at[k, pl.ds(off, SC_CHUNK)], idx_v.at[k])
                pltpu.sync_copy(rows_v, out_hbm.at[idx_v.at[k]])

    return scatter(rows, idx)


def _moe_plan(comb, slot):
    n = slot.shape[0]
    sel = (slot > 0).astype(jnp.int32)
    count = jnp.sum(sel, axis=0)
    group = (count + MOE_TM - 1) // MOE_TM * MOE_TM
    ends = jnp.cumsum(group)
    pos = (ends - group)[None, :] + jnp.cumsum(sel, axis=0) - sel
    pos12 = jnp.concatenate([jnp.sum(jnp.where(slot == 1.0, pos, 0), axis=1),
                             jnp.sum(jnp.where(slot == 2.0, pos, 0), axis=1)]).astype(jnp.int32)
    n_rows = 2 * n + N_EXPERTS * MOE_TM
    tile_start = jnp.arange(n_rows // MOE_TM, dtype=jnp.int32) * MOE_TM
    tile_expert = jnp.minimum(jnp.sum(tile_start[:, None] >= ends[None, :], axis=1), N_EXPERTS - 1)
    n_used = (ends[-1] // MOE_TM).astype(jnp.int32).reshape(1)
    w12 = jnp.stack([jnp.sum(jnp.where(slot == 1.0, comb, 0.0), axis=1),
                     jnp.sum(jnp.where(slot == 2.0, comb, 0.0), axis=1)], axis=1)
    return pos12, n_rows, tile_expert.astype(jnp.int32), n_used, w12


def _moe_group_kernel(te_ref, nu_ref, xs_ref, wg_ref, wu_ref, wd_ref, ys_ref):
    used = pl.program_id(0) < nu_ref[0]

    @pl.when(used)
    def _():
        x = xs_ref[...].astype(BF16)
        g = _dot(x, wg_ref[0])
        u = _dot(x, wu_ref[0])
        ys_ref[...] = _dot((g * _sigmoid(g) * u).astype(BF16), wd_ref[0])

    @pl.when(jnp.logical_not(used))
    def _():
        ys_ref[...] = jnp.zeros_like(ys_ref)


def _moe_grouped(xs, tile_expert, n_used, wg, wu, wd):
    p, d = xs.shape
    _, _, dfe = wg.shape
    grid_spec = pltpu.PrefetchScalarGridSpec(
        num_scalar_prefetch=2,
        grid=(p // MOE_TM,),
        in_specs=[
            pl.BlockSpec((MOE_TM, d), lambda t, te, nu: (t, 0)),
            pl.BlockSpec((1, d, dfe), lambda t, te, nu: (te[t], 0, 0)),
            pl.BlockSpec((1, d, dfe), lambda t, te, nu: (te[t], 0, 0)),
            pl.BlockSpec((1, dfe, d), lambda t, te, nu: (te[t], 0, 0)),
        ],
        out_specs=pl.BlockSpec((MOE_TM, d), lambda t, te, nu: (t, 0)),
    )
    return pl.pallas_call(
        _moe_group_kernel,
        grid_spec=grid_spec,
        out_shape=jax.ShapeDtypeStruct((p, d), F32),
        compiler_params=_params(("arbitrary",)),
        name="moe_grouped",
    )(tile_expert, n_used, xs, wg, wu, wd)


def _moe_combine_kernel(x1_ref, y1_ref, y2_ref, w_ref, o_ref):
    w = w_ref[...]
    o_ref[...] = x1_ref[...] + w[:, 0:1] * y1_ref[...] + w[:, 1:2] * y2_ref[...]


def _moe_combine(x1, y, w12, tm):
    n, d = x1.shape
    nt = n // tm
    return pl.pallas_call(
        _moe_combine_kernel,
        grid=(nt,),
        in_specs=[
            pl.BlockSpec((tm, d), lambda i: (i, 0)),
            pl.BlockSpec((tm, d), lambda i: (i, 0)),
            pl.BlockSpec((tm, d), lambda i: (i + nt, 0)),
            pl.BlockSpec((tm, 2), lambda i: (i, 0)),
        ],
        out_specs=pl.BlockSpec((tm, d), lambda i: (i, 0)),
        out_shape=jax.ShapeDtypeStruct((n, d), F32),
        compiler_params=_params(("parallel",)),
        name="moe_combine",
    )(x1, y, y, w12)


def _moe(x1, h2, comb, slot, wg, wu, wd):
    pos12, n_rows, tile_expert, n_used, w12 = _moe_plan(comb, slot)
    xs = _sc_scatter_rows(h2, pos12, n_rows)
    ys = _moe_grouped(xs, tile_expert, n_used, wg, wu, wd)
    y = _sc_gather_rows(ys, pos12)
    return _moe_combine(x1, y, w12, _row_tile(x1.shape[0], 1024))


def _row_tile(n, want):
    t = min(n, want)
    assert n % t == 0
    return t


def _prep_layer(l, norm_mix, w_in, w_decay_f, b_decay_f, w_decay_b, b_decay_b, gla_out_norm, q_norm,
                k_norm, rpb, w_o_gla, w_o_na, w_out, norm_ffn):
    w = w_in[l]
    c0 = 2 * QK_W + 2 * V_W
    c1 = c0 + 2 * GLA_RANK
    c2 = c1 + 3 * NA_W
    w_main = jnp.concatenate([w[:, :QK_W] * (GLA_DK ** -0.5), w[:, QK_W:c0], w[:, c2:], w[:, c1:c2]],
                             axis=1).astype(BF16)
    wz = w[:, c0:c1].astype(BF16)
    zero = jnp.zeros((GLA_RANK, QK_W), F32)
    wdec = jnp.concatenate([jnp.concatenate([w_decay_f[l], zero], axis=1),
                            jnp.concatenate([zero, w_decay_b[l]], axis=1)], axis=0)
    wd_hi, wd_lo = _split_bf16(wdec)
    bdec = jnp.concatenate([b_decay_f[l], b_decay_b[l]])[None, :]
    qg = jnp.tile(q_norm[l] * (NA_DH ** -0.5), NA_HEADS)[None, :]
    kg = jnp.tile(k_norm[l], NA_HEADS)[None, :]
    return dict(
        g_mix=norm_mix[l][None, :], w_main=w_main, wz=wz, wd_hi=wd_hi, wd_lo=wd_lo, bdec=bdec,
        qg=qg, kg=kg, bias=_na_bias_table(rpb[l]),
        gog=gla_out_norm[l].reshape(1, V_W), wog=w_o_gla[l].astype(BF16),
        won=w_o_na[l].astype(BF16), wout=w_out[l].astype(BF16), g_ffn=norm_ffn[l][None, :])


def _trunk(x, layers, dense, moe):
    b, l, d = x.shape
    n = b * l
    xf = x.reshape(n, d)
    head_mean = jnp.asarray(np.kron(np.eye(NA_HEADS), np.full((NA_DH, NA_DH), 1.0 / NA_DH)), BF16)
    tb = _row_tile(l, 1024)
    for li, p in enumerate(layers):
        proj, la, lamin = _inproj(xf, p["g_mix"], p["w_main"], p["wz"], p["wd_hi"], p["wd_lo"], p["bdec"],
                           head_mean, p["qg"], p["kg"], _row_tile(n, 256))
        proj3 = proj.reshape(b, l, PROJ_W)
        la3 = la.reshape(b, l, 2 * QK_W)
        of = _gla(proj3, la3, lamin, False, tb).reshape(n, V_W)
        ob = _gla(proj3, la3, lamin, True, tb).reshape(n, V_W)
        na = _natten(proj3, p["bias"]).reshape(n, NA_W)
        if li % 2 == 0:
            wg, wu, wd = dense[li // 2]
            x1, h2 = _postmix(of, ob, proj, na, xf, p["gog"], p["wog"], p["won"], p["wout"],
                              p["g_ffn"], None, _row_tile(n, 512))
            xf = _ffn(x1, h2, wg, wu, wd, _row_tile(n, 1024), wg.shape[1] // 2)
        else:
            router, wg, wu, wd = moe[li // 2]
            x1, h2, comb, slot = _postmix(of, ob, proj, na, xf, p["gog"], p["wog"], p["won"], p["wout"],
                                          p["g_ffn"], router, _row_tile(n, 512))
            xf = _moe(x1, h2, comb, slot, wg, wu, wd)
    return xf.reshape(b, l, d)


def kernel(x_prompt, x_sample, norm_mix, w_in, w_decay_f, b_decay_f, w_decay_b, b_decay_b, gla_out_norm,
           q_norm, k_norm, rpb, w_o_gla, w_o_na, w_out, norm_ffn, ffn_w_gate, ffn_w_up, ffn_w_down,
           moe_router, moe_w_gate, moe_w_up, moe_w_down):
    depth = w_in.shape[0]
    layers = [_prep_layer(l, norm_mix, w_in, w_decay_f, b_decay_f, w_decay_b, b_decay_b, gla_out_norm,
                          q_norm, k_norm, rpb, w_o_gla, w_o_na, w_out, norm_ffn) for l in range(depth)]
    dense = [(ffn_w_gate[j].astype(BF16), ffn_w_up[j].astype(BF16), ffn_w_down[j].astype(BF16))
             for j in range(ffn_w_gate.shape[0])]
    moe = [(_split_bf16(moe_router[j]), moe_w_gate[j].astype(BF16), moe_w_up[j].astype(BF16),
            moe_w_down[j].astype(BF16)) for j in range(moe_router.shape[0])]
    return (_trunk(x_prompt, layers, dense, moe), _trunk(x_sample, layers, dense, moe))
```

```python
import functools

import numpy as np
import jax
import jax.numpy as jnp
from jax import lax
from jax.experimental import pallas as pl
from jax.experimental.pallas import tpu as pltpu
from jax.experimental.pallas import tpu_sc as plsc

F32 = jnp.float32
BF16 = jnp.bfloat16

EPS = 1e-6
GRID_W = 64
GLA_HEADS = 4
GLA_DK = 128
GLA_DV = 256
GLA_CHUNK = 64
GLA_RANK = 16
GLA_TAU = 16.0
NA_HEADS = 8
NA_DH = 64
NA_KR = 8
NA_KC = 16
N_EXPERTS = 8
NEG = -1e30

VMEM_LIMIT = 56 * 1024 * 1024

QK_W = GLA_HEADS * GLA_DK
V_W = GLA_HEADS * GLA_DV
NA_W = NA_HEADS * NA_DH
PROJ_W = 2 * QK_W + 2 * V_W + 2 * 1024 + 3 * NA_W
CW = 512
NQ_CHUNK = (2 * QK_W + 2 * V_W + 2048) // CW
NK_CHUNK = NQ_CHUNK + 1


def _params(sem):
    return pltpu.CompilerParams(dimension_semantics=sem, vmem_limit_bytes=VMEM_LIMIT)


def _resident(a):
    nd = a.ndim
    return pl.BlockSpec(a.shape, lambda *_: (0,) * nd, pipeline_mode=pl.Buffered(1))


def _split_bf16(a):
    hi = a.astype(BF16)
    lo = (a - hi.astype(F32)).astype(BF16)
    return hi, lo


def _dot(a, b):
    return jnp.dot(a, b, preferred_element_type=F32)


def _dot_nt(a, b):
    return lax.dot_general(a, b, (((1,), (1,)), ((), ())), preferred_element_type=F32)


def _dot_tn(a, b):
    return lax.dot_general(a, b, (((0,), (0,)), ((), ())), preferred_element_type=F32)


def _sigmoid(x):
    return 1.0 / (1.0 + jnp.exp(-x))


def _rmsnorm(x, g):
    ms = jnp.mean(x * x, axis=-1, keepdims=True)
    return x * lax.rsqrt(ms + EPS) * g


def _inproj_kernel(x_ref, g_ref, w_ref, wz_ref, wdh_ref, wdl_ref, bdec_ref, hm_ref, qg_ref, kg_ref,
                   proj_ref, la_ref, lamin_ref):
    h = _rmsnorm(x_ref[...], g_ref[...]).astype(BF16)
    z = _dot(h, wz_ref[...])
    zh, zl = _split_bf16(z)
    y = _dot(zh, wdh_ref[...]) + _dot(zl, wdh_ref[...]) + _dot(zh, wdl_ref[...]) + bdec_ref[...]
    ls = jnp.minimum(y, 0.0) - jnp.log(1.0 + jnp.exp(-jnp.abs(y)))
    la = ls * (1.0 / GLA_TAU)
    la_ref[...] = la
    lamin_ref[0] = jnp.broadcast_to(jnp.min(la, axis=0, keepdims=True), lamin_ref.shape[1:])
    for c in range(PROJ_W // CW):
        acc = _dot(h, w_ref[:, c * CW:(c + 1) * CW])
        if c in (NQ_CHUNK, NK_CHUNK):
            gain = qg_ref if c == NQ_CHUNK else kg_ref
            ms = _dot((acc * acc).astype(BF16), hm_ref[...])
            acc = acc * lax.rsqrt(ms + EPS) * gain[...]
        proj_ref[:, c * CW:(c + 1) * CW] = acc.astype(BF16)


def _inproj(x, g, w_main, wz, wd_hi, wd_lo, bdec, hm, qg, kg, tm):
    n, d = x.shape
    return pl.pallas_call(
        _inproj_kernel,
        grid=(n // tm,),
        in_specs=[
            pl.BlockSpec((tm, d), lambda i: (i, 0)),
            _resident(g), _resident(w_main), _resident(wz), _resident(wd_hi), _resident(wd_lo),
            _resident(bdec), _resident(hm), _resident(qg), _resident(kg),
        ],
        out_specs=[
            pl.BlockSpec((tm, PROJ_W), lambda i: (i, 0)),
            pl.BlockSpec((tm, 2 * QK_W), lambda i: (i, 0)),
            pl.BlockSpec((1, 8, 2 * QK_W), lambda i: (i, 0, 0)),
        ],
        out_shape=[
            jax.ShapeDtypeStruct((n, PROJ_W), BF16),
            jax.ShapeDtypeStruct((n, 2 * QK_W), F32),
            jax.ShapeDtypeStruct((n // tm, 8, 2 * QK_W), F32),
        ],
        compiler_params=_params(("parallel",)),
        name="inproj",
    )(x, g, w_main, wz, wd_hi, wd_lo, bdec, hm, qg, kg)


N_LEVELS = 6
INTER_BLK = N_LEVELS
STATE_BLK = N_LEVELS + 1


def _decay_sum_matrix(reverse):
    c = GLA_CHUNK
    m_all = np.zeros((N_LEVELS + 2, c, c), np.float32)
    for l in range(N_LEVELS):
        m = 1 << l
        for p in range(c):
            mid = (p // (2 * m)) * 2 * m + m
            if p >= mid:
                m_all[l, p, mid + 1:p + 1] = 1.0
            else:
                m_all[l, p, p + 1:mid + 1] = 1.0
    for p in range(c):
        m_all[INTER_BLK, p, :p + 1] = 1.0
        m_all[STATE_BLK, p, p + 1:] = 1.0
    if reverse:
        m_all = m_all[:, ::-1, ::-1]
    return np.ascontiguousarray(m_all).reshape((N_LEVELS + 2) * c, c)


def _level_matrix(reverse):
    i = np.arange(GLA_CHUNK)[:, None]
    j = np.arange(GLA_CHUNK)[None, :]
    x = i ^ j
    lvl = np.where(x > 0, np.floor(np.log2(np.maximum(x, 1))), N_LEVELS).astype(np.int32)
    valid = (i <= j) if reverse else (i >= j)
    return np.where(valid, lvl, N_LEVELS + 1).astype(np.int32)


GLA_FAST_CHUNK = 128
GLA_FAST_MAX_DECAY = 60.0


def _tri_matrix(reverse, c):
    i = np.arange(c)[:, None]
    j = np.arange(c)[None, :]
    return ((j >= i) if reverse else (j <= i)).astype(np.float32)


def _decay_columns(dec_row):
    t = jnp.broadcast_to(dec_row, (GLA_DK, GLA_DK)).T
    return jnp.concatenate([t, t], axis=1)


def _gla_kernel(q_ref, k_ref, v_ref, la_ref, lamin_ref, m_ref, lvl_ref, tri_ref, o_ref,
                s_ref, lr_ref, qh_ref, kh_ref, ks_ref, dec_ref, *, reverse, tb, sb):
    @pl.when(pl.program_id(1) == 0)
    def _():
        s_ref[...] = jnp.zeros_like(s_ref)

    fc = GLA_FAST_CHUNK
    n_fast = tb // fc
    total_row = 0 if reverse else fc - 1
    seqs = range(sb)

    def fast_rows(ci):
        return pl.ds(pl.multiple_of(ci * fc, fc), fc)

    def sc_lanes(h):
        return slice(h * 2 * GLA_DK, h * 2 * GLA_DK + GLA_DK)

    def qh_lanes(h):
        return slice(h * 2 * GLA_DK + GLA_DK, (h + 1) * 2 * GLA_DK)

    def fast_factors(ci, carry):
        rows = fast_rows(ci)
        for s in seqs:
            la_hi, la_lo = _split_bf16(la_ref[s, rows, :])
            b = _dot(tri_ref[...], la_hi) + _dot(tri_ref[...], la_lo)
            dec = jnp.exp(b[total_row:total_row + 1, :])
            f_q = jnp.exp(b)
            qh = (q_ref[s, rows, :].astype(F32) * f_q).astype(BF16)
            kh = k_ref[s, rows, :].astype(F32) / f_q
            qh_ref[s, rows, :] = qh
            for h in range(GLA_HEADS):
                lr_ref[s, rows, qh_lanes(h)] = qh[:, h * GLA_DK:(h + 1) * GLA_DK]
            kh_ref[s, rows, :] = kh.astype(BF16)
            ks_ref[s, rows, :] = (kh * dec).astype(BF16)
            dec_ref[s, pl.ds(pl.multiple_of(ci * 8, 8), 8), :] = jnp.broadcast_to(dec, (8, QK_W))
        return carry

    def fast_scores(ci, carry):
        rows = fast_rows(ci)
        i = lax.broadcasted_iota(jnp.int32, (fc, fc), 0)
        j = lax.broadcasted_iota(jnp.int32, (fc, fc), 1)
        causal = (i <= j) if reverse else (i >= j)
        for s in seqs:
            for h in range(GLA_HEADS):
                hs = slice(h * GLA_DK, (h + 1) * GLA_DK)
                sc = _dot_nt(qh_ref[s, rows, hs], kh_ref[s, rows, hs])
                lr_ref[s, rows, sc_lanes(h)] = jnp.where(causal, sc, 0.0).astype(BF16)
        return carry

    def fast_state(ci, carry):
        cc = (n_fast - 1 - ci) if reverse else ci
        rows = fast_rows(cc)
        heads = [(s, h) for s in seqs for h in range(GLA_HEADS)]
        vals = {(s, h): v_ref[s, rows, h * GLA_DV:(h + 1) * GLA_DV] for s, h in heads}
        kv = {(s, h): _dot_tn(ks_ref[s, rows, h * GLA_DK:(h + 1) * GLA_DK], vals[s, h]) for s, h in heads}
        for s, h in heads:
            dec = dec_ref[s, pl.ds(pl.multiple_of(cc * 8, 8), 8), :][:1, h * GLA_DK:(h + 1) * GLA_DK]
            st = s_ref[s, h]
            o = _dot(lr_ref[s, rows, h * 2 * GLA_DK:(h + 1) * 2 * GLA_DK],
                     jnp.concatenate([vals[s, h], st.astype(BF16)], axis=0))
            o_ref[s, rows, h * GLA_DV:(h + 1) * GLA_DV] = o.astype(o_ref.dtype)
            s_ref[s, h] = st * _decay_columns(dec) + kv[s, h]
        return carry

    def robust_chunk(ci, carry):
        c = GLA_CHUNK
        n_chunks = tb // c
        lvl = lvl_ref[...]
        row = lax.broadcasted_iota(jnp.int32, (c, GLA_DK), 0)
        total_row = 0 if reverse else c - 1
        cc = (n_chunks - 1 - ci) if reverse else ci
        rows = pl.ds(pl.multiple_of(cc * c, c), c)
        for s in seqs:
            la_hi, la_lo = _split_bf16(la_ref[s, rows, :])
            fac = jnp.exp(_dot(m_ref[...], la_hi) + _dot(m_ref[...], la_lo))
            for h in range(GLA_HEADS):
                hs = slice(h * GLA_DK, (h + 1) * GLA_DK)
                vs = slice(h * GLA_DV, (h + 1) * GLA_DV)
                q = q_ref[s, rows, hs].astype(F32)
                k = k_ref[s, rows, hs].astype(F32)
                v = v_ref[s, rows, vs]
                scores = jnp.zeros((c, c), F32)
                for l in range(N_LEVELS):
                    is_query = ((row >> l) & 1) == (0 if reverse else 1)
                    f_l = fac[l * c:(l + 1) * c, hs]
                    g_l = (jnp.where(is_query, q, k) * f_l).astype(BF16)
                    scores = jnp.where(lvl == l, _dot_nt(g_l, g_l), scores)
                scores = jnp.where(lvl == N_LEVELS, _dot_nt(q.astype(BF16), k.astype(BF16)), scores)
                f_in = fac[INTER_BLK * c:(INTER_BLK + 1) * c, hs]
                f_st = fac[STATE_BLK * c:(STATE_BLK + 1) * c, hs]
                st = s_ref[s, h]
                inter = _dot((q * f_in).astype(BF16), st.astype(BF16))
                intra = _dot(scores.astype(BF16), v)
                o_ref[s, rows, vs] = (inter + intra).astype(o_ref.dtype)
                dec = f_in[total_row:total_row + 1, :]
                s_ref[s, h] = st * _decay_columns(dec) + _dot_tn((k * f_st).astype(BF16), v)
        return carry

    bounded = jnp.min(lamin_ref[...]) >= -GLA_FAST_MAX_DECAY / GLA_FAST_CHUNK

    @pl.when(bounded)
    def _():
        lax.fori_loop(0, n_fast, fast_factors, 0, unroll=4)
        lax.fori_loop(0, n_fast, fast_scores, 0, unroll=4)
        lax.fori_loop(0, n_fast, fast_state, 0, unroll=4)

    @pl.when(jnp.logical_not(bounded))
    def _():
        lax.fori_loop(0, tb // GLA_CHUNK, robust_chunk, 0)


def _gla(proj3, la3, lamin, reverse, tb, sb):
    b, l, _ = proj3.shape
    nb = l // tb
    lamin4 = lamin.reshape(b, -1, 8, 2 * QK_W)
    tiles = lamin4.shape[1] // nb
    blk = (lambda i: nb - 1 - i) if reverse else (lambda i: i)
    m_all = jnp.asarray(_decay_sum_matrix(reverse), BF16)
    lvl = jnp.asarray(_level_matrix(reverse))
    tri = jnp.asarray(_tri_matrix(reverse, GLA_FAST_CHUNK), BF16)
    kern = functools.partial(_gla_kernel, reverse=reverse, tb=tb, sb=sb)
    decay_col = 1 if reverse else 0
    return pl.pallas_call(
        kern,
        grid=(b // sb, nb),
        in_specs=[
            pl.BlockSpec((sb, tb, QK_W), lambda s, i: (s, blk(i), 0)),
            pl.BlockSpec((sb, tb, QK_W), lambda s, i: (s, blk(i), 1)),
            pl.BlockSpec((sb, tb, V_W), lambda s, i: (s, blk(i), 1)),
            pl.BlockSpec((sb, tb, QK_W), lambda s, i: (s, blk(i), decay_col)),
            pl.BlockSpec((sb, tiles, 8, QK_W), lambda s, i: (s, blk(i), 0, decay_col)),
            _resident(m_all), _resident(lvl), _resident(tri),
        ],
        out_specs=pl.BlockSpec((sb, tb, V_W), lambda s, i: (s, blk(i), 0)),
        out_shape=jax.ShapeDtypeStruct((b, l, V_W), BF16),
        scratch_shapes=[
            pltpu.VMEM((sb, GLA_HEADS, GLA_DK, GLA_DV), F32),
            pltpu.VMEM((sb, tb, 2 * QK_W), BF16),
            pltpu.VMEM((sb, tb, QK_W), BF16),
            pltpu.VMEM((sb, tb, QK_W), BF16),
            pltpu.VMEM((sb, tb, QK_W), BF16),
            pltpu.VMEM((sb, tb // GLA_FAST_CHUNK * 8, QK_W), F32),
        ],
        compiler_params=_params(("parallel", "arbitrary")),
        name="gla_bwd" if reverse else "gla_fwd",
    )(proj3, proj3, proj3, la3, lamin4, m_all, lvl, tri)


NA_GROUP = 8
NA_BLK = NA_GROUP * GRID_W
NA_BAND = NA_KR * GRID_W


def _na_bias_table(rpb):
    qc = np.arange(GRID_W)[:, None]
    kc = np.arange(GRID_W)[None, :]
    col_start = np.clip(qc - NA_KC // 2, 0, GRID_W - NA_KC)
    valid = (kc >= col_start) & (kc < col_start + NA_KC)
    dc = np.clip(kc - qc, -(NA_KC - 1), NA_KC - 1) + NA_KC - 1
    onehot = ((dc[None] == np.arange(2 * NA_KC - 1)[:, None, None]) & valid[None]).astype(np.float32)
    mask = np.where(valid, 0.0, NEG).astype(np.float32)
    rows = jnp.stack([rpb[:, NA_KR - 1 - d:2 * NA_KR - 1 - d, :] for d in range(NA_KR)])
    t = jnp.einsum('dhkc,cqj->dhqkj', rows, jnp.asarray(onehot), precision=lax.Precision.HIGHEST)
    t = t + jnp.asarray(mask)[None, None, :, None, :]
    return t.reshape(NA_KR, NA_HEADS // 2, 2 * GRID_W, NA_BAND)


def _na_kernel(q_ref, kp_ref, kc_ref, kn_ref, vp_ref, vc_ref, vn_ref, bias_ref, o_ref,
               kwin_ref, vwin_ref, *, rows):
    g = pl.program_id(1)
    for j, (kr, vr) in enumerate(((kp_ref, vp_ref), (kc_ref, vc_ref), (kn_ref, vn_ref))):
        kwin_ref[j * NA_BLK:(j + 1) * NA_BLK, :] = kr[0]
        vwin_ref[j * NA_BLK:(j + 1) * NA_BLK, :] = vr[0]
    lane = lax.broadcasted_iota(jnp.int32, (GRID_W, 2 * NA_DH), 1)
    first = lane < NA_DH

    def one_row(rl, carry):
        r = g * NA_GROUP + rl
        rs = jnp.clip(r - NA_KR // 2, 0, rows - NA_KR)
        d = r - rs
        off = pl.multiple_of((rs - (g - 1) * NA_GROUP) * GRID_W, GRID_W)
        qrows = pl.ds(pl.multiple_of(rl * GRID_W, GRID_W), GRID_W)
        pairs = [slice(p * 2 * NA_DH, (p + 1) * 2 * NA_DH) for p in range(NA_HEADS // 2)]
        scores = []
        for p, ps in enumerate(pairs):
            qp = q_ref[0, qrows, ps]
            zero = jnp.zeros_like(qp)
            qs = jnp.concatenate([jnp.where(first, qp, zero), jnp.where(first, zero, qp)], axis=0)
            kb = kwin_ref[pl.ds(off, NA_BAND), ps]
            scores.append(_dot_nt(qs, kb) + bias_ref[d, p])
        probs = []
        for s in scores:
            e = jnp.exp(s - jnp.max(s, axis=-1, keepdims=True))
            probs.append((e.astype(BF16), jnp.sum(e, axis=-1, keepdims=True)))
        for (e, den), ps in zip(probs, pairs):
            vb = vwin_ref[pl.ds(off, NA_BAND), ps]
            pv = _dot(e, vb) / den
            o_ref[0, qrows, ps] = jnp.where(first, pv[:GRID_W], pv[GRID_W:]).astype(o_ref.dtype)
        return carry

    lax.fori_loop(0, NA_GROUP, one_row, 0, unroll=8)


def _natten(proj3, bias):
    b, l, _ = proj3.shape
    rows = l // GRID_W
    ng = rows // NA_GROUP
    qcol, kcol, vcol = NQ_CHUNK, NQ_CHUNK + 1, NQ_CHUNK + 2
    prev = lambda i: jnp.maximum(i - 1, 0)
    nxt = lambda i: jnp.minimum(i + 1, ng - 1)
    spec = lambda f, col: pl.BlockSpec((1, NA_BLK, NA_W), lambda s, i: (s, f(i), col))
    same = lambda i: i
    return pl.pallas_call(
        functools.partial(_na_kernel, rows=rows),
        grid=(b, ng),
        in_specs=[
            spec(same, qcol),
            spec(prev, kcol), spec(same, kcol), spec(nxt, kcol),
            spec(prev, vcol), spec(same, vcol), spec(nxt, vcol),
            _resident(bias),
        ],
        out_specs=pl.BlockSpec((1, NA_BLK, NA_W), lambda s, i: (s, i, 0)),
        out_shape=jax.ShapeDtypeStruct((b, l, NA_W), BF16),
        scratch_shapes=[pltpu.VMEM((3 * NA_BLK, NA_W), BF16), pltpu.VMEM((3 * NA_BLK, NA_W), BF16)],
        compiler_params=_params(("parallel", "parallel")),
        name="natten",
    )(proj3, proj3, proj3, proj3, proj3, proj3, proj3, bias)


def _postmix_kernel(*refs, with_router):
    if with_router:
        (of_ref, ob_ref, r_ref, ga_ref, gb_ref, na_ref, x_ref, gog_ref, wog_ref, won_ref, wout_ref,
         gf_ref, rh_ref, rl_ref, x1_ref, h2_ref, comb_ref, slot_ref) = refs
    else:
        (of_ref, ob_ref, r_ref, ga_ref, gb_ref, na_ref, x_ref, gog_ref, wog_ref, won_ref, wout_ref,
         gf_ref, x1_ref, h2_ref) = refs
    o = of_ref[...].astype(F32) + ob_ref[...].astype(F32)
    parts = []
    for h in range(GLA_HEADS):
        seg = o[:, h * GLA_DV:(h + 1) * GLA_DV]
        ms = jnp.mean(seg * seg, axis=-1, keepdims=True)
        parts.append(seg * lax.rsqrt(ms + EPS))
    r = r_ref[...].astype(F32)
    on = jnp.concatenate(parts, axis=-1) * gog_ref[...] * (r * _sigmoid(r))
    ya = _dot(on.astype(BF16), wog_ref[...])
    yb = _dot(na_ref[...], won_ref[...])
    merged = _sigmoid(ga_ref[...].astype(F32)) * ya + _sigmoid(gb_ref[...].astype(F32)) * yb
    x1 = x_ref[...] + _dot(merged.astype(BF16), wout_ref[...])
    x1_ref[...] = x1
    h2 = _rmsnorm(x1, gf_ref[...])
    h2_ref[...] = h2.astype(h2_ref.dtype)
    if with_router:
        hh, hl = _split_bf16(h2)
        lg = _dot(hh, rh_ref[...]) + _dot(hl, rh_ref[...]) + _dot(hh, rl_ref[...])
        lane = lax.broadcasted_iota(jnp.int32, lg.shape, 1).astype(F32)
        m1 = jnp.max(lg, axis=-1, keepdims=True)
        i1 = jnp.min(jnp.where(lg == m1, lane, float(N_EXPERTS)), axis=-1, keepdims=True)
        lg2 = jnp.where(lane == i1, -jnp.inf, lg)
        m2 = jnp.max(lg2, axis=-1, keepdims=True)
        i2 = jnp.min(jnp.where(lg2 == m2, lane, float(N_EXPERTS)), axis=-1, keepdims=True)
        t = jnp.exp(m2 - m1)
        w1 = 1.0 / (1.0 + t)
        comb_ref[...] = jnp.where(lane == i1, w1, 0.0) + jnp.where(lane == i2, t * w1, 0.0)
        slot_ref[...] = jnp.where(lane == i1, 1.0, 0.0) + jnp.where(lane == i2, 2.0, 0.0)


def _postmix(of, ob, proj, na, x, gog, wog, won, wout, gf, router, tm):
    n, d = x.shape
    with_router = router is not None
    in_specs = [
        pl.BlockSpec((tm, V_W), lambda i: (i, 0)),
        pl.BlockSpec((tm, V_W), lambda i: (i, 0)),
        pl.BlockSpec((tm, 1024), lambda i: (i, 2)),
        pl.BlockSpec((tm, 1024), lambda i: (i, 3)),
        pl.BlockSpec((tm, 1024), lambda i: (i, 4)),
        pl.BlockSpec((tm, NA_W), lambda i: (i, 0)),
        pl.BlockSpec((tm, d), lambda i: (i, 0)),
        _resident(gog), _resident(wog), _resident(won), _resident(wout), _resident(gf),
    ]
    args = [of, ob, proj, proj, proj, na, x, gog, wog, won, wout, gf]
    out_specs = [pl.BlockSpec((tm, d), lambda i: (i, 0)), pl.BlockSpec((tm, d), lambda i: (i, 0))]
    out_shape = [jax.ShapeDtypeStruct((n, d), F32), jax.ShapeDtypeStruct((n, d), F32 if with_router else BF16)]
    if with_router:
        in_specs += [_resident(router[0]), _resident(router[1])]
        args += list(router)
        for _ in range(2):
            out_specs.append(pl.BlockSpec((tm, N_EXPERTS), lambda i: (i, 0)))
            out_shape.append(jax.ShapeDtypeStruct((n, N_EXPERTS), F32))
    return pl.pallas_call(
        functools.partial(_postmix_kernel, with_router=with_router),
        grid=(n // tm,),
        in_specs=in_specs,
        out_specs=out_specs,
        out_shape=out_shape,
        compiler_params=_params(("parallel",)),
        name="postmix_router" if with_router else "postmix",
    )(*args)


def _ffn_kernel(x1_ref, h_ref, wg_ref, wu_ref, wd_ref, o_ref, acc_ref):
    j = pl.program_id(1)

    @pl.when(j == 0)
    def _():
        acc_ref[...] = jnp.zeros_like(acc_ref)

    h = h_ref[...]
    g = _dot(h, wg_ref[...])
    u = _dot(h, wu_ref[...])
    a = (g * _sigmoid(g) * u).astype(BF16)
    acc_ref[...] += _dot(a, wd_ref[...])

    @pl.when(j == pl.num_programs(1) - 1)
    def _():
        o_ref[...] = x1_ref[...] + acc_ref[...]


def _ffn(x1, h2, wg, wu, wd, tm, tf):
    n, d = x1.shape
    dff = wg.shape[1]
    return pl.pallas_call(
        _ffn_kernel,
        grid=(n // tm, dff // tf),
        in_specs=[
            pl.BlockSpec((tm, d), lambda i, j: (i, 0)),
            pl.BlockSpec((tm, d), lambda i, j: (i, 0)),
            pl.BlockSpec((d, tf), lambda i, j: (0, j)),
            pl.BlockSpec((d, tf), lambda i, j: (0, j)),
            pl.BlockSpec((tf, d), lambda i, j: (j, 0)),
        ],
        out_specs=pl.BlockSpec((tm, d), lambda i, j: (i, 0)),
        out_shape=jax.ShapeDtypeStruct((n, d), F32),
        scratch_shapes=[pltpu.VMEM((tm, d), F32)],
        compiler_params=_params(("parallel", "arbitrary")),
        name="ffn",
    )(x1, h2, wg, wu, wd)


SC_CORES = 2
SC_SUBCORES = 16
SC_WORKERS = SC_CORES * SC_SUBCORES
SC_CHUNK = 32
MOE_TM = 512


def _sc_gather_rows(table, idx):
    nrow, d = idx.shape[0], table.shape[1]
    assert nrow % (SC_WORKERS * SC_CHUNK) == 0
    per_worker = nrow // SC_WORKERS
    mesh = plsc.VectorSubcoreMesh(core_axis_name="c", subcore_axis_name="s",
                                  num_cores=SC_CORES, num_subcores=SC_SUBCORES)

    @functools.partial(
        pl.kernel, mesh=mesh, out_type=jax.ShapeDtypeStruct((nrow, d), table.dtype),
        scratch_types=[pltpu.VMEM((SC_CHUNK,), jnp.int32), pltpu.VMEM((SC_CHUNK, d), table.dtype),
                       pltpu.SemaphoreType.DMA])
    def gather(table_hbm, idx_hbm, out_hbm, idx_v, rows_v, sem):
        base = (lax.axis_index("s") * SC_CORES + lax.axis_index("c")) * per_worker

        @pl.loop(0, per_worker // SC_CHUNK)
        def _(j):
            off = base + j * SC_CHUNK
            pltpu.sync_copy(idx_hbm.at[pl.ds(off, SC_CHUNK)], idx_v)
            pltpu.async_copy(table_hbm.at[idx_v], rows_v, sem).wait()
            pltpu.sync_copy(rows_v, out_hbm.at[pl.ds(off, SC_CHUNK)])

    return gather(table, idx)


def _sc_scatter_rows(rows, pos12, n_out):
    n, d = rows.shape
    assert n % (SC_WORKERS * SC_CHUNK) == 0
    per_worker = n // SC_WORKERS
    idx = pos12.reshape(2, n)
    mesh = plsc.VectorSubcoreMesh(core_axis_name="c", subcore_axis_name="s",
                                  num_cores=SC_CORES, num_subcores=SC_SUBCORES)

    @functools.partial(
        pl.kernel, mesh=mesh, out_type=jax.ShapeDtypeStruct((n_out, d), rows.dtype),
        scratch_types=[pltpu.VMEM((2, SC_CHUNK), jnp.int32), pltpu.VMEM((SC_CHUNK, d), rows.dtype)])
    def scatter(rows_hbm, idx_hbm, out_hbm, idx_v, rows_v):
        base = (lax.axis_index("s") * SC_CORES + lax.axis_index("c")) * per_worker

        @pl.loop(0, per_worker // SC_CHUNK)
        def _(j):
            off = base + j * SC_CHUNK
            pltpu.sync_copy(rows_hbm.at[pl.ds(off, SC_CHUNK)], rows_v)
            for k in range(2):
                pltpu.sync_copy(idx_hbm.at[k, pl.ds(off, SC_CHUNK)], idx_v.at[k])
                pltpu.sync_copy(rows_v, out_hbm.at[idx_v.at[k]])

    return scatter(rows, idx)


def _moe_plan(comb, slot):
    n = slot.shape[0]
    sel = (slot > 0).astype(jnp.int32)
    count = jnp.sum(sel, axis=0)
    group = (count + MOE_TM - 1) // MOE_TM * MOE_TM
    ends = jnp.cumsum(group)
    pos = (ends - group)[None, :] + jnp.cumsum(sel, axis=0) - sel
    pos12 = jnp.concatenate([jnp.sum(jnp.where(slot == 1.0, pos, 0), axis=1),
                             jnp.sum(jnp.where(slot == 2.0, pos, 0), axis=1)]).astype(jnp.int32)
    n_rows = 2 * n + N_EXPERTS * MOE_TM
    tile_start = jnp.arange(n_rows // MOE_TM, dtype=jnp.int32) * MOE_TM
    tile_expert = jnp.minimum(jnp.sum(tile_start[:, None] >= ends[None, :], axis=1), N_EXPERTS - 1)
    n_used = (ends[-1] // MOE_TM).astype(jnp.int32).reshape(1)
    w12 = jnp.stack([jnp.sum(jnp.where(slot == 1.0, comb, 0.0), axis=1),
                     jnp.sum(jnp.where(slot == 2.0, comb, 0.0), axis=1)], axis=1)
    return pos12, n_rows, tile_expert.astype(jnp.int32), n_used, w12


def _moe_group_kernel(te_ref, nu_ref, xs_ref, wg_ref, wu_ref, wd_ref, ys_ref):
    used = pl.program_id(0) < nu_ref[0]

    @pl.when(used)
    def _():
        x = xs_ref[...].astype(BF16)
        g = _dot(x, wg_ref[0])
        u = _dot(x, wu_ref[0])
        ys_ref[...] = _dot((g * _sigmoid(g) * u).astype(BF16), wd_ref[0])

    @pl.when(jnp.logical_not(used))
    def _():
        ys_ref[...] = jnp.zeros_like(ys_ref)


def _moe_grouped(xs, tile_expert, n_used, wg, wu, wd):
    p, d = xs.shape
    _, _, dfe = wg.shape
    grid_spec = pltpu.PrefetchScalarGridSpec(
        num_scalar_prefetch=2,
        grid=(p // MOE_TM,),
        in_specs=[
            pl.BlockSpec((MOE_TM, d), lambda t, te, nu: (t, 0)),
            pl.BlockSpec((1, d, dfe), lambda t, te, nu: (te[t], 0, 0)),
            pl.BlockSpec((1, d, dfe), lambda t, te, nu: (te[t], 0, 0)),
            pl.BlockSpec((1, dfe, d), lambda t, te, nu: (te[t], 0, 0)),
        ],
        out_specs=pl.BlockSpec((MOE_TM, d), lambda t, te, nu: (t, 0)),
    )
    return pl.pallas_call(
        _moe_group_kernel,
        grid_spec=grid_spec,
        out_shape=jax.ShapeDtypeStruct((p, d), F32),
        compiler_params=_params(("arbitrary",)),
        name="moe_grouped",
    )(tile_expert, n_used, xs, wg, wu, wd)


def _moe_combine_kernel(x1_ref, y1_ref, y2_ref, w_ref, o_ref):
    w = w_ref[...]
    o_ref[...] = x1_ref[...] + w[:, 0:1] * y1_ref[...] + w[:, 1:2] * y2_ref[...]


def _moe_combine(x1, y, w12, tm):
    n, d = x1.shape
    nt = n // tm
    return pl.pallas_call(
        _moe_combine_kernel,
        grid=(nt,),
        in_specs=[
            pl.BlockSpec((tm, d), lambda i: (i, 0)),
            pl.BlockSpec((tm, d), lambda i: (i, 0)),
            pl.BlockSpec((tm, d), lambda i: (i + nt, 0)),
            pl.BlockSpec((tm, 2), lambda i: (i, 0)),
        ],
        out_specs=pl.BlockSpec((tm, d), lambda i: (i, 0)),
        out_shape=jax.ShapeDtypeStruct((n, d), F32),
        compiler_params=_params(("parallel",)),
        name="moe_combine",
    )(x1, y, y, w12)


def _moe(x1, h2, comb, slot, wg, wu, wd):
    pos12, n_rows, tile_expert, n_used, w12 = _moe_plan(comb, slot)
    xs = _sc_scatter_rows(h2, pos12, n_rows)
    ys = _moe_grouped(xs, tile_expert, n_used, wg, wu, wd)
    y = _sc_gather_rows(ys, pos12)
    return _moe_combine(x1, y, w12, _row_tile(x1.shape[0], 1024))


def _row_tile(n, want):
    t = min(n, want)
    assert n % t == 0
    return t


def _prep_layer(l, norm_mix, w_in, w_decay_f, b_decay_f, w_decay_b, b_decay_b, gla_out_norm, q_norm,
                k_norm, rpb, w_o_gla, w_o_na, w_out, norm_ffn):
    w = w_in[l]
    c0 = 2 * QK_W + 2 * V_W
    c1 = c0 + 2 * GLA_RANK
    c2 = c1 + 3 * NA_W
    w_main = jnp.concatenate([w[:, :QK_W] * (GLA_DK ** -0.5), w[:, QK_W:c0], w[:, c2:], w[:, c1:c2]],
                             axis=1).astype(BF16)
    wz = w[:, c0:c1].astype(BF16)
    zero = jnp.zeros((GLA_RANK, QK_W), F32)
    wdec = jnp.concatenate([jnp.concatenate([w_decay_f[l], zero], axis=1),
                            jnp.concatenate([zero, w_decay_b[l]], axis=1)], axis=0)
    wd_hi, wd_lo = _split_bf16(wdec)
    bdec = jnp.concatenate([b_decay_f[l], b_decay_b[l]])[None, :]
    qg = jnp.tile(q_norm[l] * (NA_DH ** -0.5), NA_HEADS)[None, :]
    kg = jnp.tile(k_norm[l], NA_HEADS)[None, :]
    return dict(
        g_mix=norm_mix[l][None, :], w_main=w_main, wz=wz, wd_hi=wd_hi, wd_lo=wd_lo, bdec=bdec,
        qg=qg, kg=kg, bias=_na_bias_table(rpb[l]),
        gog=gla_out_norm[l].reshape(1, V_W), wog=w_o_gla[l].astype(BF16),
        won=w_o_na[l].astype(BF16), wout=w_out[l].astype(BF16), g_ffn=norm_ffn[l][None, :])


def _trunk(x, layers, dense, moe):
    b, l, d = x.shape
    n = b * l
    xf = x.reshape(n, d)
    head_mean = jnp.asarray(np.kron(np.eye(NA_HEADS), np.full((NA_DH, NA_DH), 1.0 / NA_DH)), BF16)
    tb = _row_tile(l, 512)
    sb = 2 if b % 2 == 0 else 1
    for li, p in enumerate(layers):
        proj, la, lamin = _inproj(xf, p["g_mix"], p["w_main"], p["wz"], p["wd_hi"], p["wd_lo"], p["bdec"],
                           head_mean, p["qg"], p["kg"], _row_tile(n, 256))
        proj3 = proj.reshape(b, l, PROJ_W)
        la3 = la.reshape(b, l, 2 * QK_W)
        of = _gla(proj3, la3, lamin, False, tb, sb).reshape(n, V_W)
        ob = _gla(proj3, la3, lamin, True, tb, sb).reshape(n, V_W)
        na = _natten(proj3, p["bias"]).reshape(n, NA_W)
        if li % 2 == 0:
            wg, wu, wd = dense[li // 2]
            x1, h2 = _postmix(of, ob, proj, na, xf, p["gog"], p["wog"], p["won"], p["wout"],
                              p["g_ffn"], None, _row_tile(n, 512))
            xf = _ffn(x1, h2, wg, wu, wd, _row_tile(n, 1024), wg.shape[1] // 2)
        else:
            router, wg, wu, wd = moe[li // 2]
            x1, h2, comb, slot = _postmix(of, ob, proj, na, xf, p["gog"], p["wog"], p["won"], p["wout"],
                                          p["g_ffn"], router, _row_tile(n, 512))
            xf = _moe(x1, h2, comb, slot, wg, wu, wd)
    return xf.reshape(b, l, d)


def kernel(x_prompt, x_sample, norm_mix, w_in, w_decay_f, b_decay_f, w_decay_b, b_decay_b, gla_out_norm,
           q_norm, k_norm, rpb, w_o_gla, w_o_na, w_out, norm_ffn, ffn_w_gate, ffn_w_up, ffn_w_down,
           moe_router, moe_w_gate, moe_w_up, moe_w_down):
    depth = w_in.shape[0]
    layers = [_prep_layer(l, norm_mix, w_in, w_decay_f, b_decay_f, w_decay_b, b_decay_b, gla_out_norm,
                          q_norm, k_norm, rpb, w_o_gla, w_o_na, w_out, norm_ffn) for l in range(depth)]
    dense = [(ffn_w_gate[j].astype(BF16), ffn_w_up[j].astype(BF16), ffn_w_down[j].astype(BF16))
             for j in range(ffn_w_gate.shape[0])]
    moe = [(_split_bf16(moe_router[j]), moe_w_gate[j].astype(BF16), moe_w_up[j].astype(BF16),
            moe_w_down[j].astype(BF16)) for j in range(moe_router.shape[0])]
    return (_trunk(x_prompt, layers, dense, moe), _trunk(x_sample, layers, dense, moe))
```

```python
import functools

import numpy as np
import jax
import jax.numpy as jnp
from jax import lax
from jax.experimental import pallas as pl
from jax.experimental.pallas import tpu as pltpu
from jax.experimental.pallas import tpu_sc as plsc

F32 = jnp.float32
BF16 = jnp.bfloat16

EPS = 1e-6
GRID_W = 64
GLA_HEADS = 4
GLA_DK = 128
GLA_DV = 256
GLA_CHUNK = 64
GLA_RANK = 16
GLA_TAU = 16.0
NA_HEADS = 8
NA_DH = 64
NA_KR = 8
NA_KC = 16
N_EXPERTS = 8
NEG = -1e30

VMEM_LIMIT = 56 * 1024 * 1024

QK_W = GLA_HEADS * GLA_DK
V_W = GLA_HEADS * GLA_DV
NA_W = NA_HEADS * NA_DH
PROJ_W = 2 * QK_W + 2 * V_W + 2 * 1024 + 3 * NA_W
CW = 512
NQ_CHUNK = (2 * QK_W + 2 * V_W + 2048) // CW
NK_CHUNK = NQ_CHUNK + 1
INPROJ_SUB = 256


def _params(sem):
    return pltpu.CompilerParams(dimension_semantics=sem, vmem_limit_bytes=VMEM_LIMIT)


def _resident(a):
    nd = a.ndim
    return pl.BlockSpec(a.shape, lambda *_: (0,) * nd, pipeline_mode=pl.Buffered(1))


def _split_bf16(a):
    hi = a.astype(BF16)
    lo = (a - hi.astype(F32)).astype(BF16)
    return hi, lo


def _dot(a, b):
    return jnp.dot(a, b, preferred_element_type=F32)


def _dot_nt(a, b):
    return lax.dot_general(a, b, (((1,), (1,)), ((), ())), preferred_element_type=F32)


def _dot_tn(a, b):
    return lax.dot_general(a, b, (((0,), (0,)), ((), ())), preferred_element_type=F32)


def _sigmoid(x):
    return 1.0 / (1.0 + jnp.exp(-x))


def _rmsnorm(x, g):
    ms = jnp.mean(x * x, axis=-1, keepdims=True)
    return x * lax.rsqrt(ms + EPS) * g


def _inproj_kernel(x_ref, g_ref, w_ref, wz_ref, wdh_ref, wdl_ref, bdec_ref, hm_ref, qg_ref, kg_ref,
                   proj_ref, la_ref, lamin_ref):
    for t in range(x_ref.shape[0] // INPROJ_SUB):
        rs = slice(t * INPROJ_SUB, (t + 1) * INPROJ_SUB)
        h = _rmsnorm(x_ref[rs, :], g_ref[...]).astype(BF16)
        z = _dot(h, wz_ref[...])
        zh, zl = _split_bf16(z)
        y = _dot(zh, wdh_ref[...]) + _dot(zl, wdh_ref[...]) + _dot(zh, wdl_ref[...]) + bdec_ref[...]
        ls = jnp.minimum(y, 0.0) - jnp.log(1.0 + jnp.exp(-jnp.abs(y)))
        la = ls * (1.0 / GLA_TAU)
        la_ref[rs, :] = la
        lamin_ref[t] = jnp.broadcast_to(jnp.min(la, axis=0, keepdims=True), lamin_ref.shape[1:])
        for c in range(PROJ_W // CW):
            acc = _dot(h, w_ref[:, c * CW:(c + 1) * CW])
            if c in (NQ_CHUNK, NK_CHUNK):
                gain = qg_ref if c == NQ_CHUNK else kg_ref
                ms = _dot((acc * acc).astype(BF16), hm_ref[...])
                acc = acc * lax.rsqrt(ms + EPS) * gain[...]
            proj_ref[rs, c * CW:(c + 1) * CW] = acc.astype(BF16)


def _inproj(x, g, w_main, wz, wd_hi, wd_lo, bdec, hm, qg, kg, tm):
    n, d = x.shape
    return pl.pallas_call(
        _inproj_kernel,
        grid=(n // tm,),
        in_specs=[
            pl.BlockSpec((tm, d), lambda i: (i, 0)),
            _resident(g), _resident(w_main), _resident(wz), _resident(wd_hi), _resident(wd_lo),
            _resident(bdec), _resident(hm), _resident(qg), _resident(kg),
        ],
        out_specs=[
            pl.BlockSpec((tm, PROJ_W), lambda i: (i, 0)),
            pl.BlockSpec((tm, 2 * QK_W), lambda i: (i, 0)),
            pl.BlockSpec((tm // INPROJ_SUB, 8, 2 * QK_W), lambda i: (i, 0, 0)),
        ],
        out_shape=[
            jax.ShapeDtypeStruct((n, PROJ_W), BF16),
            jax.ShapeDtypeStruct((n, 2 * QK_W), F32),
            jax.ShapeDtypeStruct((n // INPROJ_SUB, 8, 2 * QK_W), F32),
        ],
        compiler_params=_params(("parallel",)),
        name="inproj",
    )(x, g, w_main, wz, wd_hi, wd_lo, bdec, hm, qg, kg)


N_LEVELS = 6
INTER_BLK = N_LEVELS
STATE_BLK = N_LEVELS + 1


def _decay_sum_matrix(reverse):
    c = GLA_CHUNK
    m_all = np.zeros((N_LEVELS + 2, c, c), np.float32)
    for l in range(N_LEVELS):
        m = 1 << l
        for p in range(c):
            mid = (p // (2 * m)) * 2 * m + m
            if p >= mid:
                m_all[l, p, mid + 1:p + 1] = 1.0
            else:
                m_all[l, p, p + 1:mid + 1] = 1.0
    for p in range(c):
        m_all[INTER_BLK, p, :p + 1] = 1.0
        m_all[STATE_BLK, p, p + 1:] = 1.0
    if reverse:
        m_all = m_all[:, ::-1, ::-1]
    return np.ascontiguousarray(m_all).reshape((N_LEVELS + 2) * c, c)


def _level_matrix(reverse):
    i = np.arange(GLA_CHUNK)[:, None]
    j = np.arange(GLA_CHUNK)[None, :]
    x = i ^ j
    lvl = np.where(x > 0, np.floor(np.log2(np.maximum(x, 1))), N_LEVELS).astype(np.int32)
    valid = (i <= j) if reverse else (i >= j)
    return np.where(valid, lvl, N_LEVELS + 1).astype(np.int32)


GLA_FAST_CHUNK = 128
GLA_FAST_MAX_DECAY = 60.0


def _tri_matrix(reverse, c):
    i = np.arange(c)[:, None]
    j = np.arange(c)[None, :]
    return ((j >= i) if reverse else (j <= i)).astype(np.float32)


def _decay_columns(dec_row):
    t = jnp.broadcast_to(dec_row, (GLA_DK, GLA_DK)).T
    return jnp.concatenate([t, t], axis=1)


def _gla_kernel(q_ref, k_ref, v_ref, la_ref, lamin_ref, m_ref, lvl_ref, tri_ref, o_ref,
                s_ref, lr_ref, qh_ref, kh_ref, ks_ref, dec_ref, *, reverse, tb, sb):
    @pl.when(pl.program_id(1) == 0)
    def _():
        s_ref[...] = jnp.zeros_like(s_ref)

    fc = GLA_FAST_CHUNK
    n_fast = tb // fc
    total_row = 0 if reverse else fc - 1
    seqs = range(sb)

    def fast_rows(ci):
        return pl.ds(pl.multiple_of(ci * fc, fc), fc)

    def sc_lanes(h):
        return slice(h * 2 * GLA_DK, h * 2 * GLA_DK + GLA_DK)

    def qh_lanes(h):
        return slice(h * 2 * GLA_DK + GLA_DK, (h + 1) * 2 * GLA_DK)

    def fast_factors(ci, carry):
        rows = fast_rows(ci)
        for s in seqs:
            la_hi, la_lo = _split_bf16(la_ref[s, rows, :])
            b = _dot(tri_ref[...], la_hi) + _dot(tri_ref[...], la_lo)
            dec = jnp.exp(b[total_row:total_row + 1, :])
            f_q = jnp.exp(b)
            qh = (q_ref[s, rows, :].astype(F32) * f_q).astype(BF16)
            kh = k_ref[s, rows, :].astype(F32) / f_q
            qh_ref[s, rows, :] = qh
            for h in range(GLA_HEADS):
                lr_ref[s, rows, qh_lanes(h)] = qh[:, h * GLA_DK:(h + 1) * GLA_DK]
            kh_ref[s, rows, :] = kh.astype(BF16)
            ks_ref[s, rows, :] = (kh * dec).astype(BF16)
            dec_ref[s, pl.ds(pl.multiple_of(ci * 8, 8), 8), :] = jnp.broadcast_to(dec, (8, QK_W))
        return carry

    def fast_scores(ci, carry):
        rows = fast_rows(ci)
        i = lax.broadcasted_iota(jnp.int32, (fc, fc), 0)
        j = lax.broadcasted_iota(jnp.int32, (fc, fc), 1)
        causal = (i <= j) if reverse else (i >= j)
        for s in seqs:
            for h in range(GLA_HEADS):
                hs = slice(h * GLA_DK, (h + 1) * GLA_DK)
                sc = _dot_nt(qh_ref[s, rows, hs], kh_ref[s, rows, hs])
                lr_ref[s, rows, sc_lanes(h)] = jnp.where(causal, sc, 0.0).astype(BF16)
        return carry

    def fast_state(ci, carry):
        cc = (n_fast - 1 - ci) if reverse else ci
        rows = fast_rows(cc)
        heads = [(s, h) for s in seqs for h in range(GLA_HEADS)]
        vals = {(s, h): v_ref[s, rows, h * GLA_DV:(h + 1) * GLA_DV] for s, h in heads}
        kv = {(s, h): _dot_tn(ks_ref[s, rows, h * GLA_DK:(h + 1) * GLA_DK], vals[s, h]) for s, h in heads}
        for s, h in heads:
            dec = dec_ref[s, pl.ds(pl.multiple_of(cc * 8, 8), 8), :][:1, h * GLA_DK:(h + 1) * GLA_DK]
            st = s_ref[s, h]
            o = _dot(lr_ref[s, rows, h * 2 * GLA_DK:(h + 1) * 2 * GLA_DK],
                     jnp.concatenate([vals[s, h], st.astype(BF16)], axis=0))
            o_ref[s, rows, h * GLA_DV:(h + 1) * GLA_DV] = o.astype(o_ref.dtype)
            s_ref[s, h] = st * _decay_columns(dec) + kv[s, h]
        return carry

    def robust_chunk(ci, carry):
        c = GLA_CHUNK
        n_chunks = tb // c
        lvl = lvl_ref[...]
        row = lax.broadcasted_iota(jnp.int32, (c, GLA_DK), 0)
        total_row = 0 if reverse else c - 1
        cc = (n_chunks - 1 - ci) if reverse else ci
        rows = pl.ds(pl.multiple_of(cc * c, c), c)
        for s in seqs:
            la_hi, la_lo = _split_bf16(la_ref[s, rows, :])
            fac = jnp.exp(_dot(m_ref[...], la_hi) + _dot(m_ref[...], la_lo))
            for h in range(GLA_HEADS):
                hs = slice(h * GLA_DK, (h + 1) * GLA_DK)
                vs = slice(h * GLA_DV, (h + 1) * GLA_DV)
                q = q_ref[s, rows, hs].astype(F32)
                k = k_ref[s, rows, hs].astype(F32)
                v = v_ref[s, rows, vs]
                scores = jnp.zeros((c, c), F32)
                for l in range(N_LEVELS):
                    is_query = ((row >> l) & 1) == (0 if reverse else 1)
                    f_l = fac[l * c:(l + 1) * c, hs]
                    g_l = (jnp.where(is_query, q, k) * f_l).astype(BF16)
                    scores = jnp.where(lvl == l, _dot_nt(g_l, g_l), scores)
                scores = jnp.where(lvl == N_LEVELS, _dot_nt(q.astype(BF16), k.astype(BF16)), scores)
                f_in = fac[INTER_BLK * c:(INTER_BLK + 1) * c, hs]
                f_st = fac[STATE_BLK * c:(STATE_BLK + 1) * c, hs]
                st = s_ref[s, h]
                inter = _dot((q * f_in).astype(BF16), st.astype(BF16))
                intra = _dot(scores.astype(BF16), v)
                o_ref[s, rows, vs] = (inter + intra).astype(o_ref.dtype)
                dec = f_in[total_row:total_row + 1, :]
                s_ref[s, h] = st * _decay_columns(dec) + _dot_tn((k * f_st).astype(BF16), v)
        return carry

    bounded = jnp.min(lamin_ref[...]) >= -GLA_FAST_MAX_DECAY / GLA_FAST_CHUNK

    @pl.when(bounded)
    def _():
        lax.fori_loop(0, n_fast, fast_factors, 0, unroll=4)
        lax.fori_loop(0, n_fast, fast_scores, 0, unroll=4)
        lax.fori_loop(0, n_fast, fast_state, 0, unroll=4)

    @pl.when(jnp.logical_not(bounded))
    def _():
        lax.fori_loop(0, tb // GLA_CHUNK, robust_chunk, 0)


def _gla(proj3, la3, lamin, reverse, tb, sb):
    b, l, _ = proj3.shape
    nb = l // tb
    lamin4 = lamin.reshape(b, -1, 8, 2 * QK_W)
    tiles = lamin4.shape[1] // nb
    blk = (lambda i: nb - 1 - i) if reverse else (lambda i: i)
    m_all = jnp.asarray(_decay_sum_matrix(reverse), BF16)
    lvl = jnp.asarray(_level_matrix(reverse))
    tri = jnp.asarray(_tri_matrix(reverse, GLA_FAST_CHUNK), BF16)
    kern = functools.partial(_gla_kernel, reverse=reverse, tb=tb, sb=sb)
    decay_col = 1 if reverse else 0
    return pl.pallas_call(
        kern,
        grid=(b // sb, nb),
        in_specs=[
            pl.BlockSpec((sb, tb, QK_W), lambda s, i: (s, blk(i), 0)),
            pl.BlockSpec((sb, tb, QK_W), lambda s, i: (s, blk(i), 1)),
            pl.BlockSpec((sb, tb, V_W), lambda s, i: (s, blk(i), 1)),
            pl.BlockSpec((sb, tb, QK_W), lambda s, i: (s, blk(i), decay_col)),
            pl.BlockSpec((sb, tiles, 8, QK_W), lambda s, i: (s, blk(i), 0, decay_col)),
            _resident(m_all), _resident(lvl), _resident(tri),
        ],
        out_specs=pl.BlockSpec((sb, tb, V_W), lambda s, i: (s, blk(i), 0)),
        out_shape=jax.ShapeDtypeStruct((b, l, V_W), BF16),
        scratch_shapes=[
            pltpu.VMEM((sb, GLA_HEADS, GLA_DK, GLA_DV), F32),
            pltpu.VMEM((sb, tb, 2 * QK_W), BF16),
            pltpu.VMEM((sb, tb, QK_W), BF16),
            pltpu.VMEM((sb, tb, QK_W), BF16),
            pltpu.VMEM((sb, tb, QK_W), BF16),
            pltpu.VMEM((sb, tb // GLA_FAST_CHUNK * 8, QK_W), F32),
        ],
        compiler_params=_params(("parallel", "arbitrary")),
        name="gla_bwd" if reverse else "gla_fwd",
    )(proj3, proj3, proj3, la3, lamin4, m_all, lvl, tri)


NA_GROUP = 8
NA_BLK = NA_GROUP * GRID_W
NA_BAND = NA_KR * GRID_W


def _na_bias_table(rpb):
    qc = np.arange(GRID_W)[:, None]
    kc = np.arange(GRID_W)[None, :]
    col_start = np.clip(qc - NA_KC // 2, 0, GRID_W - NA_KC)
    valid = (kc >= col_start) & (kc < col_start + NA_KC)
    dc = np.clip(kc - qc, -(NA_KC - 1), NA_KC - 1) + NA_KC - 1
    onehot = ((dc[None] == np.arange(2 * NA_KC - 1)[:, None, None]) & valid[None]).astype(np.float32)
    mask = np.where(valid, 0.0, NEG).astype(np.float32)
    rows = jnp.stack([rpb[:, NA_KR - 1 - d:2 * NA_KR - 1 - d, :] for d in range(NA_KR)])
    t = jnp.einsum('dhkc,cqj->dhqkj', rows, jnp.asarray(onehot), precision=lax.Precision.HIGHEST)
    t = t + jnp.asarray(mask)[None, None, :, None, :]
    return t.reshape(NA_KR, NA_HEADS // 2, 2 * GRID_W, NA_BAND)


def _na_kernel(q_ref, kp_ref, kc_ref, kn_ref, vp_ref, vc_ref, vn_ref, bias_ref, o_ref,
               kwin_ref, vwin_ref, *, rows):
    g = pl.program_id(1)
    for j, (kr, vr) in enumerate(((kp_ref, vp_ref), (kc_ref, vc_ref), (kn_ref, vn_ref))):
        kwin_ref[j * NA_BLK:(j + 1) * NA_BLK, :] = kr[0]
        vwin_ref[j * NA_BLK:(j + 1) * NA_BLK, :] = vr[0]
    lane = lax.broadcasted_iota(jnp.int32, (GRID_W, 2 * NA_DH), 1)
    first = lane < NA_DH

    def one_row(rl, carry):
        r = g * NA_GROUP + rl
        rs = jnp.clip(r - NA_KR // 2, 0, rows - NA_KR)
        d = r - rs
        off = pl.multiple_of((rs - (g - 1) * NA_GROUP) * GRID_W, GRID_W)
        qrows = pl.ds(pl.multiple_of(rl * GRID_W, GRID_W), GRID_W)
        pairs = [slice(p * 2 * NA_DH, (p + 1) * 2 * NA_DH) for p in range(NA_HEADS // 2)]
        scores = []
        for p, ps in enumerate(pairs):
            qp = q_ref[0, qrows, ps]
            zero = jnp.zeros_like(qp)
            qs = jnp.concatenate([jnp.where(first, qp, zero), jnp.where(first, zero, qp)], axis=0)
            kb = kwin_ref[pl.ds(off, NA_BAND), ps]
            scores.append(_dot_nt(qs, kb) + bias_ref[d, p])
        probs = []
        for s in scores:
            e = jnp.exp(s - jnp.max(s, axis=-1, keepdims=True))
            probs.append((e.astype(BF16), jnp.sum(e, axis=-1, keepdims=True)))
        for (e, den), ps in zip(probs, pairs):
            vb = vwin_ref[pl.ds(off, NA_BAND), ps]
            pv = _dot(e, vb) / den
            o_ref[0, qrows, ps] = jnp.where(first, pv[:GRID_W], pv[GRID_W:]).astype(o_ref.dtype)
        return carry

    lax.fori_loop(0, NA_GROUP, one_row, 0, unroll=8)


def _natten(proj3, bias):
    b, l, _ = proj3.shape
    rows = l // GRID_W
    ng = rows // NA_GROUP
    qcol, kcol, vcol = NQ_CHUNK, NQ_CHUNK + 1, NQ_CHUNK + 2
    prev = lambda i: jnp.maximum(i - 1, 0)
    nxt = lambda i: jnp.minimum(i + 1, ng - 1)
    spec = lambda f, col: pl.BlockSpec((1, NA_BLK, NA_W), lambda s, i: (s, f(i), col))
    same = lambda i: i
    return pl.pallas_call(
        functools.partial(_na_kernel, rows=rows),
        grid=(b, ng),
        in_specs=[
            spec(same, qcol),
            spec(prev, kcol), spec(same, kcol), spec(nxt, kcol),
            spec(prev, vcol), spec(same, vcol), spec(nxt, vcol),
            _resident(bias),
        ],
        out_specs=pl.BlockSpec((1, NA_BLK, NA_W), lambda s, i: (s, i, 0)),
        out_shape=jax.ShapeDtypeStruct((b, l, NA_W), BF16),
        scratch_shapes=[pltpu.VMEM((3 * NA_BLK, NA_W), BF16), pltpu.VMEM((3 * NA_BLK, NA_W), BF16)],
        compiler_params=_params(("parallel", "parallel")),
        name="natten",
    )(proj3, proj3, proj3, proj3, proj3, proj3, proj3, bias)


def _postmix_kernel(*refs, with_router):
    if with_router:
        (of_ref, ob_ref, r_ref, ga_ref, gb_ref, na_ref, x_ref, gog_ref, wog_ref, won_ref, wout_ref,
         gf_ref, rh_ref, rl_ref, x1_ref, h2_ref, comb_ref, slot_ref) = refs
    else:
        (of_ref, ob_ref, r_ref, ga_ref, gb_ref, na_ref, x_ref, gog_ref, wog_ref, won_ref, wout_ref,
         gf_ref, x1_ref, h2_ref) = refs
    o = of_ref[...].astype(F32) + ob_ref[...].astype(F32)
    parts = []
    for h in range(GLA_HEADS):
        seg = o[:, h * GLA_DV:(h + 1) * GLA_DV]
        ms = jnp.mean(seg * seg, axis=-1, keepdims=True)
        parts.append(seg * lax.rsqrt(ms + EPS))
    r = r_ref[...].astype(F32)
    on = jnp.concatenate(parts, axis=-1) * gog_ref[...] * (r * _sigmoid(r))
    ya = _dot(on.astype(BF16), wog_ref[...])
    yb = _dot(na_ref[...], won_ref[...])
    merged = _sigmoid(ga_ref[...].astype(F32)) * ya + _sigmoid(gb_ref[...].astype(F32)) * yb
    x1 = x_ref[...] + _dot(merged.astype(BF16), wout_ref[...])
    x1_ref[...] = x1
    h2 = _rmsnorm(x1, gf_ref[...])
    h2_ref[...] = h2.astype(h2_ref.dtype)
    if with_router:
        hh, hl = _split_bf16(h2)
        lg2 = _dot_nt(rh_ref[...], hh) + _dot_nt(rl_ref[...], hl)
        lg = lg2[:N_EXPERTS] + lg2[N_EXPERTS:]
        row = lax.broadcasted_iota(jnp.int32, lg.shape, 0).astype(F32)
        m1 = jnp.max(lg, axis=0, keepdims=True)
        i1 = jnp.min(jnp.where(lg == m1, row, float(N_EXPERTS)), axis=0, keepdims=True)
        lg_rest = jnp.where(row == i1, -jnp.inf, lg)
        m2 = jnp.max(lg_rest, axis=0, keepdims=True)
        i2 = jnp.min(jnp.where(lg_rest == m2, row, float(N_EXPERTS)), axis=0, keepdims=True)
        t = jnp.exp(m2 - m1)
        w1 = 1.0 / (1.0 + t)
        comb_ref[...] = jnp.where(row == i1, w1, 0.0) + jnp.where(row == i2, t * w1, 0.0)
        slot_ref[...] = jnp.where(row == i1, 1.0, 0.0) + jnp.where(row == i2, 2.0, 0.0)


def _postmix(of, ob, proj, na, x, gog, wog, won, wout, gf, router, tm):
    n, d = x.shape
    with_router = router is not None
    in_specs = [
        pl.BlockSpec((tm, V_W), lambda i: (i, 0)),
        pl.BlockSpec((tm, V_W), lambda i: (i, 0)),
        pl.BlockSpec((tm, 1024), lambda i: (i, 2)),
        pl.BlockSpec((tm, 1024), lambda i: (i, 3)),
        pl.BlockSpec((tm, 1024), lambda i: (i, 4)),
        pl.BlockSpec((tm, NA_W), lambda i: (i, 0)),
        pl.BlockSpec((tm, d), lambda i: (i, 0)),
        _resident(gog), _resident(wog), _resident(won), _resident(wout), _resident(gf),
    ]
    args = [of, ob, proj, proj, proj, na, x, gog, wog, won, wout, gf]
    out_specs = [pl.BlockSpec((tm, d), lambda i: (i, 0)), pl.BlockSpec((tm, d), lambda i: (i, 0))]
    out_shape = [jax.ShapeDtypeStruct((n, d), F32), jax.ShapeDtypeStruct((n, d), F32 if with_router else BF16)]
    if with_router:
        in_specs += [_resident(router[0]), _resident(router[1])]
        args += list(router)
        for _ in range(2):
            out_specs.append(pl.BlockSpec((N_EXPERTS, tm), lambda i: (0, i)))
            out_shape.append(jax.ShapeDtypeStruct((N_EXPERTS, n), F32))
    return pl.pallas_call(
        functools.partial(_postmix_kernel, with_router=with_router),
        grid=(n // tm,),
        in_specs=in_specs,
        out_specs=out_specs,
        out_shape=out_shape,
        compiler_params=_params(("parallel",)),
        name="postmix_router" if with_router else "postmix",
    )(*args)


def _ffn_kernel(x1_ref, h_ref, wg_ref, wu_ref, wd_ref, o_ref, acc_ref):
    j = pl.program_id(1)

    @pl.when(j == 0)
    def _():
        acc_ref[...] = jnp.zeros_like(acc_ref)

    h = h_ref[...]
    g = _dot(h, wg_ref[...])
    u = _dot(h, wu_ref[...])
    a = (g * _sigmoid(g) * u).astype(BF16)
    acc_ref[...] += _dot(a, wd_ref[...])

    @pl.when(j == pl.num_programs(1) - 1)
    def _():
        o_ref[...] = x1_ref[...] + acc_ref[...]


def _ffn(x1, h2, wg, wu, wd, tm, tf):
    n, d = x1.shape
    dff = wg.shape[1]
    return pl.pallas_call(
        _ffn_kernel,
        grid=(n // tm, dff // tf),
        in_specs=[
            pl.BlockSpec((tm, d), lambda i, j: (i, 0)),
            pl.BlockSpec((tm, d), lambda i, j: (i, 0)),
            pl.BlockSpec((d, tf), lambda i, j: (0, j)),
            pl.BlockSpec((d, tf), lambda i, j: (0, j)),
            pl.BlockSpec((tf, d), lambda i, j: (j, 0)),
        ],
        out_specs=pl.BlockSpec((tm, d), lambda i, j: (i, 0)),
        out_shape=jax.ShapeDtypeStruct((n, d), F32),
        scratch_shapes=[pltpu.VMEM((tm, d), F32)],
        compiler_params=_params(("parallel", "arbitrary")),
        name="ffn",
    )(x1, h2, wg, wu, wd)


SC_CORES = 2
SC_SUBCORES = 16
SC_WORKERS = SC_CORES * SC_SUBCORES
SC_CHUNK = 32
MOE_TM = 512


def _sc_gather_rows(table, idx):
    nrow, d = idx.shape[0], table.shape[1]
    assert nrow % (SC_WORKERS * SC_CHUNK) == 0
    per_worker = nrow // SC_WORKERS
    mesh = plsc.VectorSubcoreMesh(core_axis_name="c", subcore_axis_name="s",
                                  num_cores=SC_CORES, num_subcores=SC_SUBCORES)

    @functools.partial(
        pl.kernel, mesh=mesh, out_type=jax.ShapeDtypeStruct((nrow, d), table.dtype),
        scratch_types=[pltpu.VMEM((SC_CHUNK,), jnp.int32), pltpu.VMEM((SC_CHUNK, d), table.dtype),
                       pltpu.SemaphoreType.DMA])
    def gather(table_hbm, idx_hbm, out_hbm, idx_v, rows_v, sem):
        base = (lax.axis_index("s") * SC_CORES + lax.axis_index("c")) * per_worker

        @pl.loop(0, per_worker // SC_CHUNK)
        def _(j):
            off = base + j * SC_CHUNK
            pltpu.sync_copy(idx_hbm.at[pl.ds(off, SC_CHUNK)], idx_v)
            pltpu.async_copy(table_hbm.at[idx_v], rows_v, sem).wait()
            pltpu.sync_copy(rows_v, out_hbm.at[pl.ds(off, SC_CHUNK)])

    return gather(table, idx)


def _sc_scatter_rows(rows, pos12, n_out):
    n, d = rows.shape
    assert n % (SC_WORKERS * SC_CHUNK) == 0
    per_worker = n // SC_WORKERS
    idx = pos12.reshape(2, n)
    mesh = plsc.VectorSubcoreMesh(core_axis_name="c", subcore_axis_name="s",
                                  num_cores=SC_CORES, num_subcores=SC_SUBCORES)

    @functools.partial(
        pl.kernel, mesh=mesh, out_type=jax.ShapeDtypeStruct((n_out, d), rows.dtype),
        scratch_types=[pltpu.VMEM((2, SC_CHUNK), jnp.int32), pltpu.VMEM((SC_CHUNK, d), rows.dtype)])
    def scatter(rows_hbm, idx_hbm, out_hbm, idx_v, rows_v):
        base = (lax.axis_index("s") * SC_CORES + lax.axis_index("c")) * per_worker

        @pl.loop(0, per_worker // SC_CHUNK)
        def _(j):
            off = base + j * SC_CHUNK
            pltpu.sync_copy(rows_hbm.at[pl.ds(off, SC_CHUNK)], rows_v)
            for k in range(2):
                pltpu.sync_copy(idx_hbm.at[k, pl.ds(off, SC_CHUNK)], idx_v.at[k])
                pltpu.sync_copy(rows_v, out_hbm.at[idx_v.at[k]])

    return scatter(rows, idx)


def _moe_plan(comb, slot):
    n = slot.shape[1]
    sel = (slot > 0).astype(jnp.int32)
    count = jnp.sum(sel, axis=1)
    group = (count + MOE_TM - 1) // MOE_TM * MOE_TM
    ends = jnp.cumsum(group)
    pos = (ends - group)[:, None] + jnp.cumsum(sel, axis=1) - sel
    pos12 = jnp.concatenate([jnp.sum(jnp.where(slot == 1.0, pos, 0), axis=0),
                             jnp.sum(jnp.where(slot == 2.0, pos, 0), axis=0)]).astype(jnp.int32)
    n_rows = 2 * n + N_EXPERTS * MOE_TM
    tile_start = jnp.arange(n_rows // MOE_TM, dtype=jnp.int32) * MOE_TM
    tile_expert = jnp.minimum(jnp.sum(tile_start[:, None] >= ends[None, :], axis=1), N_EXPERTS - 1)
    filled = (ends - group + count)[tile_expert] - tile_start
    tile_rows = jnp.where(tile_start < ends[-1], jnp.clip(filled, 0, MOE_TM), 0).astype(jnp.int32)
    w12 = jnp.stack([jnp.sum(jnp.where(slot == 1.0, comb, 0.0), axis=0),
                     jnp.sum(jnp.where(slot == 2.0, comb, 0.0), axis=0)], axis=1)
    return pos12, n_rows, tile_expert.astype(jnp.int32), tile_rows, w12


def _moe_group_kernel(te_ref, tr_ref, xs_ref, wg_ref, wu_ref, wd_ref, ys_ref):
    n_filled = tr_ref[pl.program_id(0)]
    used = n_filled > 0

    @pl.when(used)
    def _():
        row = lax.broadcasted_iota(jnp.int32, xs_ref.shape, 0)
        x = jnp.where(row < n_filled, xs_ref[...], 0.0).astype(BF16)
        g = _dot(x, wg_ref[0])
        u = _dot(x, wu_ref[0])
        ys_ref[...] = _dot((g * _sigmoid(g) * u).astype(BF16), wd_ref[0])

    @pl.when(jnp.logical_not(used))
    def _():
        ys_ref[...] = jnp.zeros_like(ys_ref)


def _moe_grouped(xs, tile_expert, tile_rows, wg, wu, wd):
    p, d = xs.shape
    _, _, dfe = wg.shape
    grid_spec = pltpu.PrefetchScalarGridSpec(
        num_scalar_prefetch=2,
        grid=(p // MOE_TM,),
        in_specs=[
            pl.BlockSpec((MOE_TM, d), lambda t, te, tr: (t, 0)),
            pl.BlockSpec((1, d, dfe), lambda t, te, tr: (te[t], 0, 0)),
            pl.BlockSpec((1, d, dfe), lambda t, te, tr: (te[t], 0, 0)),
            pl.BlockSpec((1, dfe, d), lambda t, te, tr: (te[t], 0, 0)),
        ],
        out_specs=pl.BlockSpec((MOE_TM, d), lambda t, te, tr: (t, 0)),
    )
    return pl.pallas_call(
        _moe_group_kernel,
        grid_spec=grid_spec,
        out_shape=jax.ShapeDtypeStruct((p, d), F32),
        compiler_params=_params(("arbitrary",)),
        name="moe_grouped",
    )(tile_expert, tile_rows, xs, wg, wu, wd)


def _moe_combine_kernel(x1_ref, y1_ref, y2_ref, w_ref, o_ref):
    w = w_ref[...]
    o_ref[...] = x1_ref[...] + w[:, 0:1] * y1_ref[...] + w[:, 1:2] * y2_ref[...]


def _moe_combine(x1, y, w12, tm):
    n, d = x1.shape
    nt = n // tm
    return pl.pallas_call(
        _moe_combine_kernel,
        grid=(nt,),
        in_specs=[
            pl.BlockSpec((tm, d), lambda i: (i, 0)),
            pl.BlockSpec((tm, d), lambda i: (i, 0)),
            pl.BlockSpec((tm, d), lambda i: (i + nt, 0)),
            pl.BlockSpec((tm, 2), lambda i: (i, 0)),
        ],
        out_specs=pl.BlockSpec((tm, d), lambda i: (i, 0)),
        out_shape=jax.ShapeDtypeStruct((n, d), F32),
        compiler_params=_params(("parallel",)),
        name="moe_combine",
    )(x1, y, y, w12)


def _moe(x1, h2, comb, slot, wg, wu, wd):
    pos12, n_rows, tile_expert, tile_rows, w12 = _moe_plan(comb, slot)
    xs = _sc_scatter_rows(h2, pos12, n_rows)
    ys = _moe_grouped(xs, tile_expert, tile_rows, wg, wu, wd)
    y = _sc_gather_rows(ys, pos12)
    return _moe_combine(x1, y, w12, _row_tile(x1.shape[0], 1024))


def _row_tile(n, want):
    t = min(n, want)
    assert n % t == 0
    return t


def _prep_layer(l, norm_mix, w_in, w_decay_f, b_decay_f, w_decay_b, b_decay_b, gla_out_norm, q_norm,
                k_norm, rpb, w_o_gla, w_o_na, w_out, norm_ffn):
    w = w_in[l]
    c0 = 2 * QK_W + 2 * V_W
    c1 = c0 + 2 * GLA_RANK
    c2 = c1 + 3 * NA_W
    w_main = jnp.concatenate([w[:, :QK_W] * (GLA_DK ** -0.5), w[:, QK_W:c0], w[:, c2:], w[:, c1:c2]],
                             axis=1).astype(BF16)
    wz = w[:, c0:c1].astype(BF16)
    zero = jnp.zeros((GLA_RANK, QK_W), F32)
    wdec = jnp.concatenate([jnp.concatenate([w_decay_f[l], zero], axis=1),
                            jnp.concatenate([zero, w_decay_b[l]], axis=1)], axis=0)
    wd_hi, wd_lo = _split_bf16(wdec)
    bdec = jnp.concatenate([b_decay_f[l], b_decay_b[l]])[None, :]
    qg = jnp.tile(q_norm[l] * (NA_DH ** -0.5), NA_HEADS)[None, :]
    kg = jnp.tile(k_norm[l], NA_HEADS)[None, :]
    return dict(
        g_mix=norm_mix[l][None, :], w_main=w_main, wz=wz, wd_hi=wd_hi, wd_lo=wd_lo, bdec=bdec,
        qg=qg, kg=kg, bias=_na_bias_table(rpb[l]),
        gog=gla_out_norm[l].reshape(1, V_W), wog=w_o_gla[l].astype(BF16),
        won=w_o_na[l].astype(BF16), wout=w_out[l].astype(BF16), g_ffn=norm_ffn[l][None, :])


def _router_operands(router):
    hi, lo = _split_bf16(router.T)
    return jnp.concatenate([hi, lo], axis=0), jnp.concatenate([hi, jnp.zeros_like(hi)], axis=0)


def _trunk(x, layers, dense, moe):
    b, l, d = x.shape
    n = b * l
    xf = x.reshape(n, d)
    head_mean = jnp.asarray(np.kron(np.eye(NA_HEADS), np.full((NA_DH, NA_DH), 1.0 / NA_DH)), BF16)
    tb = _row_tile(l, 512)
    sb = 2 if b % 2 == 0 else 1
    for li, p in enumerate(layers):
        proj, la, lamin = _inproj(xf, p["g_mix"], p["w_main"], p["wz"], p["wd_hi"], p["wd_lo"], p["bdec"],
                           head_mean, p["qg"], p["kg"], _row_tile(n, 512))
        proj3 = proj.reshape(b, l, PROJ_W)
        la3 = la.reshape(b, l, 2 * QK_W)
        of = _gla(proj3, la3, lamin, False, tb, sb).reshape(n, V_W)
        ob = _gla(proj3, la3, lamin, True, tb, sb).reshape(n, V_W)
        na = _natten(proj3, p["bias"]).reshape(n, NA_W)
        if li % 2 == 0:
            wg, wu, wd = dense[li // 2]
            x1, h2 = _postmix(of, ob, proj, na, xf, p["gog"], p["wog"], p["won"], p["wout"],
                              p["g_ffn"], None, _row_tile(n, 512))
            xf = _ffn(x1, h2, wg, wu, wd, _row_tile(n, 1024), wg.shape[1] // 2)
        else:
            router, wg, wu, wd = moe[li // 2]
            x1, h2, comb, slot = _postmix(of, ob, proj, na, xf, p["gog"], p["wog"], p["won"], p["wout"],
                                          p["g_ffn"], router, _row_tile(n, 512))
            xf = _moe(x1, h2, comb, slot, wg, wu, wd)
    return xf.reshape(b, l, d)


def kernel(x_prompt, x_sample, norm_mix, w_in, w_decay_f, b_decay_f, w_decay_b, b_decay_b, gla_out_norm,
           q_norm, k_norm, rpb, w_o_gla, w_o_na, w_out, norm_ffn, ffn_w_gate, ffn_w_up, ffn_w_down,
           moe_router, moe_w_gate, moe_w_up, moe_w_down):
    depth = w_in.shape[0]
    layers = [_prep_layer(l, norm_mix, w_in, w_decay_f, b_decay_f, w_decay_b, b_decay_b, gla_out_norm,
                          q_norm, k_norm, rpb, w_o_gla, w_o_na, w_out, norm_ffn) for l in range(depth)]
    dense = [(ffn_w_gate[j].astype(BF16), ffn_w_up[j].astype(BF16), ffn_w_down[j].astype(BF16))
             for j in range(ffn_w_gate.shape[0])]
    moe = [(_router_operands(moe_router[j]), moe_w_gate[j].astype(BF16), moe_w_up[j].astype(BF16),
            moe_w_down[j].astype(BF16)) for j in range(moe_router.shape[0])]
    return (_trunk(x_prompt, layers, dense, moe), _trunk(x_sample, layers, dense, moe))
```

```python
import functools

import numpy as np
import jax
import jax.numpy as jnp
from jax import lax
from jax.experimental import pallas as pl
from jax.experimental.pallas import tpu as pltpu
from jax.experimental.pallas import tpu_sc as plsc

F32 = jnp.float32
BF16 = jnp.bfloat16

EPS = 1e-6
GRID_W = 64
GLA_HEADS = 4
GLA_DK = 128
GLA_DV = 256
GLA_CHUNK = 64
GLA_RANK = 16
GLA_TAU = 16.0
NA_HEADS = 8
NA_DH = 64
NA_KR = 8
NA_KC = 16
N_EXPERTS = 8
NEG = -1e30
LOG2_E = 1.4426950408889634

VMEM_LIMIT = 56 * 1024 * 1024

QK_W = GLA_HEADS * GLA_DK
V_W = GLA_HEADS * GLA_DV
NA_W = NA_HEADS * NA_DH
PROJ_W = 2 * QK_W + 2 * V_W + 2 * 1024 + 3 * NA_W
CW = 512
NQ_CHUNK = (2 * QK_W + 2 * V_W + 2048) // CW
NK_CHUNK = NQ_CHUNK + 1
INPROJ_SUB = 256


def _params(sem):
    return pltpu.CompilerParams(dimension_semantics=sem, vmem_limit_bytes=VMEM_LIMIT)


def _resident(a):
    nd = a.ndim
    return pl.BlockSpec(a.shape, lambda *_: (0,) * nd, pipeline_mode=pl.Buffered(1))


def _split_bf16(a):
    hi = a.astype(BF16)
    lo = (a - hi.astype(F32)).astype(BF16)
    return hi, lo


def _dot(a, b):
    return jnp.dot(a, b, preferred_element_type=F32)


def _dot_nt(a, b):
    return lax.dot_general(a, b, (((1,), (1,)), ((), ())), preferred_element_type=F32)


def _dot_tn(a, b):
    return lax.dot_general(a, b, (((0,), (0,)), ((), ())), preferred_element_type=F32)


def _sigmoid(x):
    return 1.0 / (1.0 + jnp.exp2(x * (-LOG2_E)))


def _rmsnorm(x, g):
    ms = jnp.mean(x * x, axis=-1, keepdims=True)
    return x * lax.rsqrt(ms + EPS) * g


def _inproj_kernel(x_ref, g_ref, w_ref, wz_ref, wd_ref, bdec_ref, hm_ref, qg_ref, kg_ref,
                   proj_ref, la_ref, lamin_ref):
    for t in range(x_ref.shape[0] // INPROJ_SUB):
        rs = slice(t * INPROJ_SUB, (t + 1) * INPROJ_SUB)
        h = _rmsnorm(x_ref[rs, :], g_ref[...]).astype(BF16)
        z = _dot(h, wz_ref[...])
        y = _dot(z.astype(BF16), wd_ref[...]) + bdec_ref[...]
        ls = jnp.minimum(y, 0.0) - jnp.log(1.0 + jnp.exp(-jnp.abs(y)))
        la = ls * (1.0 / GLA_TAU)
        la_ref[rs, :] = la
        lamin_ref[t] = jnp.broadcast_to(jnp.min(la, axis=0, keepdims=True), lamin_ref.shape[1:])
        for c in range(PROJ_W // CW):
            acc = _dot(h, w_ref[:, c * CW:(c + 1) * CW])
            if c in (NQ_CHUNK, NK_CHUNK):
                gain = qg_ref if c == NQ_CHUNK else kg_ref
                ms = _dot((acc * acc).astype(BF16), hm_ref[...])
                acc = acc * lax.rsqrt(ms + EPS) * gain[...]
            proj_ref[rs, c * CW:(c + 1) * CW] = acc.astype(BF16)


def _inproj(x, g, w_main, wz, wd, bdec, hm, qg, kg, tm):
    n, d = x.shape
    return pl.pallas_call(
        _inproj_kernel,
        grid=(n // tm,),
        in_specs=[
            pl.BlockSpec((tm, d), lambda i: (i, 0)),
            _resident(g), _resident(w_main), _resident(wz), _resident(wd),
            _resident(bdec), _resident(hm), _resident(qg), _resident(kg),
        ],
        out_specs=[
            pl.BlockSpec((tm, PROJ_W), lambda i: (i, 0)),
            pl.BlockSpec((tm, 2 * QK_W), lambda i: (i, 0)),
            pl.BlockSpec((tm // INPROJ_SUB, 8, 2 * QK_W), lambda i: (i, 0, 0)),
        ],
        out_shape=[
            jax.ShapeDtypeStruct((n, PROJ_W), BF16),
            jax.ShapeDtypeStruct((n, 2 * QK_W), F32),
            jax.ShapeDtypeStruct((n // INPROJ_SUB, 8, 2 * QK_W), F32),
        ],
        compiler_params=_params(("parallel",)),
        name="inproj",
    )(x, g, w_main, wz, wd, bdec, hm, qg, kg)


N_LEVELS = 6
INTER_BLK = N_LEVELS
STATE_BLK = N_LEVELS + 1


def _decay_sum_matrix(reverse):
    c = GLA_CHUNK
    m_all = np.zeros((N_LEVELS + 2, c, c), np.float32)
    for l in range(N_LEVELS):
        m = 1 << l
        for p in range(c):
            mid = (p // (2 * m)) * 2 * m + m
            if p >= mid:
                m_all[l, p, mid + 1:p + 1] = 1.0
            else:
                m_all[l, p, p + 1:mid + 1] = 1.0
    for p in range(c):
        m_all[INTER_BLK, p, :p + 1] = 1.0
        m_all[STATE_BLK, p, p + 1:] = 1.0
    if reverse:
        m_all = m_all[:, ::-1, ::-1]
    return np.ascontiguousarray(m_all).reshape((N_LEVELS + 2) * c, c)


def _level_matrix(reverse):
    i = np.arange(GLA_CHUNK)[:, None]
    j = np.arange(GLA_CHUNK)[None, :]
    x = i ^ j
    lvl = np.where(x > 0, np.floor(np.log2(np.maximum(x, 1))), N_LEVELS).astype(np.int32)
    valid = (i <= j) if reverse else (i >= j)
    return np.where(valid, lvl, N_LEVELS + 1).astype(np.int32)


GLA_FAST_CHUNK = 128
GLA_FAST_MAX_DECAY = 60.0


def _tri_matrix(reverse, c):
    i = np.arange(c)[:, None]
    j = np.arange(c)[None, :]
    return ((j >= i) if reverse else (j <= i)).astype(np.float32)


def _decay_columns(dec_row):
    t = jnp.broadcast_to(dec_row, (GLA_DK, GLA_DK)).T
    return jnp.concatenate([t, t], axis=1)


def _gla_kernel(q_ref, k_ref, v_ref, la_ref, lamin_ref, m_ref, lvl_ref, tri_ref, o_ref,
                s_ref, lr_ref, qh_ref, kh_ref, ks_ref, dec_ref, *, reverse, tb, sb):
    @pl.when(pl.program_id(1) == 0)
    def _():
        s_ref[...] = jnp.zeros_like(s_ref)

    fc = GLA_FAST_CHUNK
    n_fast = tb // fc
    total_row = 0 if reverse else fc - 1
    seqs = range(sb)

    def fast_rows(ci):
        return pl.ds(pl.multiple_of(ci * fc, fc), fc)

    def sc_lanes(h):
        return slice(h * 2 * GLA_DK, h * 2 * GLA_DK + GLA_DK)

    def qh_lanes(h):
        return slice(h * 2 * GLA_DK + GLA_DK, (h + 1) * 2 * GLA_DK)

    def fast_factors(ci, carry):
        rows = fast_rows(ci)
        for s in seqs:
            b = _dot(tri_ref[...], la_ref[s, rows, :].astype(BF16))
            dec = jnp.exp(b[total_row:total_row + 1, :])
            f_q = jnp.exp(b)
            qh = (q_ref[s, rows, :].astype(F32) * f_q).astype(BF16)
            kh = k_ref[s, rows, :].astype(F32) / f_q
            qh_ref[s, rows, :] = qh
            for h in range(GLA_HEADS):
                lr_ref[s, rows, qh_lanes(h)] = qh[:, h * GLA_DK:(h + 1) * GLA_DK]
            kh_ref[s, rows, :] = kh.astype(BF16)
            ks_ref[s, rows, :] = (kh * dec).astype(BF16)
            dec_ref[s, pl.ds(pl.multiple_of(ci * 8, 8), 8), :] = jnp.broadcast_to(dec, (8, QK_W))
        return carry

    def fast_scores(ci, carry):
        rows = fast_rows(ci)
        i = lax.broadcasted_iota(jnp.int32, (fc, fc), 0)
        j = lax.broadcasted_iota(jnp.int32, (fc, fc), 1)
        causal = (i <= j) if reverse else (i >= j)
        for s in seqs:
            for h in range(GLA_HEADS):
                hs = slice(h * GLA_DK, (h + 1) * GLA_DK)
                sc = _dot_nt(qh_ref[s, rows, hs], kh_ref[s, rows, hs])
                lr_ref[s, rows, sc_lanes(h)] = jnp.where(causal, sc, 0.0).astype(BF16)
        return carry

    def fast_state(ci, carry):
        cc = (n_fast - 1 - ci) if reverse else ci
        rows = fast_rows(cc)
        heads = [(s, h) for s in seqs for h in range(GLA_HEADS)]
        vals = {(s, h): v_ref[s, rows, h * GLA_DV:(h + 1) * GLA_DV] for s, h in heads}
        kv = {(s, h): _dot_tn(ks_ref[s, rows, h * GLA_DK:(h + 1) * GLA_DK], vals[s, h]) for s, h in heads}
        for s, h in heads:
            dec = dec_ref[s, pl.ds(pl.multiple_of(cc * 8, 8), 8), :][:1, h * GLA_DK:(h + 1) * GLA_DK]
            st = s_ref[s, h]
            o = _dot(lr_ref[s, rows, h * 2 * GLA_DK:(h + 1) * 2 * GLA_DK],
                     jnp.concatenate([vals[s, h], st.astype(BF16)], axis=0))
            o_ref[s, rows, h * GLA_DV:(h + 1) * GLA_DV] = o.astype(o_ref.dtype)
            s_ref[s, h] = st * _decay_columns(dec) + kv[s, h]
        return carry

    def robust_chunk(ci, carry):
        c = GLA_CHUNK
        n_chunks = tb // c
        lvl = lvl_ref[...]
        row = lax.broadcasted_iota(jnp.int32, (c, GLA_DK), 0)
        total_row = 0 if reverse else c - 1
        cc = (n_chunks - 1 - ci) if reverse else ci
        rows = pl.ds(pl.multiple_of(cc * c, c), c)
        for s in seqs:
            la_hi, la_lo = _split_bf16(la_ref[s, rows, :])
            fac = jnp.exp(_dot(m_ref[...], la_hi) + _dot(m_ref[...], la_lo))
            for h in range(GLA_HEADS):
                hs = slice(h * GLA_DK, (h + 1) * GLA_DK)
                vs = slice(h * GLA_DV, (h + 1) * GLA_DV)
                q = q_ref[s, rows, hs].astype(F32)
                k = k_ref[s, rows, hs].astype(F32)
                v = v_ref[s, rows, vs]
                scores = jnp.zeros((c, c), F32)
                for l in range(N_LEVELS):
                    is_query = ((row >> l) & 1) == (0 if reverse else 1)
                    f_l = fac[l * c:(l + 1) * c, hs]
                    g_l = (jnp.where(is_query, q, k) * f_l).astype(BF16)
                    scores = jnp.where(lvl == l, _dot_nt(g_l, g_l), scores)
                scores = jnp.where(lvl == N_LEVELS, _dot_nt(q.astype(BF16), k.astype(BF16)), scores)
                f_in = fac[INTER_BLK * c:(INTER_BLK + 1) * c, hs]
                f_st = fac[STATE_BLK * c:(STATE_BLK + 1) * c, hs]
                st = s_ref[s, h]
                inter = _dot((q * f_in).astype(BF16), st.astype(BF16))
                intra = _dot(scores.astype(BF16), v)
                o_ref[s, rows, vs] = (inter + intra).astype(o_ref.dtype)
                dec = f_in[total_row:total_row + 1, :]
                s_ref[s, h] = st * _decay_columns(dec) + _dot_tn((k * f_st).astype(BF16), v)
        return carry

    bounded = jnp.min(lamin_ref[...]) >= -GLA_FAST_MAX_DECAY / GLA_FAST_CHUNK

    @pl.when(bounded)
    def _():
        lax.fori_loop(0, n_fast, fast_factors, 0, unroll=4)
        lax.fori_loop(0, n_fast, fast_scores, 0, unroll=4)
        lax.fori_loop(0, n_fast, fast_state, 0, unroll=4)

    @pl.when(jnp.logical_not(bounded))
    def _():
        lax.fori_loop(0, tb // GLA_CHUNK, robust_chunk, 0)


def _gla(proj3, la3, lamin, reverse, tb, sb):
    b, l, _ = proj3.shape
    nb = l // tb
    lamin4 = lamin.reshape(b, -1, 8, 2 * QK_W)
    tiles = lamin4.shape[1] // nb
    blk = (lambda i: nb - 1 - i) if reverse else (lambda i: i)
    m_all = jnp.asarray(_decay_sum_matrix(reverse), BF16)
    lvl = jnp.asarray(_level_matrix(reverse))
    tri = jnp.asarray(_tri_matrix(reverse, GLA_FAST_CHUNK), BF16)
    kern = functools.partial(_gla_kernel, reverse=reverse, tb=tb, sb=sb)
    decay_col = 1 if reverse else 0
    return pl.pallas_call(
        kern,
        grid=(b // sb, nb),
        in_specs=[
            pl.BlockSpec((sb, tb, QK_W), lambda s, i: (s, blk(i), 0)),
            pl.BlockSpec((sb, tb, QK_W), lambda s, i: (s, blk(i), 1)),
            pl.BlockSpec((sb, tb, V_W), lambda s, i: (s, blk(i), 1)),
            pl.BlockSpec((sb, tb, QK_W), lambda s, i: (s, blk(i), decay_col)),
            pl.BlockSpec((sb, tiles, 8, QK_W), lambda s, i: (s, blk(i), 0, decay_col)),
            _resident(m_all), _resident(lvl), _resident(tri),
        ],
        out_specs=pl.BlockSpec((sb, tb, V_W), lambda s, i: (s, blk(i), 0)),
        out_shape=jax.ShapeDtypeStruct((b, l, V_W), BF16),
        scratch_shapes=[
            pltpu.VMEM((sb, GLA_HEADS, GLA_DK, GLA_DV), F32),
            pltpu.VMEM((sb, tb, 2 * QK_W), BF16),
            pltpu.VMEM((sb, tb, QK_W), BF16),
            pltpu.VMEM((sb, tb, QK_W), BF16),
            pltpu.VMEM((sb, tb, QK_W), BF16),
            pltpu.VMEM((sb, tb // GLA_FAST_CHUNK * 8, QK_W), F32),
        ],
        compiler_params=_params(("parallel", "arbitrary")),
        name="gla_bwd" if reverse else "gla_fwd",
    )(proj3, proj3, proj3, la3, lamin4, m_all, lvl, tri)


NA_GROUP = 8
NA_BLK = NA_GROUP * GRID_W
NA_BAND = NA_KR * GRID_W


def _na_bias_table(rpb):
    qc = np.arange(GRID_W)[:, None]
    kc = np.arange(GRID_W)[None, :]
    col_start = np.clip(qc - NA_KC // 2, 0, GRID_W - NA_KC)
    valid = (kc >= col_start) & (kc < col_start + NA_KC)
    dc = np.clip(kc - qc, -(NA_KC - 1), NA_KC - 1) + NA_KC - 1
    onehot = ((dc[None] == np.arange(2 * NA_KC - 1)[:, None, None]) & valid[None]).astype(np.float32)
    mask = np.where(valid, 0.0, NEG).astype(np.float32)
    rows = jnp.stack([rpb[:, NA_KR - 1 - d:2 * NA_KR - 1 - d, :] for d in range(NA_KR)])
    t = jnp.einsum('dhkc,cqj->dhqkj', rows, jnp.asarray(onehot), precision=lax.Precision.HIGHEST)
    t = t + jnp.asarray(mask)[None, None, :, None, :]
    return t.reshape(NA_KR, NA_HEADS // 2, 2 * GRID_W, NA_BAND)


def _na_kernel(q_ref, kp_ref, kc_ref, kn_ref, vp_ref, vc_ref, vn_ref, bias_ref, o_ref,
               kwin_ref, vwin_ref, *, rows):
    g = pl.program_id(1)
    for j, (kr, vr) in enumerate(((kp_ref, vp_ref), (kc_ref, vc_ref), (kn_ref, vn_ref))):
        kwin_ref[j * NA_BLK:(j + 1) * NA_BLK, :] = kr[0]
        vwin_ref[j * NA_BLK:(j + 1) * NA_BLK, :] = vr[0]
    lane = lax.broadcasted_iota(jnp.int32, (GRID_W, 2 * NA_DH), 1)
    first = lane < NA_DH

    def one_row(rl, carry):
        r = g * NA_GROUP + rl
        rs = jnp.clip(r - NA_KR // 2, 0, rows - NA_KR)
        d = r - rs
        off = pl.multiple_of((rs - (g - 1) * NA_GROUP) * GRID_W, GRID_W)
        qrows = pl.ds(pl.multiple_of(rl * GRID_W, GRID_W), GRID_W)
        pairs = [slice(p * 2 * NA_DH, (p + 1) * 2 * NA_DH) for p in range(NA_HEADS // 2)]
        scores = []
        for p, ps in enumerate(pairs):
            qp = q_ref[0, qrows, ps]
            zero = jnp.zeros_like(qp)
            qs = jnp.concatenate([jnp.where(first, qp, zero), jnp.where(first, zero, qp)], axis=0)
            kb = kwin_ref[pl.ds(off, NA_BAND), ps]
            scores.append(_dot_nt(qs, kb) + bias_ref[d, p])
        probs = []
        for s in scores:
            e = jnp.exp(s - jnp.max(s, axis=-1, keepdims=True))
            probs.append((e.astype(BF16), jnp.sum(e, axis=-1, keepdims=True)))
        for (e, den), ps in zip(probs, pairs):
            vb = vwin_ref[pl.ds(off, NA_BAND), ps]
            pv = _dot(e, vb) / den
            o_ref[0, qrows, ps] = jnp.where(first, pv[:GRID_W], pv[GRID_W:]).astype(o_ref.dtype)
        return carry

    lax.fori_loop(0, NA_GROUP, one_row, 0, unroll=8)


def _natten(proj3, bias):
    b, l, _ = proj3.shape
    rows = l // GRID_W
    ng = rows // NA_GROUP
    qcol, kcol, vcol = NQ_CHUNK, NQ_CHUNK + 1, NQ_CHUNK + 2
    prev = lambda i: jnp.maximum(i - 1, 0)
    nxt = lambda i: jnp.minimum(i + 1, ng - 1)
    spec = lambda f, col: pl.BlockSpec((1, NA_BLK, NA_W), lambda s, i: (s, f(i), col))
    same = lambda i: i
    return pl.pallas_call(
        functools.partial(_na_kernel, rows=rows),
        grid=(b, ng),
        in_specs=[
            spec(same, qcol),
            spec(prev, kcol), spec(same, kcol), spec(nxt, kcol),
            spec(prev, vcol), spec(same, vcol), spec(nxt, vcol),
            _resident(bias),
        ],
        out_specs=pl.BlockSpec((1, NA_BLK, NA_W), lambda s, i: (s, i, 0)),
        out_shape=jax.ShapeDtypeStruct((b, l, NA_W), BF16),
        scratch_shapes=[pltpu.VMEM((3 * NA_BLK, NA_W), BF16), pltpu.VMEM((3 * NA_BLK, NA_W), BF16)],
        compiler_params=_params(("parallel", "parallel")),
        name="natten",
    )(proj3, proj3, proj3, proj3, proj3, proj3, proj3, bias)


def _postmix_kernel(*refs, with_router):
    if with_router:
        (of_ref, ob_ref, r_ref, ga_ref, gb_ref, na_ref, x_ref, gog_ref, wog_ref, won_ref, wout_ref,
         gf_ref, rh_ref, rl_ref, x1_ref, h2_ref, comb_ref, slot_ref) = refs
    else:
        (of_ref, ob_ref, r_ref, ga_ref, gb_ref, na_ref, x_ref, gog_ref, wog_ref, won_ref, wout_ref,
         gf_ref, x1_ref, h2_ref) = refs
    o = of_ref[...].astype(F32) + ob_ref[...].astype(F32)
    parts = []
    for h in range(GLA_HEADS):
        seg = o[:, h * GLA_DV:(h + 1) * GLA_DV]
        ms = jnp.mean(seg * seg, axis=-1, keepdims=True)
        parts.append(seg * lax.rsqrt(ms + EPS))
    r = r_ref[...].astype(F32)
    on = jnp.concatenate(parts, axis=-1) * gog_ref[...] * (r * _sigmoid(r))
    ya = _dot(on.astype(BF16), wog_ref[...])
    yb = _dot(na_ref[...], won_ref[...])
    merged = _sigmoid(ga_ref[...].astype(F32)) * ya + _sigmoid(gb_ref[...].astype(F32)) * yb
    x1 = x_ref[...] + _dot(merged.astype(BF16), wout_ref[...])
    x1_ref[...] = x1
    h2 = _rmsnorm(x1, gf_ref[...])
    h2_ref[...] = h2.astype(h2_ref.dtype)
    if with_router:
        hh, hl = _split_bf16(h2)
        lg2 = _dot_nt(rh_ref[...], hh) + _dot_nt(rl_ref[...], hl)
        lg = lg2[:N_EXPERTS] + lg2[N_EXPERTS:]
        row = lax.broadcasted_iota(jnp.int32, lg.shape, 0).astype(F32)
        m1 = jnp.max(lg, axis=0, keepdims=True)
        i1 = jnp.min(jnp.where(lg == m1, row, float(N_EXPERTS)), axis=0, keepdims=True)
        lg_rest = jnp.where(row == i1, -jnp.inf, lg)
        m2 = jnp.max(lg_rest, axis=0, keepdims=True)
        i2 = jnp.min(jnp.where(lg_rest == m2, row, float(N_EXPERTS)), axis=0, keepdims=True)
        t = jnp.exp(m2 - m1)
        w1 = 1.0 / (1.0 + t)
        comb_ref[...] = jnp.where(row == i1, w1, 0.0) + jnp.where(row == i2, t * w1, 0.0)
        slot_ref[...] = jnp.where(row == i1, 1.0, 0.0) + jnp.where(row == i2, 2.0, 0.0)


def _postmix(of, ob, proj, na, x, gog, wog, won, wout, gf, router, tm):
    n, d = x.shape
    with_router = router is not None
    in_specs = [
        pl.BlockSpec((tm, V_W), lambda i: (i, 0)),
        pl.BlockSpec((tm, V_W), lambda i: (i, 0)),
        pl.BlockSpec((tm, 1024), lambda i: (i, 2)),
        pl.BlockSpec((tm, 1024), lambda i: (i, 3)),
        pl.BlockSpec((tm, 1024), lambda i: (i, 4)),
        pl.BlockSpec((tm, NA_W), lambda i: (i, 0)),
        pl.BlockSpec((tm, d), lambda i: (i, 0)),
        _resident(gog), _resident(wog), _resident(won), _resident(wout), _resident(gf),
    ]
    args = [of, ob, proj, proj, proj, na, x, gog, wog, won, wout, gf]
    out_specs = [pl.BlockSpec((tm, d), lambda i: (i, 0)), pl.BlockSpec((tm, d), lambda i: (i, 0))]
    out_shape = [jax.ShapeDtypeStruct((n, d), F32), jax.ShapeDtypeStruct((n, d), F32 if with_router else BF16)]
    if with_router:
        in_specs += [_resident(router[0]), _resident(router[1])]
        args += list(router)
        for _ in range(2):
            out_specs.append(pl.BlockSpec((N_EXPERTS, tm), lambda i: (0, i)))
            out_shape.append(jax.ShapeDtypeStruct((N_EXPERTS, n), F32))
    return pl.pallas_call(
        functools.partial(_postmix_kernel, with_router=with_router),
        grid=(n // tm,),
        in_specs=in_specs,
        out_specs=out_specs,
        out_shape=out_shape,
        compiler_params=_params(("parallel",)),
        name="postmix_router" if with_router else "postmix",
    )(*args)


def _ffn_kernel(x1_ref, h_ref, wg_ref, wu_ref, wd_ref, o_ref, acc_ref):
    j = pl.program_id(1)

    @pl.when(j == 0)
    def _():
        acc_ref[...] = jnp.zeros_like(acc_ref)

    h = h_ref[...]
    g = _dot(h, wg_ref[...])
    u = _dot(h, wu_ref[...])
    a = (g * _sigmoid(g) * u).astype(BF16)
    acc_ref[...] += _dot(a, wd_ref[...])

    @pl.when(j == pl.num_programs(1) - 1)
    def _():
        o_ref[...] = x1_ref[...] + acc_ref[...]


def _ffn(x1, h2, wg, wu, wd, tm, tf):
    n, d = x1.shape
    dff = wg.shape[1]
    return pl.pallas_call(
        _ffn_kernel,
        grid=(n // tm, dff // tf),
        in_specs=[
            pl.BlockSpec((tm, d), lambda i, j: (i, 0)),
            pl.BlockSpec((tm, d), lambda i, j: (i, 0)),
            pl.BlockSpec((d, tf), lambda i, j: (0, j)),
            pl.BlockSpec((d, tf), lambda i, j: (0, j)),
            pl.BlockSpec((tf, d), lambda i, j: (j, 0)),
        ],
        out_specs=pl.BlockSpec((tm, d), lambda i, j: (i, 0)),
        out_shape=jax.ShapeDtypeStruct((n, d), F32),
        scratch_shapes=[pltpu.VMEM((tm, d), F32)],
        compiler_params=_params(("parallel", "arbitrary")),
        name="ffn",
    )(x1, h2, wg, wu, wd)


SC_CORES = 2
SC_SUBCORES = 16
SC_WORKERS = SC_CORES * SC_SUBCORES
SC_CHUNK = 32
MOE_TM = 512


def _sc_gather_rows(table, idx):
    nrow, d = idx.shape[0], table.shape[1]
    assert nrow % (SC_WORKERS * SC_CHUNK) == 0
    per_worker = nrow // SC_WORKERS
    mesh = plsc.VectorSubcoreMesh(core_axis_name="c", subcore_axis_name="s",
                                  num_cores=SC_CORES, num_subcores=SC_SUBCORES)

    @functools.partial(
        pl.kernel, mesh=mesh, out_type=jax.ShapeDtypeStruct((nrow, d), table.dtype),
        scratch_types=[pltpu.VMEM((SC_CHUNK,), jnp.int32), pltpu.VMEM((SC_CHUNK, d), table.dtype),
                       pltpu.SemaphoreType.DMA])
    def gather(table_hbm, idx_hbm, out_hbm, idx_v, rows_v, sem):
        base = (lax.axis_index("s") * SC_CORES + lax.axis_index("c")) * per_worker

        @pl.loop(0, per_worker // SC_CHUNK)
        def _(j):
            off = base + j * SC_CHUNK
            pltpu.sync_copy(idx_hbm.at[pl.ds(off, SC_CHUNK)], idx_v)
            pltpu.async_copy(table_hbm.at[idx_v], rows_v, sem).wait()
            pltpu.sync_copy(rows_v, out_hbm.at[pl.ds(off, SC_CHUNK)])

    return gather(table, idx)


def _sc_scatter_rows(rows, pos12, n_out):
    n, d = rows.shape
    assert n % (SC_WORKERS * SC_CHUNK) == 0
    per_worker = n // SC_WORKERS
    idx = pos12.reshape(2, n)
    mesh = plsc.VectorSubcoreMesh(core_axis_name="c", subcore_axis_name="s",
                                  num_cores=SC_CORES, num_subcores=SC_SUBCORES)

    @functools.partial(
        pl.kernel, mesh=mesh, out_type=jax.ShapeDtypeStruct((n_out, d), rows.dtype),
        scratch_types=[pltpu.VMEM((2, SC_CHUNK), jnp.int32), pltpu.VMEM((SC_CHUNK, d), rows.dtype)])
    def scatter(rows_hbm, idx_hbm, out_hbm, idx_v, rows_v):
        base = (lax.axis_index("s") * SC_CORES + lax.axis_index("c")) * per_worker

        @pl.loop(0, per_worker // SC_CHUNK)
        def _(j):
            off = base + j * SC_CHUNK
            pltpu.sync_copy(rows_hbm.at[pl.ds(off, SC_CHUNK)], rows_v)
            for k in range(2):
                pltpu.sync_copy(idx_hbm.at[k, pl.ds(off, SC_CHUNK)], idx_v.at[k])
                pltpu.sync_copy(rows_v, out_hbm.at[idx_v.at[k]])

    return scatter(rows, idx)


def _moe_plan(comb, slot):
    n = slot.shape[1]
    sel = (slot > 0).astype(jnp.int32)
    count = jnp.sum(sel, axis=1)
    group = (count + MOE_TM - 1) // MOE_TM * MOE_TM
    ends = jnp.cumsum(group)
    pos = (ends - group)[:, None] + jnp.cumsum(sel, axis=1) - sel
    pos12 = jnp.concatenate([jnp.sum(jnp.where(slot == 1.0, pos, 0), axis=0),
                             jnp.sum(jnp.where(slot == 2.0, pos, 0), axis=0)]).astype(jnp.int32)
    n_rows = 2 * n + N_EXPERTS * MOE_TM
    tile_start = jnp.arange(n_rows // MOE_TM, dtype=jnp.int32) * MOE_TM
    tile_expert = jnp.minimum(jnp.sum(tile_start[:, None] >= ends[None, :], axis=1), N_EXPERTS - 1)
    filled = (ends - group + count)[tile_expert] - tile_start
    tile_rows = jnp.where(tile_start < ends[-1], jnp.clip(filled, 0, MOE_TM), 0).astype(jnp.int32)
    w12 = jnp.stack([jnp.sum(jnp.where(slot == 1.0, comb, 0.0), axis=0),
                     jnp.sum(jnp.where(slot == 2.0, comb, 0.0), axis=0)], axis=1)
    return pos12, n_rows, tile_expert.astype(jnp.int32), tile_rows, w12


def _moe_group_kernel(te_ref, tr_ref, xs_ref, wg_ref, wu_ref, wd_ref, ys_ref):
    n_filled = tr_ref[pl.program_id(0)]
    used = n_filled > 0

    @pl.when(used)
    def _():
        row = lax.broadcasted_iota(jnp.int32, xs_ref.shape, 0)
        x = jnp.where(row < n_filled, xs_ref[...], 0.0).astype(BF16)
        g = _dot(x, wg_ref[0])
        u = _dot(x, wu_ref[0])
        ys_ref[...] = _dot((g * _sigmoid(g) * u).astype(BF16), wd_ref[0])

    @pl.when(jnp.logical_not(used))
    def _():
        ys_ref[...] = jnp.zeros_like(ys_ref)


def _moe_grouped(xs, tile_expert, tile_rows, wg, wu, wd):
    p, d = xs.shape
    _, _, dfe = wg.shape
    grid_spec = pltpu.PrefetchScalarGridSpec(
        num_scalar_prefetch=2,
        grid=(p // MOE_TM,),
        in_specs=[
            pl.BlockSpec((MOE_TM, d), lambda t, te, tr: (t, 0)),
            pl.BlockSpec((1, d, dfe), lambda t, te, tr: (te[t], 0, 0)),
            pl.BlockSpec((1, d, dfe), lambda t, te, tr: (te[t], 0, 0)),
            pl.BlockSpec((1, dfe, d), lambda t, te, tr: (te[t], 0, 0)),
        ],
        out_specs=pl.BlockSpec((MOE_TM, d), lambda t, te, tr: (t, 0)),
    )
    return pl.pallas_call(
        _moe_group_kernel,
        grid_spec=grid_spec,
        out_shape=jax.ShapeDtypeStruct((p, d), F32),
        compiler_params=_params(("arbitrary",)),
        name="moe_grouped",
    )(tile_expert, tile_rows, xs, wg, wu, wd)


def _moe_combine_kernel(x1_ref, y1_ref, y2_ref, w_ref, o_ref):
    w = w_ref[...]
    o_ref[...] = x1_ref[...] + w[:, 0:1] * y1_ref[...] + w[:, 1:2] * y2_ref[...]


def _moe_combine(x1, y, w12, tm):
    n, d = x1.shape
    nt = n // tm
    return pl.pallas_call(
        _moe_combine_kernel,
        grid=(nt,),
        in_specs=[
            pl.BlockSpec((tm, d), lambda i: (i, 0)),
            pl.BlockSpec((tm, d), lambda i: (i, 0)),
            pl.BlockSpec((tm, d), lambda i: (i + nt, 0)),
            pl.BlockSpec((tm, 2), lambda i: (i, 0)),
        ],
        out_specs=pl.BlockSpec((tm, d), lambda i: (i, 0)),
        out_shape=jax.ShapeDtypeStruct((n, d), F32),
        compiler_params=_params(("parallel",)),
        name="moe_combine",
    )(x1, y, y, w12)


def _moe(x1, h2, comb, slot, wg, wu, wd):
    pos12, n_rows, tile_expert, tile_rows, w12 = _moe_plan(comb, slot)
    xs = _sc_scatter_rows(h2, pos12, n_rows)
    ys = _moe_grouped(xs, tile_expert, tile_rows, wg, wu, wd)
    y = _sc_gather_rows(ys, pos12)
    return _moe_combine(x1, y, w12, _row_tile(x1.shape[0], 1024))


def _row_tile(n, want):
    t = min(n, want)
    assert n % t == 0
    return t


def _prep_layer(l, norm_mix, w_in, w_decay_f, b_decay_f, w_decay_b, b_decay_b, gla_out_norm, q_norm,
                k_norm, rpb, w_o_gla, w_o_na, w_out, norm_ffn):
    w = w_in[l]
    c0 = 2 * QK_W + 2 * V_W
    c1 = c0 + 2 * GLA_RANK
    c2 = c1 + 3 * NA_W
    w_main = jnp.concatenate([w[:, :QK_W] * (GLA_DK ** -0.5), w[:, QK_W:c0], w[:, c2:], w[:, c1:c2]],
                             axis=1).astype(BF16)
    wz = w[:, c0:c1].astype(BF16)
    zero = jnp.zeros((GLA_RANK, QK_W), F32)
    wdec = jnp.concatenate([jnp.concatenate([w_decay_f[l], zero], axis=1),
                            jnp.concatenate([zero, w_decay_b[l]], axis=1)], axis=0)
    bdec = jnp.concatenate([b_decay_f[l], b_decay_b[l]])[None, :]
    qg = jnp.tile(q_norm[l] * (NA_DH ** -0.5), NA_HEADS)[None, :]
    kg = jnp.tile(k_norm[l], NA_HEADS)[None, :]
    return dict(
        g_mix=norm_mix[l][None, :], w_main=w_main, wz=wz, wd=wdec.astype(BF16), bdec=bdec,
        qg=qg, kg=kg, bias=_na_bias_table(rpb[l]),
        gog=gla_out_norm[l].reshape(1, V_W), wog=w_o_gla[l].astype(BF16),
        won=w_o_na[l].astype(BF16), wout=w_out[l].astype(BF16), g_ffn=norm_ffn[l][None, :])


def _router_operands(router):
    hi, lo = _split_bf16(router.T)
    return jnp.concatenate([hi, lo], axis=0), jnp.concatenate([hi, jnp.zeros_like(hi)], axis=0)


def _trunk(x, layers, dense, moe):
    b, l, d = x.shape
    n = b * l
    xf = x.reshape(n, d)
    head_mean = jnp.asarray(np.kron(np.eye(NA_HEADS), np.full((NA_DH, NA_DH), 1.0 / NA_DH)), BF16)
    tb = _row_tile(l, 512)
    sb = 2 if b % 2 == 0 else 1
    for li, p in enumerate(layers):
        proj, la, lamin = _inproj(xf, p["g_mix"], p["w_main"], p["wz"], p["wd"], p["bdec"],
                           head_mean, p["qg"], p["kg"], _row_tile(n, 512))
        proj3 = proj.reshape(b, l, PROJ_W)
        la3 = la.reshape(b, l, 2 * QK_W)
        of = _gla(proj3, la3, lamin, False, tb, sb).reshape(n, V_W)
        ob = _gla(proj3, la3, lamin, True, tb, sb).reshape(n, V_W)
        na = _natten(proj3, p["bias"]).reshape(n, NA_W)
        if li % 2 == 0:
            wg, wu, wd = dense[li // 2]
            x1, h2 = _postmix(of, ob, proj, na, xf, p["gog"], p["wog"], p["won"], p["wout"],
                              p["g_ffn"], None, _row_tile(n, 512))
            xf = _ffn(x1, h2, wg, wu, wd, _row_tile(n, 1024), wg.shape[1] // 2)
        else:
            router, wg, wu, wd = moe[li // 2]
            x1, h2, comb, slot = _postmix(of, ob, proj, na, xf, p["gog"], p["wog"], p["won"], p["wout"],
                                          p["g_ffn"], router, _row_tile(n, 512))
            xf = _moe(x1, h2, comb, slot, wg, wu, wd)
    return xf.reshape(b, l, d)


def kernel(x_prompt, x_sample, norm_mix, w_in, w_decay_f, b_decay_f, w_decay_b, b_decay_b, gla_out_norm,
           q_norm, k_norm, rpb, w_o_gla, w_o_na, w_out, norm_ffn, ffn_w_gate, ffn_w_up, ffn_w_down,
           moe_router, moe_w_gate, moe_w_up, moe_w_down):
    depth = w_in.shape[0]
    layers = [_prep_layer(l, norm_mix, w_in, w_decay_f, b_decay_f, w_decay_b, b_decay_b, gla_out_norm,
                          q_norm, k_norm, rpb, w_o_gla, w_o_na, w_out, norm_ffn) for l in range(depth)]
    dense = [(ffn_w_gate[j].astype(BF16), ffn_w_up[j].astype(BF16), ffn_w_down[j].astype(BF16))
             for j in range(ffn_w_gate.shape[0])]
    moe = [(_router_operands(moe_router[j]), moe_w_gate[j].astype(BF16), moe_w_up[j].astype(BF16),
            moe_w_down[j].astype(BF16)) for j in range(moe_router.shape[0])]
    return (_trunk(x_prompt, layers, dense, moe), _trunk(x_sample, layers, dense, moe))
```

```python
import functools

import numpy as np
import jax
import jax.numpy as jnp
from jax import lax
from jax.experimental import pallas as pl
from jax.experimental.pallas import tpu as pltpu
from jax.experimental.pallas import tpu_sc as plsc

F32 = jnp.float32
BF16 = jnp.bfloat16

EPS = 1e-6
GRID_W = 64
GLA_HEADS = 4
GLA_DK = 128
GLA_DV = 256
GLA_CHUNK = 64
GLA_RANK = 16
GLA_TAU = 16.0
NA_HEADS = 8
NA_DH = 64
NA_KR = 8
NA_KC = 16
N_EXPERTS = 8
NEG = -1e30
LOG2_E = 1.4426950408889634

VMEM_LIMIT = 56 * 1024 * 1024

QK_W = GLA_HEADS * GLA_DK
V_W = GLA_HEADS * GLA_DV
NA_W = NA_HEADS * NA_DH
PROJ_W = 2 * QK_W + 2 * V_W + 2 * 1024 + 3 * NA_W
CW = 512
NQ_CHUNK = (2 * QK_W + 2 * V_W + 2048) // CW
NK_CHUNK = NQ_CHUNK + 1
INPROJ_SUB = 256


def _params(sem):
    return pltpu.CompilerParams(dimension_semantics=sem, vmem_limit_bytes=VMEM_LIMIT)


def _resident(a):
    nd = a.ndim
    return pl.BlockSpec(a.shape, lambda *_: (0,) * nd, pipeline_mode=pl.Buffered(1))


def _split_bf16(a):
    hi = a.astype(BF16)
    lo = (a - hi.astype(F32)).astype(BF16)
    return hi, lo


def _dot(a, b):
    return jnp.dot(a, b, preferred_element_type=F32)


def _dot_nt(a, b):
    return lax.dot_general(a, b, (((1,), (1,)), ((), ())), preferred_element_type=F32)


def _dot_tn(a, b):
    return lax.dot_general(a, b, (((0,), (0,)), ((), ())), preferred_element_type=F32)


def _sigmoid(x):
    return 1.0 / (1.0 + jnp.exp2(x * (-LOG2_E)))


def _rmsnorm(x, g):
    ms = jnp.mean(x * x, axis=-1, keepdims=True)
    return x * lax.rsqrt(ms + EPS) * g


def _inproj_kernel(x_ref, g_ref, w_ref, wz_ref, wd_ref, bdec_ref, hm_ref, qg_ref, kg_ref,
                   proj_ref, la_ref, lamin_ref):
    for t in range(x_ref.shape[0] // INPROJ_SUB):
        rs = slice(t * INPROJ_SUB, (t + 1) * INPROJ_SUB)
        h = _rmsnorm(x_ref[rs, :], g_ref[...]).astype(BF16)
        z = _dot(h, wz_ref[...])
        y = _dot(z.astype(BF16), wd_ref[...]) + bdec_ref[...]
        ls = jnp.minimum(y, 0.0) - jnp.log(1.0 + jnp.exp(-jnp.abs(y)))
        la = ls * (1.0 / GLA_TAU)
        la_ref[rs, :] = la
        lamin_ref[t] = jnp.broadcast_to(jnp.min(la, axis=0, keepdims=True), lamin_ref.shape[1:])
        for c in range(PROJ_W // CW):
            acc = _dot(h, w_ref[:, c * CW:(c + 1) * CW])
            if c in (NQ_CHUNK, NK_CHUNK):
                gain = qg_ref if c == NQ_CHUNK else kg_ref
                ms = _dot((acc * acc).astype(BF16), hm_ref[...])
                acc = acc * lax.rsqrt(ms + EPS) * gain[...]
            proj_ref[rs, c * CW:(c + 1) * CW] = acc.astype(BF16)


def _inproj(x, g, w_main, wz, wd, bdec, hm, qg, kg, tm):
    n, d = x.shape
    return pl.pallas_call(
        _inproj_kernel,
        grid=(n // tm,),
        in_specs=[
            pl.BlockSpec((tm, d), lambda i: (i, 0)),
            _resident(g), _resident(w_main), _resident(wz), _resident(wd),
            _resident(bdec), _resident(hm), _resident(qg), _resident(kg),
        ],
        out_specs=[
            pl.BlockSpec((tm, PROJ_W), lambda i: (i, 0)),
            pl.BlockSpec((tm, 2 * QK_W), lambda i: (i, 0)),
            pl.BlockSpec((tm // INPROJ_SUB, 8, 2 * QK_W), lambda i: (i, 0, 0)),
        ],
        out_shape=[
            jax.ShapeDtypeStruct((n, PROJ_W), BF16),
            jax.ShapeDtypeStruct((n, 2 * QK_W), F32),
            jax.ShapeDtypeStruct((n // INPROJ_SUB, 8, 2 * QK_W), F32),
        ],
        compiler_params=_params(("parallel",)),
        name="inproj",
    )(x, g, w_main, wz, wd, bdec, hm, qg, kg)


N_LEVELS = 6
INTER_BLK = N_LEVELS
STATE_BLK = N_LEVELS + 1


def _decay_sum_matrix(reverse):
    c = GLA_CHUNK
    m_all = np.zeros((N_LEVELS + 2, c, c), np.float32)
    for l in range(N_LEVELS):
        m = 1 << l
        for p in range(c):
            mid = (p // (2 * m)) * 2 * m + m
            if p >= mid:
                m_all[l, p, mid + 1:p + 1] = 1.0
            else:
                m_all[l, p, p + 1:mid + 1] = 1.0
    for p in range(c):
        m_all[INTER_BLK, p, :p + 1] = 1.0
        m_all[STATE_BLK, p, p + 1:] = 1.0
    if reverse:
        m_all = m_all[:, ::-1, ::-1]
    return np.ascontiguousarray(m_all).reshape((N_LEVELS + 2) * c, c)


def _level_matrix(reverse):
    i = np.arange(GLA_CHUNK)[:, None]
    j = np.arange(GLA_CHUNK)[None, :]
    x = i ^ j
    lvl = np.where(x > 0, np.floor(np.log2(np.maximum(x, 1))), N_LEVELS).astype(np.int32)
    valid = (i <= j) if reverse else (i >= j)
    return np.where(valid, lvl, N_LEVELS + 1).astype(np.int32)


GLA_FAST_CHUNK = 128
GLA_FAST_MAX_DECAY = 60.0


def _tri_matrix(reverse, c):
    i = np.arange(c)[:, None]
    j = np.arange(c)[None, :]
    return ((j >= i) if reverse else (j <= i)).astype(np.float32)


def _decay_columns(dec_row):
    t = jnp.broadcast_to(dec_row, (GLA_DK, GLA_DK)).T
    return jnp.concatenate([t, t], axis=1)


def _gla_kernel(q_ref, k_ref, v_ref, la_ref, lamin_ref, m_ref, lvl_ref, tri_ref, o_ref,
                s_ref, lr_ref, qh_ref, kh_ref, ks_ref, dec_ref, *, reverse, tb, sb):
    @pl.when(pl.program_id(1) == 0)
    def _():
        s_ref[...] = jnp.zeros_like(s_ref)

    fc = GLA_FAST_CHUNK
    n_fast = tb // fc
    total_row = 0 if reverse else fc - 1
    seqs = range(sb)

    def fast_rows(ci):
        return pl.ds(pl.multiple_of(ci * fc, fc), fc)

    def sc_lanes(h):
        return slice(h * 2 * GLA_DK, h * 2 * GLA_DK + GLA_DK)

    def qh_lanes(h):
        return slice(h * 2 * GLA_DK + GLA_DK, (h + 1) * 2 * GLA_DK)

    def fast_factors(ci, carry):
        rows = fast_rows(ci)
        for s in seqs:
            b = _dot(tri_ref[...], la_ref[s, rows, :].astype(BF16))
            dec = jnp.exp(b[total_row:total_row + 1, :])
            f_q = jnp.exp(b)
            qh = (q_ref[s, rows, :].astype(F32) * f_q).astype(BF16)
            kh = k_ref[s, rows, :].astype(F32) / f_q
            qh_ref[s, rows, :] = qh
            for h in range(GLA_HEADS):
                lr_ref[s, rows, qh_lanes(h)] = qh[:, h * GLA_DK:(h + 1) * GLA_DK]
            kh_ref[s, rows, :] = kh.astype(BF16)
            ks_ref[s, rows, :] = (kh * dec).astype(BF16)
            dec_ref[s, pl.ds(pl.multiple_of(ci * 8, 8), 8), :] = jnp.broadcast_to(dec, (8, QK_W))
        return carry

    def fast_scores(ci, carry):
        rows = fast_rows(ci)
        i = lax.broadcasted_iota(jnp.int32, (fc, fc), 0)
        j = lax.broadcasted_iota(jnp.int32, (fc, fc), 1)
        causal = (i <= j) if reverse else (i >= j)
        for s in seqs:
            for h in range(GLA_HEADS):
                hs = slice(h * GLA_DK, (h + 1) * GLA_DK)
                sc = _dot_nt(qh_ref[s, rows, hs], kh_ref[s, rows, hs])
                lr_ref[s, rows, sc_lanes(h)] = jnp.where(causal, sc, 0.0).astype(BF16)
        return carry

    def fast_state(ci, carry):
        cc = (n_fast - 1 - ci) if reverse else ci
        rows = fast_rows(cc)
        heads = [(s, h) for s in seqs for h in range(GLA_HEADS)]
        vals = {(s, h): v_ref[s, rows, h * GLA_DV:(h + 1) * GLA_DV] for s, h in heads}
        kv = {(s, h): _dot_tn(ks_ref[s, rows, h * GLA_DK:(h + 1) * GLA_DK], vals[s, h]) for s, h in heads}
        for s, h in heads:
            dec = dec_ref[s, pl.ds(pl.multiple_of(cc * 8, 8), 8), :][:1, h * GLA_DK:(h + 1) * GLA_DK]
            st = s_ref[s, h]
            o = _dot(lr_ref[s, rows, h * 2 * GLA_DK:(h + 1) * 2 * GLA_DK],
                     jnp.concatenate([vals[s, h], st.astype(BF16)], axis=0))
            o_ref[s, rows, h * GLA_DV:(h + 1) * GLA_DV] = o.astype(o_ref.dtype)
            s_ref[s, h] = st * _decay_columns(dec) + kv[s, h]
        return carry

    def robust_chunk(ci, carry):
        c = GLA_CHUNK
        n_chunks = tb // c
        lvl = lvl_ref[...]
        row = lax.broadcasted_iota(jnp.int32, (c, GLA_DK), 0)
        total_row = 0 if reverse else c - 1
        cc = (n_chunks - 1 - ci) if reverse else ci
        rows = pl.ds(pl.multiple_of(cc * c, c), c)
        for s in seqs:
            la_hi, la_lo = _split_bf16(la_ref[s, rows, :])
            fac = jnp.exp(_dot(m_ref[...], la_hi) + _dot(m_ref[...], la_lo))
            for h in range(GLA_HEADS):
                hs = slice(h * GLA_DK, (h + 1) * GLA_DK)
                vs = slice(h * GLA_DV, (h + 1) * GLA_DV)
                q = q_ref[s, rows, hs].astype(F32)
                k = k_ref[s, rows, hs].astype(F32)
                v = v_ref[s, rows, vs]
                scores = jnp.zeros((c, c), F32)
                for l in range(N_LEVELS):
                    is_query = ((row >> l) & 1) == (0 if reverse else 1)
                    f_l = fac[l * c:(l + 1) * c, hs]
                    g_l = (jnp.where(is_query, q, k) * f_l).astype(BF16)
                    scores = jnp.where(lvl == l, _dot_nt(g_l, g_l), scores)
                scores = jnp.where(lvl == N_LEVELS, _dot_nt(q.astype(BF16), k.astype(BF16)), scores)
                f_in = fac[INTER_BLK * c:(INTER_BLK + 1) * c, hs]
                f_st = fac[STATE_BLK * c:(STATE_BLK + 1) * c, hs]
                st = s_ref[s, h]
                inter = _dot((q * f_in).astype(BF16), st.astype(BF16))
                intra = _dot(scores.astype(BF16), v)
                o_ref[s, rows, vs] = (inter + intra).astype(o_ref.dtype)
                dec = f_in[total_row:total_row + 1, :]
                s_ref[s, h] = st * _decay_columns(dec) + _dot_tn((k * f_st).astype(BF16), v)
        return carry

    bounded = jnp.min(lamin_ref[...]) >= -GLA_FAST_MAX_DECAY / GLA_FAST_CHUNK

    @pl.when(bounded)
    def _():
        lax.fori_loop(0, n_fast, fast_factors, 0, unroll=4)
        lax.fori_loop(0, n_fast, fast_scores, 0, unroll=4)
        lax.fori_loop(0, n_fast, fast_state, 0, unroll=4)

    @pl.when(jnp.logical_not(bounded))
    def _():
        lax.fori_loop(0, tb // GLA_CHUNK, robust_chunk, 0)


def _gla(proj3, la3, lamin, reverse, tb, sb):
    b, l, _ = proj3.shape
    nb = l // tb
    lamin4 = lamin.reshape(b, -1, 8, 2 * QK_W)
    tiles = lamin4.shape[1] // nb
    blk = (lambda i: nb - 1 - i) if reverse else (lambda i: i)
    m_all = jnp.asarray(_decay_sum_matrix(reverse), BF16)
    lvl = jnp.asarray(_level_matrix(reverse))
    tri = jnp.asarray(_tri_matrix(reverse, GLA_FAST_CHUNK), BF16)
    kern = functools.partial(_gla_kernel, reverse=reverse, tb=tb, sb=sb)
    decay_col = 1 if reverse else 0
    return pl.pallas_call(
        kern,
        grid=(b // sb, nb),
        in_specs=[
            pl.BlockSpec((sb, tb, QK_W), lambda s, i: (s, blk(i), 0)),
            pl.BlockSpec((sb, tb, QK_W), lambda s, i: (s, blk(i), 1)),
            pl.BlockSpec((sb, tb, V_W), lambda s, i: (s, blk(i), 1)),
            pl.BlockSpec((sb, tb, QK_W), lambda s, i: (s, blk(i), decay_col)),
            pl.BlockSpec((sb, tiles, 8, QK_W), lambda s, i: (s, blk(i), 0, decay_col)),
            _resident(m_all), _resident(lvl), _resident(tri),
        ],
        out_specs=pl.BlockSpec((sb, tb, V_W), lambda s, i: (s, blk(i), 0)),
        out_shape=jax.ShapeDtypeStruct((b, l, V_W), BF16),
        scratch_shapes=[
            pltpu.VMEM((sb, GLA_HEADS, GLA_DK, GLA_DV), F32),
            pltpu.VMEM((sb, tb, 2 * QK_W), BF16),
            pltpu.VMEM((sb, tb, QK_W), BF16),
            pltpu.VMEM((sb, tb, QK_W), BF16),
            pltpu.VMEM((sb, tb, QK_W), BF16),
            pltpu.VMEM((sb, tb // GLA_FAST_CHUNK * 8, QK_W), F32),
        ],
        compiler_params=_params(("parallel", "arbitrary")),
        name="gla_bwd" if reverse else "gla_fwd",
    )(proj3, proj3, proj3, la3, lamin4, m_all, lvl, tri)


NA_GROUP = 8
NA_BLK = NA_GROUP * GRID_W
NA_BAND = NA_KR * GRID_W


def _na_bias_table(rpb):
    qc = np.arange(GRID_W)[:, None]
    kc = np.arange(GRID_W)[None, :]
    col_start = np.clip(qc - NA_KC // 2, 0, GRID_W - NA_KC)
    valid = (kc >= col_start) & (kc < col_start + NA_KC)
    dc = np.clip(kc - qc, -(NA_KC - 1), NA_KC - 1) + NA_KC - 1
    onehot = ((dc[None] == np.arange(2 * NA_KC - 1)[:, None, None]) & valid[None]).astype(np.float32)
    mask = np.where(valid, 0.0, NEG).astype(np.float32)
    rows = jnp.stack([rpb[:, NA_KR - 1 - d:2 * NA_KR - 1 - d, :] for d in range(NA_KR)])
    t = jnp.einsum('dhkc,cqj->dhqkj', rows, jnp.asarray(onehot), precision=lax.Precision.HIGHEST)
    t = t + jnp.asarray(mask)[None, None, :, None, :]
    return t.reshape(NA_KR, NA_HEADS // 2, 2 * GRID_W, NA_BAND)


def _na_kernel(q_ref, kp_ref, kc_ref, kn_ref, vp_ref, vc_ref, vn_ref, bias_ref, o_ref,
               kwin_ref, vwin_ref, *, rows):
    g = pl.program_id(1)
    for j, (kr, vr) in enumerate(((kp_ref, vp_ref), (kc_ref, vc_ref), (kn_ref, vn_ref))):
        kwin_ref[j * NA_BLK:(j + 1) * NA_BLK, :] = kr[0]
        vwin_ref[j * NA_BLK:(j + 1) * NA_BLK, :] = vr[0]
    lane = lax.broadcasted_iota(jnp.int32, (GRID_W, 2 * NA_DH), 1)
    first = lane < NA_DH

    def one_row(rl, carry):
        r = g * NA_GROUP + rl
        rs = jnp.clip(r - NA_KR // 2, 0, rows - NA_KR)
        d = r - rs
        off = pl.multiple_of((rs - (g - 1) * NA_GROUP) * GRID_W, GRID_W)
        qrows = pl.ds(pl.multiple_of(rl * GRID_W, GRID_W), GRID_W)
        pairs = [slice(p * 2 * NA_DH, (p + 1) * 2 * NA_DH) for p in range(NA_HEADS // 2)]
        scores = []
        for p, ps in enumerate(pairs):
            qp = q_ref[0, qrows, ps]
            zero = jnp.zeros_like(qp)
            qs = jnp.concatenate([jnp.where(first, qp, zero), jnp.where(first, zero, qp)], axis=0)
            kb = kwin_ref[pl.ds(off, NA_BAND), ps]
            scores.append(_dot_nt(qs, kb) + bias_ref[d, p])
        probs = []
        for s in scores:
            e = jnp.exp(s - jnp.max(s, axis=-1, keepdims=True))
            probs.append((e.astype(BF16), jnp.sum(e, axis=-1, keepdims=True)))
        for (e, den), ps in zip(probs, pairs):
            vb = vwin_ref[pl.ds(off, NA_BAND), ps]
            pv = _dot(e, vb) / den
            o_ref[0, qrows, ps] = jnp.where(first, pv[:GRID_W], pv[GRID_W:]).astype(o_ref.dtype)
        return carry

    lax.fori_loop(0, NA_GROUP, one_row, 0, unroll=8)


def _natten(proj3, bias):
    b, l, _ = proj3.shape
    rows = l // GRID_W
    ng = rows // NA_GROUP
    qcol, kcol, vcol = NQ_CHUNK, NQ_CHUNK + 1, NQ_CHUNK + 2
    prev = lambda i: jnp.maximum(i - 1, 0)
    nxt = lambda i: jnp.minimum(i + 1, ng - 1)
    spec = lambda f, col: pl.BlockSpec((1, NA_BLK, NA_W), lambda s, i: (s, f(i), col))
    same = lambda i: i
    return pl.pallas_call(
        functools.partial(_na_kernel, rows=rows),
        grid=(b, ng),
        in_specs=[
            spec(same, qcol),
            spec(prev, kcol), spec(same, kcol), spec(nxt, kcol),
            spec(prev, vcol), spec(same, vcol), spec(nxt, vcol),
            _resident(bias),
        ],
        out_specs=pl.BlockSpec((1, NA_BLK, NA_W), lambda s, i: (s, i, 0)),
        out_shape=jax.ShapeDtypeStruct((b, l, NA_W), BF16),
        scratch_shapes=[pltpu.VMEM((3 * NA_BLK, NA_W), BF16), pltpu.VMEM((3 * NA_BLK, NA_W), BF16)],
        compiler_params=_params(("parallel", "parallel")),
        name="natten",
    )(proj3, proj3, proj3, proj3, proj3, proj3, proj3, bias)


def _postmix_kernel(*refs, with_router):
    if with_router:
        (of_ref, ob_ref, r_ref, ga_ref, gb_ref, na_ref, x_ref, gog_ref, wog_ref, won_ref, wout_ref,
         gf_ref, rh_ref, rl_ref, before_ref, x1_ref, h2_ref, route_ref, count_ref) = refs
    else:
        (of_ref, ob_ref, r_ref, ga_ref, gb_ref, na_ref, x_ref, gog_ref, wog_ref, won_ref, wout_ref,
         gf_ref, x1_ref, h2_ref) = refs
    o = of_ref[...].astype(F32) + ob_ref[...].astype(F32)
    parts = []
    for h in range(GLA_HEADS):
        seg = o[:, h * GLA_DV:(h + 1) * GLA_DV]
        ms = jnp.mean(seg * seg, axis=-1, keepdims=True)
        parts.append(seg * lax.rsqrt(ms + EPS))
    r = r_ref[...].astype(F32)
    on = jnp.concatenate(parts, axis=-1) * gog_ref[...] * (r * _sigmoid(r))
    ya = _dot(on.astype(BF16), wog_ref[...])
    yb = _dot(na_ref[...], won_ref[...])
    merged = _sigmoid(ga_ref[...].astype(F32)) * ya + _sigmoid(gb_ref[...].astype(F32)) * yb
    x1 = x_ref[...] + _dot(merged.astype(BF16), wout_ref[...])
    x1_ref[...] = x1
    h2 = _rmsnorm(x1, gf_ref[...])
    h2_ref[...] = h2.astype(h2_ref.dtype)
    if with_router:
        hh, hl = _split_bf16(h2)
        lg2 = _dot_nt(rh_ref[...], hh) + _dot_nt(rl_ref[...], hl)
        lg = lg2[:N_EXPERTS] + lg2[N_EXPERTS:]
        row = lax.broadcasted_iota(jnp.int32, lg.shape, 0).astype(F32)
        m1 = jnp.max(lg, axis=0, keepdims=True)
        i1 = jnp.min(jnp.where(lg == m1, row, float(N_EXPERTS)), axis=0, keepdims=True)
        lg_rest = jnp.where(row == i1, -jnp.inf, lg)
        m2 = jnp.max(lg_rest, axis=0, keepdims=True)
        i2 = jnp.min(jnp.where(lg_rest == m2, row, float(N_EXPERTS)), axis=0, keepdims=True)
        t = jnp.exp(m2 - m1)
        w1 = 1.0 / (1.0 + t)
        @pl.when(pl.program_id(0) == 0)
        def _():
            count_ref[...] = jnp.zeros_like(count_ref)

        picked = (row == i1) | (row == i2)
        sel = jnp.where(picked, 1.0, 0.0)
        sel16 = jnp.concatenate([sel, jnp.zeros_like(sel)], axis=0).astype(BF16)
        rank = _dot(sel16, before_ref[...])[:N_EXPERTS] + count_ref[:, :1]
        count_ref[...] = count_ref[...] + jnp.sum(sel, axis=1, keepdims=True)
        pick = lambda i, a: jnp.sum(jnp.where(row == i, a, 0.0), axis=0, keepdims=True)
        route_ref[...] = jnp.concatenate(
            [i1, i2, pick(i1, rank), pick(i2, rank), w1, t * w1, jnp.zeros_like(w1), jnp.zeros_like(w1)],
            axis=0)


def _postmix(of, ob, proj, na, x, gog, wog, won, wout, gf, router, tm):
    n, d = x.shape
    with_router = router is not None
    in_specs = [
        pl.BlockSpec((tm, V_W), lambda i: (i, 0)),
        pl.BlockSpec((tm, V_W), lambda i: (i, 0)),
        pl.BlockSpec((tm, 1024), lambda i: (i, 2)),
        pl.BlockSpec((tm, 1024), lambda i: (i, 3)),
        pl.BlockSpec((tm, 1024), lambda i: (i, 4)),
        pl.BlockSpec((tm, NA_W), lambda i: (i, 0)),
        pl.BlockSpec((tm, d), lambda i: (i, 0)),
        _resident(gog), _resident(wog), _resident(won), _resident(wout), _resident(gf),
    ]
    args = [of, ob, proj, proj, proj, na, x, gog, wog, won, wout, gf]
    out_specs = [pl.BlockSpec((tm, d), lambda i: (i, 0)), pl.BlockSpec((tm, d), lambda i: (i, 0))]
    out_shape = [jax.ShapeDtypeStruct((n, d), F32), jax.ShapeDtypeStruct((n, d), F32 if with_router else BF16)]
    if with_router:
        before = jnp.asarray(np.triu(np.ones((tm, tm), np.float32), 1), BF16)
        in_specs += [_resident(router[0]), _resident(router[1]), _resident(before)]
        args += list(router) + [before]
        out_specs.append(pl.BlockSpec((N_EXPERTS, tm), lambda i: (0, i)))
        out_shape.append(jax.ShapeDtypeStruct((N_EXPERTS, n), F32))
        out_specs.append(pl.BlockSpec((N_EXPERTS, 128), lambda i: (0, 0)))
        out_shape.append(jax.ShapeDtypeStruct((N_EXPERTS, 128), F32))
    return pl.pallas_call(
        functools.partial(_postmix_kernel, with_router=with_router),
        grid=(n // tm,),
        in_specs=in_specs,
        out_specs=out_specs,
        out_shape=out_shape,
        compiler_params=_params(("arbitrary" if with_router else "parallel",)),
        name="postmix_router" if with_router else "postmix",
    )(*args)


def _ffn_kernel(x1_ref, h_ref, wg_ref, wu_ref, wd_ref, o_ref, acc_ref):
    j = pl.program_id(1)

    @pl.when(j == 0)
    def _():
        acc_ref[...] = jnp.zeros_like(acc_ref)

    h = h_ref[...]
    g = _dot(h, wg_ref[...])
    u = _dot(h, wu_ref[...])
    a = (g * _sigmoid(g) * u).astype(BF16)
    acc_ref[...] += _dot(a, wd_ref[...])

    @pl.when(j == pl.num_programs(1) - 1)
    def _():
        o_ref[...] = x1_ref[...] + acc_ref[...]


def _ffn(x1, h2, wg, wu, wd, tm, tf):
    n, d = x1.shape
    dff = wg.shape[1]
    return pl.pallas_call(
        _ffn_kernel,
        grid=(n // tm, dff // tf),
        in_specs=[
            pl.BlockSpec((tm, d), lambda i, j: (i, 0)),
            pl.BlockSpec((tm, d), lambda i, j: (i, 0)),
            pl.BlockSpec((d, tf), lambda i, j: (0, j)),
            pl.BlockSpec((d, tf), lambda i, j: (0, j)),
            pl.BlockSpec((tf, d), lambda i, j: (j, 0)),
        ],
        out_specs=pl.BlockSpec((tm, d), lambda i, j: (i, 0)),
        out_shape=jax.ShapeDtypeStruct((n, d), F32),
        scratch_shapes=[pltpu.VMEM((tm, d), F32)],
        compiler_params=_params(("parallel", "arbitrary")),
        name="ffn",
    )(x1, h2, wg, wu, wd)


SC_CORES = 2
SC_SUBCORES = 16
SC_WORKERS = SC_CORES * SC_SUBCORES
SC_CHUNK = 32
MOE_TM = 512


def _sc_gather_rows(table, idx):
    nrow, d = idx.shape[0], table.shape[1]
    assert nrow % (SC_WORKERS * SC_CHUNK) == 0
    per_worker = nrow // SC_WORKERS
    mesh = plsc.VectorSubcoreMesh(core_axis_name="c", subcore_axis_name="s",
                                  num_cores=SC_CORES, num_subcores=SC_SUBCORES)

    @functools.partial(
        pl.kernel, mesh=mesh, out_type=jax.ShapeDtypeStruct((nrow, d), table.dtype),
        scratch_types=[pltpu.VMEM((SC_CHUNK,), jnp.int32), pltpu.VMEM((SC_CHUNK, d), table.dtype),
                       pltpu.SemaphoreType.DMA])
    def gather(table_hbm, idx_hbm, out_hbm, idx_v, rows_v, sem):
        base = (lax.axis_index("s") * SC_CORES + lax.axis_index("c")) * per_worker

        @pl.loop(0, per_worker // SC_CHUNK)
        def _(j):
            off = base + j * SC_CHUNK
            pltpu.sync_copy(idx_hbm.at[pl.ds(off, SC_CHUNK)], idx_v)
            pltpu.async_copy(table_hbm.at[idx_v], rows_v, sem).wait()
            pltpu.sync_copy(rows_v, out_hbm.at[pl.ds(off, SC_CHUNK)])

    return gather(table, idx)


def _sc_scatter_rows(rows, pos12, n_out):
    n, d = rows.shape
    assert n % (SC_WORKERS * SC_CHUNK) == 0
    per_worker = n // SC_WORKERS
    idx = pos12.reshape(2, n)
    mesh = plsc.VectorSubcoreMesh(core_axis_name="c", subcore_axis_name="s",
                                  num_cores=SC_CORES, num_subcores=SC_SUBCORES)

    @functools.partial(
        pl.kernel, mesh=mesh, out_type=jax.ShapeDtypeStruct((n_out, d), rows.dtype),
        scratch_types=[pltpu.VMEM((2, SC_CHUNK), jnp.int32), pltpu.VMEM((SC_CHUNK, d), rows.dtype)])
    def scatter(rows_hbm, idx_hbm, out_hbm, idx_v, rows_v):
        base = (lax.axis_index("s") * SC_CORES + lax.axis_index("c")) * per_worker

        @pl.loop(0, per_worker // SC_CHUNK)
        def _(j):
            off = base + j * SC_CHUNK
            pltpu.sync_copy(rows_hbm.at[pl.ds(off, SC_CHUNK)], rows_v)
            for k in range(2):
                pltpu.sync_copy(idx_hbm.at[k, pl.ds(off, SC_CHUNK)], idx_v.at[k])
                pltpu.sync_copy(rows_v, out_hbm.at[idx_v.at[k]])

    return scatter(rows, idx)


def _moe_plan(route, counts):
    n = route.shape[1]
    count = counts[:, 0].astype(jnp.int32)
    group = (count + MOE_TM - 1) // MOE_TM * MOE_TM
    ends = jnp.cumsum(group)
    start = ends - group
    experts = jnp.arange(N_EXPERTS, dtype=jnp.int32)[:, None]

    def rows_of(e, rank):
        return jnp.sum(jnp.where(e.astype(jnp.int32)[None, :] == experts, start[:, None], 0), axis=0) \
            + rank.astype(jnp.int32)

    pos12 = jnp.concatenate([rows_of(route[0], route[2]), rows_of(route[1], route[3])])
    n_rows = 2 * n + N_EXPERTS * MOE_TM
    tile_start = jnp.arange(n_rows // MOE_TM, dtype=jnp.int32) * MOE_TM
    tile_expert = jnp.minimum(jnp.sum(tile_start[:, None] >= ends[None, :], axis=1), N_EXPERTS - 1)
    filled = (start + count)[tile_expert] - tile_start
    tile_rows = jnp.where(tile_start < ends[-1], jnp.clip(filled, 0, MOE_TM), 0).astype(jnp.int32)
    w12 = jnp.stack([route[4], route[5]], axis=1)
    return pos12, n_rows, tile_expert.astype(jnp.int32), tile_rows, w12


def _moe_group_kernel(te_ref, tr_ref, xs_ref, wg_ref, wu_ref, wd_ref, ys_ref):
    n_filled = tr_ref[pl.program_id(0)]
    used = n_filled > 0

    @pl.when(used)
    def _():
        row = lax.broadcasted_iota(jnp.int32, xs_ref.shape, 0)
        x = jnp.where(row < n_filled, xs_ref[...], 0.0).astype(BF16)
        g = _dot(x, wg_ref[0])
        u = _dot(x, wu_ref[0])
        ys_ref[...] = _dot((g * _sigmoid(g) * u).astype(BF16), wd_ref[0])

    @pl.when(jnp.logical_not(used))
    def _():
        ys_ref[...] = jnp.zeros_like(ys_ref)


def _moe_grouped(xs, tile_expert, tile_rows, wg, wu, wd):
    p, d = xs.shape
    _, _, dfe = wg.shape
    grid_spec = pltpu.PrefetchScalarGridSpec(
        num_scalar_prefetch=2,
        grid=(p // MOE_TM,),
        in_specs=[
            pl.BlockSpec((MOE_TM, d), lambda t, te, tr: (t, 0)),
            pl.BlockSpec((1, d, dfe), lambda t, te, tr: (te[t], 0, 0)),
            pl.BlockSpec((1, d, dfe), lambda t, te, tr: (te[t], 0, 0)),
            pl.BlockSpec((1, dfe, d), lambda t, te, tr: (te[t], 0, 0)),
        ],
        out_specs=pl.BlockSpec((MOE_TM, d), lambda t, te, tr: (t, 0)),
    )
    return pl.pallas_call(
        _moe_group_kernel,
        grid_spec=grid_spec,
        out_shape=jax.ShapeDtypeStruct((p, d), F32),
        compiler_params=_params(("arbitrary",)),
        name="moe_grouped",
    )(tile_expert, tile_rows, xs, wg, wu, wd)


def _moe_combine_kernel(x1_ref, y1_ref, y2_ref, w_ref, o_ref):
    w = w_ref[...]
    o_ref[...] = x1_ref[...] + w[:, 0:1] * y1_ref[...] + w[:, 1:2] * y2_ref[...]


def _moe_combine(x1, y, w12, tm):
    n, d = x1.shape
    nt = n // tm
    return pl.pallas_call(
        _moe_combine_kernel,
        grid=(nt,),
        in_specs=[
            pl.BlockSpec((tm, d), lambda i: (i, 0)),
            pl.BlockSpec((tm, d), lambda i: (i, 0)),
            pl.BlockSpec((tm, d), lambda i: (i + nt, 0)),
            pl.BlockSpec((tm, 2), lambda i: (i, 0)),
        ],
        out_specs=pl.BlockSpec((tm, d), lambda i: (i, 0)),
        out_shape=jax.ShapeDtypeStruct((n, d), F32),
        compiler_params=_params(("parallel",)),
        name="moe_combine",
    )(x1, y, y, w12)


def _moe(x1, h2, route, counts, wg, wu, wd):
    pos12, n_rows, tile_expert, tile_rows, w12 = _moe_plan(route, counts)
    xs = _sc_scatter_rows(h2, pos12, n_rows)
    ys = _moe_grouped(xs, tile_expert, tile_rows, wg, wu, wd)
    y = _sc_gather_rows(ys, pos12)
    return _moe_combine(x1, y, w12, _row_tile(x1.shape[0], 1024))


def _row_tile(n, want):
    t = min(n, want)
    assert n % t == 0
    return t


def _prep_layer(l, norm_mix, w_in, w_decay_f, b_decay_f, w_decay_b, b_decay_b, gla_out_norm, q_norm,
                k_norm, rpb, w_o_gla, w_o_na, w_out, norm_ffn):
    w = w_in[l]
    c0 = 2 * QK_W + 2 * V_W
    c1 = c0 + 2 * GLA_RANK
    c2 = c1 + 3 * NA_W
    w_main = jnp.concatenate([w[:, :QK_W] * (GLA_DK ** -0.5), w[:, QK_W:c0], w[:, c2:], w[:, c1:c2]],
                             axis=1).astype(BF16)
    wz = w[:, c0:c1].astype(BF16)
    zero = jnp.zeros((GLA_RANK, QK_W), F32)
    wdec = jnp.concatenate([jnp.concatenate([w_decay_f[l], zero], axis=1),
                            jnp.concatenate([zero, w_decay_b[l]], axis=1)], axis=0)
    bdec = jnp.concatenate([b_decay_f[l], b_decay_b[l]])[None, :]
    qg = jnp.tile(q_norm[l] * (NA_DH ** -0.5), NA_HEADS)[None, :]
    kg = jnp.tile(k_norm[l], NA_HEADS)[None, :]
    return dict(
        g_mix=norm_mix[l][None, :], w_main=w_main, wz=wz, wd=wdec.astype(BF16), bdec=bdec,
        qg=qg, kg=kg, bias=_na_bias_table(rpb[l]),
        gog=gla_out_norm[l].reshape(1, V_W), wog=w_o_gla[l].astype(BF16),
        won=w_o_na[l].astype(BF16), wout=w_out[l].astype(BF16), g_ffn=norm_ffn[l][None, :])


def _router_operands(router):
    hi, lo = _split_bf16(router.T)
    return jnp.concatenate([hi, lo], axis=0), jnp.concatenate([hi, jnp.zeros_like(hi)], axis=0)


def _trunk(x, layers, dense, moe):
    b, l, d = x.shape
    n = b * l
    xf = x.reshape(n, d)
    head_mean = jnp.asarray(np.kron(np.eye(NA_HEADS), np.full((NA_DH, NA_DH), 1.0 / NA_DH)), BF16)
    tb = _row_tile(l, 512)
    sb = 2 if b % 2 == 0 else 1
    for li, p in enumerate(layers):
        proj, la, lamin = _inproj(xf, p["g_mix"], p["w_main"], p["wz"], p["wd"], p["bdec"],
                           head_mean, p["qg"], p["kg"], _row_tile(n, 512))
        proj3 = proj.reshape(b, l, PROJ_W)
        la3 = la.reshape(b, l, 2 * QK_W)
        of = _gla(proj3, la3, lamin, False, tb, sb).reshape(n, V_W)
        ob = _gla(proj3, la3, lamin, True, tb, sb).reshape(n, V_W)
        na = _natten(proj3, p["bias"]).reshape(n, NA_W)
        if li % 2 == 0:
            wg, wu, wd = dense[li // 2]
            x1, h2 = _postmix(of, ob, proj, na, xf, p["gog"], p["wog"], p["won"], p["wout"],
                              p["g_ffn"], None, _row_tile(n, 512))
            xf = _ffn(x1, h2, wg, wu, wd, _row_tile(n, 1024), wg.shape[1] // 2)
        else:
            router, wg, wu, wd = moe[li // 2]
            x1, h2, route, counts = _postmix(of, ob, proj, na, xf, p["gog"], p["wog"], p["won"], p["wout"],
                                             p["g_ffn"], router, _row_tile(n, 512))
            xf = _moe(x1, h2, route, counts, wg, wu, wd)
    return xf.reshape(b, l, d)


def kernel(x_prompt, x_sample, norm_mix, w_in, w_decay_f, b_decay_f, w_decay_b, b_decay_b, gla_out_norm,
           q_norm, k_norm, rpb, w_o_gla, w_o_na, w_out, norm_ffn, ffn_w_gate, ffn_w_up, ffn_w_down,
           moe_router, moe_w_gate, moe_w_up, moe_w_down):
    depth = w_in.shape[0]
    layers = [_prep_layer(l, norm_mix, w_in, w_decay_f, b_decay_f, w_decay_b, b_decay_b, gla_out_norm,
                          q_norm, k_norm, rpb, w_o_gla, w_o_na, w_out, norm_ffn) for l in range(depth)]
    dense = [(ffn_w_gate[j].astype(BF16), ffn_w_up[j].astype(BF16), ffn_w_down[j].astype(BF16))
             for j in range(ffn_w_gate.shape[0])]
    moe = [(_router_operands(moe_router[j]), moe_w_gate[j].astype(BF16), moe_w_up[j].astype(BF16),
            moe_w_down[j].astype(BF16)) for j in range(moe_router.shape[0])]
    return (_trunk(x_prompt, layers, dense, moe), _trunk(x_sample, layers, dense, moe))
```

```python
import functools

import numpy as np
import jax
import jax.numpy as jnp
from jax import lax
from jax.experimental import pallas as pl
from jax.experimental.pallas import tpu as pltpu
from jax.experimental.pallas import tpu_sc as plsc

F32 = jnp.float32
BF16 = jnp.bfloat16

EPS = 1e-6
GRID_W = 64
GLA_HEADS = 4
GLA_DK = 128
GLA_DV = 256
GLA_CHUNK = 64
GLA_RANK = 16
GLA_TAU = 16.0
NA_HEADS = 8
NA_DH = 64
NA_KR = 8
NA_KC = 16
N_EXPERTS = 8
NEG = -1e30
LOG2_E = 1.4426950408889634

VMEM_LIMIT = 56 * 1024 * 1024

QK_W = GLA_HEADS * GLA_DK
V_W = GLA_HEADS * GLA_DV
NA_W = NA_HEADS * NA_DH
PROJ_W = 2 * QK_W + 2 * V_W + 2 * 1024 + 3 * NA_W
CW = 512
NQ_CHUNK = (2 * QK_W + 2 * V_W + 2048) // CW
NK_CHUNK = NQ_CHUNK + 1
INPROJ_SUB = 256


def _params(sem):
    return pltpu.CompilerParams(dimension_semantics=sem, vmem_limit_bytes=VMEM_LIMIT)


def _resident(a):
    nd = a.ndim
    return pl.BlockSpec(a.shape, lambda *_: (0,) * nd, pipeline_mode=pl.Buffered(1))


def _split_bf16(a):
    hi = a.astype(BF16)
    lo = (a - hi.astype(F32)).astype(BF16)
    return hi, lo


def _dot(a, b):
    return jnp.dot(a, b, preferred_element_type=F32)


def _dot_nt(a, b):
    return lax.dot_general(a, b, (((1,), (1,)), ((), ())), preferred_element_type=F32)


def _dot_tn(a, b):
    return lax.dot_general(a, b, (((0,), (0,)), ((), ())), preferred_element_type=F32)


def _sigmoid(x):
    return 1.0 / (1.0 + jnp.exp2(x * (-LOG2_E)))


def _pack_bf16_pairs(x):
    c = x.shape[1] // 2
    as_bits = lambda a: lax.bitcast_convert_type(a.astype(BF16).astype(F32), jnp.uint32)
    word = (as_bits(x[:, :c]) >> 16) | (as_bits(x[:, c:]) & jnp.uint32(0xFFFF0000))
    return lax.bitcast_convert_type(word, jnp.int32)


def _unpack_bf16_pairs(w):
    u = lax.bitcast_convert_type(w, jnp.uint32)
    lo = lax.bitcast_convert_type(u << 16, F32)
    hi = lax.bitcast_convert_type(u & jnp.uint32(0xFFFF0000), F32)
    return jnp.concatenate([lo, hi], axis=1)


def _rmsnorm(x, g):
    ms = jnp.mean(x * x, axis=-1, keepdims=True)
    return x * lax.rsqrt(ms + EPS) * g


def _inproj_kernel(x_ref, g_ref, w_ref, wz_ref, wd_ref, bdec_ref, hm_ref, qg_ref, kg_ref,
                   proj_ref, la_ref, lamin_ref):
    for t in range(x_ref.shape[0] // INPROJ_SUB):
        rs = slice(t * INPROJ_SUB, (t + 1) * INPROJ_SUB)
        h = _rmsnorm(x_ref[rs, :], g_ref[...]).astype(BF16)
        z = _dot(h, wz_ref[...])
        y = _dot(z.astype(BF16), wd_ref[...]) + bdec_ref[...]
        ls = jnp.minimum(y, 0.0) - jnp.log(1.0 + jnp.exp(-jnp.abs(y)))
        la = ls * (1.0 / GLA_TAU)
        la_ref[rs, :] = la
        lamin_ref[t] = jnp.broadcast_to(jnp.min(la, axis=0, keepdims=True), lamin_ref.shape[1:])
        for c in range(PROJ_W // CW):
            acc = _dot(h, w_ref[:, c * CW:(c + 1) * CW])
            if c in (NQ_CHUNK, NK_CHUNK):
                gain = qg_ref if c == NQ_CHUNK else kg_ref
                ms = _dot((acc * acc).astype(BF16), hm_ref[...])
                acc = acc * lax.rsqrt(ms + EPS) * gain[...]
            proj_ref[rs, c * CW:(c + 1) * CW] = acc.astype(BF16)


def _inproj(x, g, w_main, wz, wd, bdec, hm, qg, kg, tm):
    n, d = x.shape
    return pl.pallas_call(
        _inproj_kernel,
        grid=(n // tm,),
        in_specs=[
            pl.BlockSpec((tm, d), lambda i: (i, 0)),
            _resident(g), _resident(w_main), _resident(wz), _resident(wd),
            _resident(bdec), _resident(hm), _resident(qg), _resident(kg),
        ],
        out_specs=[
            pl.BlockSpec((tm, PROJ_W), lambda i: (i, 0)),
            pl.BlockSpec((tm, 2 * QK_W), lambda i: (i, 0)),
            pl.BlockSpec((tm // INPROJ_SUB, 8, 2 * QK_W), lambda i: (i, 0, 0)),
        ],
        out_shape=[
            jax.ShapeDtypeStruct((n, PROJ_W), BF16),
            jax.ShapeDtypeStruct((n, 2 * QK_W), F32),
            jax.ShapeDtypeStruct((n // INPROJ_SUB, 8, 2 * QK_W), F32),
        ],
        compiler_params=_params(("parallel",)),
        name="inproj",
    )(x, g, w_main, wz, wd, bdec, hm, qg, kg)


N_LEVELS = 6
INTER_BLK = N_LEVELS
STATE_BLK = N_LEVELS + 1


def _decay_sum_matrix(reverse):
    c = GLA_CHUNK
    m_all = np.zeros((N_LEVELS + 2, c, c), np.float32)
    for l in range(N_LEVELS):
        m = 1 << l
        for p in range(c):
            mid = (p // (2 * m)) * 2 * m + m
            if p >= mid:
                m_all[l, p, mid + 1:p + 1] = 1.0
            else:
                m_all[l, p, p + 1:mid + 1] = 1.0
    for p in range(c):
        m_all[INTER_BLK, p, :p + 1] = 1.0
        m_all[STATE_BLK, p, p + 1:] = 1.0
    if reverse:
        m_all = m_all[:, ::-1, ::-1]
    return np.ascontiguousarray(m_all).reshape((N_LEVELS + 2) * c, c)


def _level_matrix(reverse):
    i = np.arange(GLA_CHUNK)[:, None]
    j = np.arange(GLA_CHUNK)[None, :]
    x = i ^ j
    lvl = np.where(x > 0, np.floor(np.log2(np.maximum(x, 1))), N_LEVELS).astype(np.int32)
    valid = (i <= j) if reverse else (i >= j)
    return np.where(valid, lvl, N_LEVELS + 1).astype(np.int32)


GLA_FAST_CHUNK = 128
GLA_FAST_MAX_DECAY = 60.0


def _tri_matrix(reverse, c):
    i = np.arange(c)[:, None]
    j = np.arange(c)[None, :]
    return ((j >= i) if reverse else (j <= i)).astype(np.float32)


def _decay_columns(dec_row):
    t = jnp.broadcast_to(dec_row, (GLA_DK, GLA_DK)).T
    return jnp.concatenate([t, t], axis=1)


def _gla_kernel(q_ref, k_ref, v_ref, la_ref, lamin_ref, m_ref, lvl_ref, tri_ref, o_ref,
                s_ref, lr_ref, qh_ref, kh_ref, ks_ref, dec_ref, *, reverse, tb, sb):
    @pl.when(pl.program_id(1) == 0)
    def _():
        s_ref[...] = jnp.zeros_like(s_ref)

    fc = GLA_FAST_CHUNK
    n_fast = tb // fc
    total_row = 0 if reverse else fc - 1
    seqs = range(sb)

    def fast_rows(ci):
        return pl.ds(pl.multiple_of(ci * fc, fc), fc)

    def sc_lanes(h):
        return slice(h * 2 * GLA_DK, h * 2 * GLA_DK + GLA_DK)

    def qh_lanes(h):
        return slice(h * 2 * GLA_DK + GLA_DK, (h + 1) * 2 * GLA_DK)

    def fast_factors(ci, carry):
        rows = fast_rows(ci)
        for s in seqs:
            b = _dot(tri_ref[...], la_ref[s, rows, :].astype(BF16))
            dec = jnp.exp(b[total_row:total_row + 1, :])
            f_q = jnp.exp(b)
            qh = (q_ref[s, rows, :].astype(F32) * f_q).astype(BF16)
            kh = k_ref[s, rows, :].astype(F32) / f_q
            qh_ref[s, rows, :] = qh
            for h in range(GLA_HEADS):
                lr_ref[s, rows, qh_lanes(h)] = qh[:, h * GLA_DK:(h + 1) * GLA_DK]
            kh_ref[s, rows, :] = kh.astype(BF16)
            ks_ref[s, rows, :] = (kh * dec).astype(BF16)
            dec_ref[s, pl.ds(pl.multiple_of(ci * 8, 8), 8), :] = jnp.broadcast_to(dec, (8, QK_W))
        return carry

    def fast_scores(ci, carry):
        rows = fast_rows(ci)
        i = lax.broadcasted_iota(jnp.int32, (fc, fc), 0)
        j = lax.broadcasted_iota(jnp.int32, (fc, fc), 1)
        causal = (i <= j) if reverse else (i >= j)
        for s in seqs:
            for h in range(GLA_HEADS):
                hs = slice(h * GLA_DK, (h + 1) * GLA_DK)
                sc = _dot_nt(qh_ref[s, rows, hs], kh_ref[s, rows, hs])
                lr_ref[s, rows, sc_lanes(h)] = jnp.where(causal, sc, 0.0).astype(BF16)
        return carry

    def fast_state(ci, carry):
        cc = (n_fast - 1 - ci) if reverse else ci
        rows = fast_rows(cc)
        heads = [(s, h) for s in seqs for h in range(GLA_HEADS)]
        vals = {(s, h): v_ref[s, rows, h * GLA_DV:(h + 1) * GLA_DV] for s, h in heads}
        kv = {(s, h): _dot_tn(ks_ref[s, rows, h * GLA_DK:(h + 1) * GLA_DK], vals[s, h]) for s, h in heads}
        for s, h in heads:
            dec = dec_ref[s, pl.ds(pl.multiple_of(cc * 8, 8), 8), :][:1, h * GLA_DK:(h + 1) * GLA_DK]
            st = s_ref[s, h]
            o = _dot(lr_ref[s, rows, h * 2 * GLA_DK:(h + 1) * 2 * GLA_DK],
                     jnp.concatenate([vals[s, h], st.astype(BF16)], axis=0))
            o_ref[s, rows, h * GLA_DV:(h + 1) * GLA_DV] = o.astype(o_ref.dtype)
            s_ref[s, h] = st * _decay_columns(dec) + kv[s, h]
        return carry

    def robust_chunk(ci, carry):
        c = GLA_CHUNK
        n_chunks = tb // c
        lvl = lvl_ref[...]
        row = lax.broadcasted_iota(jnp.int32, (c, GLA_DK), 0)
        total_row = 0 if reverse else c - 1
        cc = (n_chunks - 1 - ci) if reverse else ci
        rows = pl.ds(pl.multiple_of(cc * c, c), c)
        for s in seqs:
            la_hi, la_lo = _split_bf16(la_ref[s, rows, :])
            fac = jnp.exp(_dot(m_ref[...], la_hi) + _dot(m_ref[...], la_lo))
            for h in range(GLA_HEADS):
                hs = slice(h * GLA_DK, (h + 1) * GLA_DK)
                vs = slice(h * GLA_DV, (h + 1) * GLA_DV)
                q = q_ref[s, rows, hs].astype(F32)
                k = k_ref[s, rows, hs].astype(F32)
                v = v_ref[s, rows, vs]
                scores = jnp.zeros((c, c), F32)
                for l in range(N_LEVELS):
                    is_query = ((row >> l) & 1) == (0 if reverse else 1)
                    f_l = fac[l * c:(l + 1) * c, hs]
                    g_l = (jnp.where(is_query, q, k) * f_l).astype(BF16)
                    scores = jnp.where(lvl == l, _dot_nt(g_l, g_l), scores)
                scores = jnp.where(lvl == N_LEVELS, _dot_nt(q.astype(BF16), k.astype(BF16)), scores)
                f_in = fac[INTER_BLK * c:(INTER_BLK + 1) * c, hs]
                f_st = fac[STATE_BLK * c:(STATE_BLK + 1) * c, hs]
                st = s_ref[s, h]
                inter = _dot((q * f_in).astype(BF16), st.astype(BF16))
                intra = _dot(scores.astype(BF16), v)
                o_ref[s, rows, vs] = (inter + intra).astype(o_ref.dtype)
                dec = f_in[total_row:total_row + 1, :]
                s_ref[s, h] = st * _decay_columns(dec) + _dot_tn((k * f_st).astype(BF16), v)
        return carry

    bounded = jnp.min(lamin_ref[...]) >= -GLA_FAST_MAX_DECAY / GLA_FAST_CHUNK

    @pl.when(bounded)
    def _():
        lax.fori_loop(0, n_fast, fast_factors, 0, unroll=4)
        lax.fori_loop(0, n_fast, fast_scores, 0, unroll=4)
        lax.fori_loop(0, n_fast, fast_state, 0, unroll=4)

    @pl.when(jnp.logical_not(bounded))
    def _():
        lax.fori_loop(0, tb // GLA_CHUNK, robust_chunk, 0)


def _gla(proj3, la3, lamin, reverse, tb, sb):
    b, l, _ = proj3.shape
    nb = l // tb
    lamin4 = lamin.reshape(b, -1, 8, 2 * QK_W)
    tiles = lamin4.shape[1] // nb
    blk = (lambda i: nb - 1 - i) if reverse else (lambda i: i)
    m_all = jnp.asarray(_decay_sum_matrix(reverse), BF16)
    lvl = jnp.asarray(_level_matrix(reverse))
    tri = jnp.asarray(_tri_matrix(reverse, GLA_FAST_CHUNK), BF16)
    kern = functools.partial(_gla_kernel, reverse=reverse, tb=tb, sb=sb)
    decay_col = 1 if reverse else 0
    return pl.pallas_call(
        kern,
        grid=(b // sb, nb),
        in_specs=[
            pl.BlockSpec((sb, tb, QK_W), lambda s, i: (s, blk(i), 0)),
            pl.BlockSpec((sb, tb, QK_W), lambda s, i: (s, blk(i), 1)),
            pl.BlockSpec((sb, tb, V_W), lambda s, i: (s, blk(i), 1)),
            pl.BlockSpec((sb, tb, QK_W), lambda s, i: (s, blk(i), decay_col)),
            pl.BlockSpec((sb, tiles, 8, QK_W), lambda s, i: (s, blk(i), 0, decay_col)),
            _resident(m_all), _resident(lvl), _resident(tri),
        ],
        out_specs=pl.BlockSpec((sb, tb, V_W), lambda s, i: (s, blk(i), 0)),
        out_shape=jax.ShapeDtypeStruct((b, l, V_W), BF16),
        scratch_shapes=[
            pltpu.VMEM((sb, GLA_HEADS, GLA_DK, GLA_DV), F32),
            pltpu.VMEM((sb, tb, 2 * QK_W), BF16),
            pltpu.VMEM((sb, tb, QK_W), BF16),
            pltpu.VMEM((sb, tb, QK_W), BF16),
            pltpu.VMEM((sb, tb, QK_W), BF16),
            pltpu.VMEM((sb, tb // GLA_FAST_CHUNK * 8, QK_W), F32),
        ],
        compiler_params=_params(("parallel", "arbitrary")),
        name="gla_bwd" if reverse else "gla_fwd",
    )(proj3, proj3, proj3, la3, lamin4, m_all, lvl, tri)


NA_GROUP = 8
NA_BLK = NA_GROUP * GRID_W
NA_BAND = NA_KR * GRID_W


def _na_bias_table(rpb):
    qc = np.arange(GRID_W)[:, None]
    kc = np.arange(GRID_W)[None, :]
    col_start = np.clip(qc - NA_KC // 2, 0, GRID_W - NA_KC)
    valid = (kc >= col_start) & (kc < col_start + NA_KC)
    dc = np.clip(kc - qc, -(NA_KC - 1), NA_KC - 1) + NA_KC - 1
    onehot = ((dc[None] == np.arange(2 * NA_KC - 1)[:, None, None]) & valid[None]).astype(np.float32)
    mask = np.where(valid, 0.0, NEG).astype(np.float32)
    rows = jnp.stack([rpb[:, NA_KR - 1 - d:2 * NA_KR - 1 - d, :] for d in range(NA_KR)])
    t = jnp.einsum('dhkc,cqj->dhqkj', rows, jnp.asarray(onehot), precision=lax.Precision.HIGHEST)
    t = t + jnp.asarray(mask)[None, None, :, None, :]
    return t.reshape(NA_KR, NA_HEADS // 2, 2 * GRID_W, NA_BAND)


def _na_kernel(q_ref, kp_ref, kc_ref, kn_ref, vp_ref, vc_ref, vn_ref, bias_ref, o_ref,
               kwin_ref, vwin_ref, *, rows):
    g = pl.program_id(1)
    for j, (kr, vr) in enumerate(((kp_ref, vp_ref), (kc_ref, vc_ref), (kn_ref, vn_ref))):
        kwin_ref[j * NA_BLK:(j + 1) * NA_BLK, :] = kr[0]
        vwin_ref[j * NA_BLK:(j + 1) * NA_BLK, :] = vr[0]
    lane = lax.broadcasted_iota(jnp.int32, (GRID_W, 2 * NA_DH), 1)
    first = lane < NA_DH

    def one_row(rl, carry):
        r = g * NA_GROUP + rl
        rs = jnp.clip(r - NA_KR // 2, 0, rows - NA_KR)
        d = r - rs
        off = pl.multiple_of((rs - (g - 1) * NA_GROUP) * GRID_W, GRID_W)
        qrows = pl.ds(pl.multiple_of(rl * GRID_W, GRID_W), GRID_W)
        pairs = [slice(p * 2 * NA_DH, (p + 1) * 2 * NA_DH) for p in range(NA_HEADS // 2)]
        scores = []
        for p, ps in enumerate(pairs):
            qp = q_ref[0, qrows, ps]
            zero = jnp.zeros_like(qp)
            qs = jnp.concatenate([jnp.where(first, qp, zero), jnp.where(first, zero, qp)], axis=0)
            kb = kwin_ref[pl.ds(off, NA_BAND), ps]
            scores.append(_dot_nt(qs, kb) + bias_ref[d, p])
        probs = []
        for s in scores:
            e = jnp.exp(s - jnp.max(s, axis=-1, keepdims=True))
            probs.append((e.astype(BF16), jnp.sum(e, axis=-1, keepdims=True)))
        for (e, den), ps in zip(probs, pairs):
            vb = vwin_ref[pl.ds(off, NA_BAND), ps]
            pv = _dot(e, vb) / den
            o_ref[0, qrows, ps] = jnp.where(first, pv[:GRID_W], pv[GRID_W:]).astype(o_ref.dtype)
        return carry

    lax.fori_loop(0, NA_GROUP, one_row, 0, unroll=8)


def _natten(proj3, bias):
    b, l, _ = proj3.shape
    rows = l // GRID_W
    ng = rows // NA_GROUP
    qcol, kcol, vcol = NQ_CHUNK, NQ_CHUNK + 1, NQ_CHUNK + 2
    prev = lambda i: jnp.maximum(i - 1, 0)
    nxt = lambda i: jnp.minimum(i + 1, ng - 1)
    spec = lambda f, col: pl.BlockSpec((1, NA_BLK, NA_W), lambda s, i: (s, f(i), col))
    same = lambda i: i
    return pl.pallas_call(
        functools.partial(_na_kernel, rows=rows),
        grid=(b, ng),
        in_specs=[
            spec(same, qcol),
            spec(prev, kcol), spec(same, kcol), spec(nxt, kcol),
            spec(prev, vcol), spec(same, vcol), spec(nxt, vcol),
            _resident(bias),
        ],
        out_specs=pl.BlockSpec((1, NA_BLK, NA_W), lambda s, i: (s, i, 0)),
        out_shape=jax.ShapeDtypeStruct((b, l, NA_W), BF16),
        scratch_shapes=[pltpu.VMEM((3 * NA_BLK, NA_W), BF16), pltpu.VMEM((3 * NA_BLK, NA_W), BF16)],
        compiler_params=_params(("parallel", "parallel")),
        name="natten",
    )(proj3, proj3, proj3, proj3, proj3, proj3, proj3, bias)


def _postmix_kernel(*refs, with_router):
    if with_router:
        (of_ref, ob_ref, r_ref, ga_ref, gb_ref, na_ref, x_ref, gog_ref, wog_ref, won_ref, wout_ref,
         gf_ref, rh_ref, rl_ref, before_ref, x1_ref, h2_ref, route_ref, count_ref) = refs
    else:
        (of_ref, ob_ref, r_ref, ga_ref, gb_ref, na_ref, x_ref, gog_ref, wog_ref, won_ref, wout_ref,
         gf_ref, x1_ref, h2_ref) = refs
    o = of_ref[...].astype(F32) + ob_ref[...].astype(F32)
    parts = []
    for h in range(GLA_HEADS):
        seg = o[:, h * GLA_DV:(h + 1) * GLA_DV]
        ms = jnp.mean(seg * seg, axis=-1, keepdims=True)
        parts.append(seg * lax.rsqrt(ms + EPS))
    r = r_ref[...].astype(F32)
    on = jnp.concatenate(parts, axis=-1) * gog_ref[...] * (r * _sigmoid(r))
    ya = _dot(on.astype(BF16), wog_ref[...])
    yb = _dot(na_ref[...], won_ref[...])
    merged = _sigmoid(ga_ref[...].astype(F32)) * ya + _sigmoid(gb_ref[...].astype(F32)) * yb
    x1 = x_ref[...] + _dot(merged.astype(BF16), wout_ref[...])
    x1_ref[...] = x1
    h2 = _rmsnorm(x1, gf_ref[...])
    h2_ref[...] = _pack_bf16_pairs(h2) if with_router else h2.astype(BF16)
    if with_router:
        hh, hl = _split_bf16(h2)
        lg2 = _dot_nt(rh_ref[...], hh) + _dot_nt(rl_ref[...], hl)
        lg = lg2[:N_EXPERTS] + lg2[N_EXPERTS:]
        row = lax.broadcasted_iota(jnp.int32, lg.shape, 0).astype(F32)
        m1 = jnp.max(lg, axis=0, keepdims=True)
        i1 = jnp.min(jnp.where(lg == m1, row, float(N_EXPERTS)), axis=0, keepdims=True)
        lg_rest = jnp.where(row == i1, -jnp.inf, lg)
        m2 = jnp.max(lg_rest, axis=0, keepdims=True)
        i2 = jnp.min(jnp.where(lg_rest == m2, row, float(N_EXPERTS)), axis=0, keepdims=True)
        t = jnp.exp(m2 - m1)
        w1 = 1.0 / (1.0 + t)
        @pl.when(pl.program_id(0) == 0)
        def _():
            count_ref[...] = jnp.zeros_like(count_ref)

        picked = (row == i1) | (row == i2)
        sel = jnp.where(picked, 1.0, 0.0)
        sel16 = jnp.concatenate([sel, jnp.zeros_like(sel)], axis=0).astype(BF16)
        rank = _dot(sel16, before_ref[...])[:N_EXPERTS] + count_ref[:, :1]
        count_ref[...] = count_ref[...] + jnp.sum(sel, axis=1, keepdims=True)
        pick = lambda i, a: jnp.sum(jnp.where(row == i, a, 0.0), axis=0, keepdims=True)
        route_ref[...] = jnp.concatenate(
            [i1, i2, pick(i1, rank), pick(i2, rank), w1, t * w1, jnp.zeros_like(w1), jnp.zeros_like(w1)],
            axis=0)


def _postmix(of, ob, proj, na, x, gog, wog, won, wout, gf, router, tm):
    n, d = x.shape
    with_router = router is not None
    in_specs = [
        pl.BlockSpec((tm, V_W), lambda i: (i, 0)),
        pl.BlockSpec((tm, V_W), lambda i: (i, 0)),
        pl.BlockSpec((tm, 1024), lambda i: (i, 2)),
        pl.BlockSpec((tm, 1024), lambda i: (i, 3)),
        pl.BlockSpec((tm, 1024), lambda i: (i, 4)),
        pl.BlockSpec((tm, NA_W), lambda i: (i, 0)),
        pl.BlockSpec((tm, d), lambda i: (i, 0)),
        _resident(gog), _resident(wog), _resident(won), _resident(wout), _resident(gf),
    ]
    args = [of, ob, proj, proj, proj, na, x, gog, wog, won, wout, gf]
    h2_w = d // 2 if with_router else d
    out_specs = [pl.BlockSpec((tm, d), lambda i: (i, 0)), pl.BlockSpec((tm, h2_w), lambda i: (i, 0))]
    out_shape = [jax.ShapeDtypeStruct((n, d), F32),
                 jax.ShapeDtypeStruct((n, h2_w), jnp.int32 if with_router else BF16)]
    if with_router:
        before = jnp.asarray(np.triu(np.ones((tm, tm), np.float32), 1), BF16)
        in_specs += [_resident(router[0]), _resident(router[1]), _resident(before)]
        args += list(router) + [before]
        out_specs.append(pl.BlockSpec((N_EXPERTS, tm), lambda i: (0, i)))
        out_shape.append(jax.ShapeDtypeStruct((N_EXPERTS, n), F32))
        out_specs.append(pl.BlockSpec((N_EXPERTS, 128), lambda i: (0, 0)))
        out_shape.append(jax.ShapeDtypeStruct((N_EXPERTS, 128), F32))
    return pl.pallas_call(
        functools.partial(_postmix_kernel, with_router=with_router),
        grid=(n // tm,),
        in_specs=in_specs,
        out_specs=out_specs,
        out_shape=out_shape,
        compiler_params=_params(("arbitrary" if with_router else "parallel",)),
        name="postmix_router" if with_router else "postmix",
    )(*args)


def _ffn_kernel(x1_ref, h_ref, wg_ref, wu_ref, wd_ref, o_ref, acc_ref):
    j = pl.program_id(1)

    @pl.when(j == 0)
    def _():
        acc_ref[...] = jnp.zeros_like(acc_ref)

    h = h_ref[...]
    g = _dot(h, wg_ref[...])
    u = _dot(h, wu_ref[...])
    a = (g * _sigmoid(g) * u).astype(BF16)
    acc_ref[...] += _dot(a, wd_ref[...])

    @pl.when(j == pl.num_programs(1) - 1)
    def _():
        o_ref[...] = x1_ref[...] + acc_ref[...]


def _ffn(x1, h2, wg, wu, wd, tm, tf):
    n, d = x1.shape
    dff = wg.shape[1]
    return pl.pallas_call(
        _ffn_kernel,
        grid=(n // tm, dff // tf),
        in_specs=[
            pl.BlockSpec((tm, d), lambda i, j: (i, 0)),
            pl.BlockSpec((tm, d), lambda i, j: (i, 0)),
            pl.BlockSpec((d, tf), lambda i, j: (0, j)),
            pl.BlockSpec((d, tf), lambda i, j: (0, j)),
            pl.BlockSpec((tf, d), lambda i, j: (j, 0)),
        ],
        out_specs=pl.BlockSpec((tm, d), lambda i, j: (i, 0)),
        out_shape=jax.ShapeDtypeStruct((n, d), F32),
        scratch_shapes=[pltpu.VMEM((tm, d), F32)],
        compiler_params=_params(("parallel", "arbitrary")),
        name="ffn",
    )(x1, h2, wg, wu, wd)


SC_CORES = 2
SC_SUBCORES = 16
SC_WORKERS = SC_CORES * SC_SUBCORES
SC_CHUNK = 32
MOE_TM = 512


def _sc_gather_rows(table, idx):
    nrow, d = idx.shape[0], table.shape[1]
    assert nrow % (SC_WORKERS * SC_CHUNK) == 0
    per_worker = nrow // SC_WORKERS
    mesh = plsc.VectorSubcoreMesh(core_axis_name="c", subcore_axis_name="s",
                                  num_cores=SC_CORES, num_subcores=SC_SUBCORES)

    @functools.partial(
        pl.kernel, mesh=mesh, out_type=jax.ShapeDtypeStruct((nrow, d), table.dtype),
        scratch_types=[pltpu.VMEM((SC_CHUNK,), jnp.int32), pltpu.VMEM((SC_CHUNK, d), table.dtype),
                       pltpu.SemaphoreType.DMA])
    def gather(table_hbm, idx_hbm, out_hbm, idx_v, rows_v, sem):
        base = (lax.axis_index("s") * SC_CORES + lax.axis_index("c")) * per_worker

        @pl.loop(0, per_worker // SC_CHUNK)
        def _(j):
            off = base + j * SC_CHUNK
            pltpu.sync_copy(idx_hbm.at[pl.ds(off, SC_CHUNK)], idx_v)
            pltpu.async_copy(table_hbm.at[idx_v], rows_v, sem).wait()
            pltpu.sync_copy(rows_v, out_hbm.at[pl.ds(off, SC_CHUNK)])

    return gather(table, idx)


def _sc_scatter_rows(rows, pos12, n_out):
    n, d = rows.shape
    assert n % (SC_WORKERS * SC_CHUNK) == 0
    per_worker = n // SC_WORKERS
    idx = pos12.reshape(2, n)
    mesh = plsc.VectorSubcoreMesh(core_axis_name="c", subcore_axis_name="s",
                                  num_cores=SC_CORES, num_subcores=SC_SUBCORES)

    @functools.partial(
        pl.kernel, mesh=mesh, out_type=jax.ShapeDtypeStruct((n_out, d), rows.dtype),
        scratch_types=[pltpu.VMEM((2, SC_CHUNK), jnp.int32), pltpu.VMEM((SC_CHUNK, d), rows.dtype)])
    def scatter(rows_hbm, idx_hbm, out_hbm, idx_v, rows_v):
        base = (lax.axis_index("s") * SC_CORES + lax.axis_index("c")) * per_worker

        @pl.loop(0, per_worker // SC_CHUNK)
        def _(j):
            off = base + j * SC_CHUNK
            pltpu.sync_copy(rows_hbm.at[pl.ds(off, SC_CHUNK)], rows_v)
            for k in range(2):
                pltpu.sync_copy(idx_hbm.at[k, pl.ds(off, SC_CHUNK)], idx_v.at[k])
                pltpu.sync_copy(rows_v, out_hbm.at[idx_v.at[k]])

    return scatter(rows, idx)


def _moe_plan(route, counts):
    n = route.shape[1]
    count = counts[:, 0].astype(jnp.int32)
    group = (count + MOE_TM - 1) // MOE_TM * MOE_TM
    ends = jnp.cumsum(group)
    start = ends - group
    experts = jnp.arange(N_EXPERTS, dtype=jnp.int32)[:, None]

    def rows_of(e, rank):
        return jnp.sum(jnp.where(e.astype(jnp.int32)[None, :] == experts, start[:, None], 0), axis=0) \
            + rank.astype(jnp.int32)

    pos12 = jnp.concatenate([rows_of(route[0], route[2]), rows_of(route[1], route[3])])
    n_rows = 2 * n + N_EXPERTS * MOE_TM
    tile_start = jnp.arange(n_rows // MOE_TM, dtype=jnp.int32) * MOE_TM
    tile_expert = jnp.minimum(jnp.sum(tile_start[:, None] >= ends[None, :], axis=1), N_EXPERTS - 1)
    filled = (start + count)[tile_expert] - tile_start
    tile_rows = jnp.where(tile_start < ends[-1], jnp.clip(filled, 0, MOE_TM), 0).astype(jnp.int32)
    w12 = jnp.stack([route[4], route[5]], axis=1)
    return pos12, n_rows, tile_expert.astype(jnp.int32), tile_rows, w12


def _moe_group_kernel(te_ref, tr_ref, xs_ref, wg_ref, wu_ref, wd_ref, ys_ref):
    n_filled = tr_ref[pl.program_id(0)]
    used = n_filled > 0

    @pl.when(used)
    def _():
        row = lax.broadcasted_iota(jnp.int32, xs_ref.shape, 0)
        x = _unpack_bf16_pairs(jnp.where(row < n_filled, xs_ref[...], 0)).astype(BF16)
        g = _dot(x, wg_ref[0])
        u = _dot(x, wu_ref[0])
        ys_ref[...] = _pack_bf16_pairs(_dot((g * _sigmoid(g) * u).astype(BF16), wd_ref[0]))

    @pl.when(jnp.logical_not(used))
    def _():
        ys_ref[...] = jnp.zeros_like(ys_ref)


def _moe_grouped(xs, tile_expert, tile_rows, wg, wu, wd):
    p, dp = xs.shape
    _, d, dfe = wg.shape
    grid_spec = pltpu.PrefetchScalarGridSpec(
        num_scalar_prefetch=2,
        grid=(p // MOE_TM,),
        in_specs=[
            pl.BlockSpec((MOE_TM, dp), lambda t, te, tr: (t, 0)),
            pl.BlockSpec((1, d, dfe), lambda t, te, tr: (te[t], 0, 0)),
            pl.BlockSpec((1, d, dfe), lambda t, te, tr: (te[t], 0, 0)),
            pl.BlockSpec((1, dfe, d), lambda t, te, tr: (te[t], 0, 0)),
        ],
        out_specs=pl.BlockSpec((MOE_TM, dp), lambda t, te, tr: (t, 0)),
    )
    return pl.pallas_call(
        _moe_group_kernel,
        grid_spec=grid_spec,
        out_shape=jax.ShapeDtypeStruct((p, dp), jnp.int32),
        compiler_params=_params(("arbitrary",)),
        name="moe_grouped",
    )(tile_expert, tile_rows, xs, wg, wu, wd)


def _moe_combine_kernel(x1_ref, y1_ref, y2_ref, w_ref, o_ref):
    w = w_ref[...]
    y1 = _unpack_bf16_pairs(y1_ref[...])
    y2 = _unpack_bf16_pairs(y2_ref[...])
    o_ref[...] = x1_ref[...] + w[:, 0:1] * y1 + w[:, 1:2] * y2


def _moe_combine(x1, y, w12, tm):
    n, d = x1.shape
    nt = n // tm
    return pl.pallas_call(
        _moe_combine_kernel,
        grid=(nt,),
        in_specs=[
            pl.BlockSpec((tm, d), lambda i: (i, 0)),
            pl.BlockSpec((tm, d // 2), lambda i: (i, 0)),
            pl.BlockSpec((tm, d // 2), lambda i: (i + nt, 0)),
            pl.BlockSpec((tm, 2), lambda i: (i, 0)),
        ],
        out_specs=pl.BlockSpec((tm, d), lambda i: (i, 0)),
        out_shape=jax.ShapeDtypeStruct((n, d), F32),
        compiler_params=_params(("parallel",)),
        name="moe_combine",
    )(x1, y, y, w12)


def _moe(x1, h2, route, counts, wg, wu, wd):
    pos12, n_rows, tile_expert, tile_rows, w12 = _moe_plan(route, counts)
    xs = _sc_scatter_rows(h2, pos12, n_rows)
    ys = _moe_grouped(xs, tile_expert, tile_rows, wg, wu, wd)
    y = _sc_gather_rows(ys, pos12)
    return _moe_combine(x1, y, w12, _row_tile(x1.shape[0], 1024))


def _row_tile(n, want):
    t = min(n, want)
    assert n % t == 0
    return t


def _prep_layer(l, norm_mix, w_in, w_decay_f, b_decay_f, w_decay_b, b_decay_b, gla_out_norm, q_norm,
                k_norm, rpb, w_o_gla, w_o_na, w_out, norm_ffn):
    w = w_in[l]
    c0 = 2 * QK_W + 2 * V_W
    c1 = c0 + 2 * GLA_RANK
    c2 = c1 + 3 * NA_W
    w_main = jnp.concatenate([w[:, :QK_W] * (GLA_DK ** -0.5), w[:, QK_W:c0], w[:, c2:], w[:, c1:c2]],
                             axis=1).astype(BF16)
    wz = w[:, c0:c1].astype(BF16)
    zero = jnp.zeros((GLA_RANK, QK_W), F32)
    wdec = jnp.concatenate([jnp.concatenate([w_decay_f[l], zero], axis=1),
                            jnp.concatenate([zero, w_decay_b[l]], axis=1)], axis=0)
    bdec = jnp.concatenate([b_decay_f[l], b_decay_b[l]])[None, :]
    qg = jnp.tile(q_norm[l] * (NA_DH ** -0.5), NA_HEADS)[None, :]
    kg = jnp.tile(k_norm[l], NA_HEADS)[None, :]
    return dict(
        g_mix=norm_mix[l][None, :], w_main=w_main, wz=wz, wd=wdec.astype(BF16), bdec=bdec,
        qg=qg, kg=kg, bias=_na_bias_table(rpb[l]),
        gog=gla_out_norm[l].reshape(1, V_W), wog=w_o_gla[l].astype(BF16),
        won=w_o_na[l].astype(BF16), wout=w_out[l].astype(BF16), g_ffn=norm_ffn[l][None, :])


def _router_operands(router):
    hi, lo = _split_bf16(router.T)
    return jnp.concatenate([hi, lo], axis=0), jnp.concatenate([hi, jnp.zeros_like(hi)], axis=0)


def _trunk(x, layers, dense, moe):
    b, l, d = x.shape
    n = b * l
    xf = x.reshape(n, d)
    head_mean = jnp.asarray(np.kron(np.eye(NA_HEADS), np.full((NA_DH, NA_DH), 1.0 / NA_DH)), BF16)
    tb = _row_tile(l, 512)
    sb = 2 if b % 2 == 0 else 1
    for li, p in enumerate(layers):
        proj, la, lamin = _inproj(xf, p["g_mix"], p["w_main"], p["wz"], p["wd"], p["bdec"],
                           head_mean, p["qg"], p["kg"], _row_tile(n, 512))
        proj3 = proj.reshape(b, l, PROJ_W)
        la3 = la.reshape(b, l, 2 * QK_W)
        of = _gla(proj3, la3, lamin, False, tb, sb).reshape(n, V_W)
        ob = _gla(proj3, la3, lamin, True, tb, sb).reshape(n, V_W)
        na = _natten(proj3, p["bias"]).reshape(n, NA_W)
        if li % 2 == 0:
            wg, wu, wd = dense[li // 2]
            x1, h2 = _postmix(of, ob, proj, na, xf, p["gog"], p["wog"], p["won"], p["wout"],
                              p["g_ffn"], None, _row_tile(n, 512))
            xf = _ffn(x1, h2, wg, wu, wd, _row_tile(n, 1024), wg.shape[1] // 2)
        else:
            router, wg, wu, wd = moe[li // 2]
            x1, h2, route, counts = _postmix(of, ob, proj, na, xf, p["gog"], p["wog"], p["won"], p["wout"],
                                             p["g_ffn"], router, _row_tile(n, 512))
            xf = _moe(x1, h2, route, counts, wg, wu, wd)
    return xf.reshape(b, l, d)


def kernel(x_prompt, x_sample, norm_mix, w_in, w_decay_f, b_decay_f, w_decay_b, b_decay_b, gla_out_norm,
           q_norm, k_norm, rpb, w_o_gla, w_o_na, w_out, norm_ffn, ffn_w_gate, ffn_w_up, ffn_w_down,
           moe_router, moe_w_gate, moe_w_up, moe_w_down):
    depth = w_in.shape[0]
    layers = [_prep_layer(l, norm_mix, w_in, w_decay_f, b_decay_f, w_decay_b, b_decay_b, gla_out_norm,
                          q_norm, k_norm, rpb, w_o_gla, w_o_na, w_out, norm_ffn) for l in range(depth)]
    dense = [(ffn_w_gate[j].astype(BF16), ffn_w_up[j].astype(BF16), ffn_w_down[j].astype(BF16))
             for j in range(ffn_w_gate.shape[0])]
    moe = [(_router_operands(moe_router[j]), moe_w_gate[j].astype(BF16), moe_w_up[j].astype(BF16),
            moe_w_down[j].astype(BF16)) for j in range(moe_router.shape[0])]
    return (_trunk(x_prompt, layers, dense, moe), _trunk(x_sample, layers, dense, moe))
```

```python
import functools

import numpy as np
import jax
import jax.numpy as jnp
from jax import lax
from jax.experimental import pallas as pl
from jax.experimental.pallas import tpu as pltpu
from jax.experimental.pallas import tpu_sc as plsc

F32 = jnp.float32
BF16 = jnp.bfloat16

EPS = 1e-6
GRID_W = 64
GLA_HEADS = 4
GLA_DK = 128
GLA_DV = 256
GLA_CHUNK = 64
GLA_RANK = 16
GLA_TAU = 16.0
NA_HEADS = 8
NA_DH = 64
NA_KR = 8
NA_KC = 16
N_EXPERTS = 8
NEG = -1e30
LOG2_E = 1.4426950408889634

VMEM_LIMIT = 56 * 1024 * 1024
SUBLANES = 8
LANES = 128
D_MODEL = 1024

TM_INPROJ = 512
TM_POSTMIX = 512
TM_FFN = 1024
FFN_SPLIT = 2
TM_COMBINE = 1024
GLA_BLOCK = 512
GLA_SEQS = 2

QK_W = GLA_HEADS * GLA_DK
V_W = GLA_HEADS * GLA_DV
NA_W = NA_HEADS * NA_DH
PROJ_W = 2 * QK_W + 2 * V_W + 2 * D_MODEL + 3 * NA_W
CW = 512
NQ_CHUNK = (2 * QK_W + 2 * V_W + 2 * D_MODEL) // CW
NK_CHUNK = NQ_CHUNK + 1
INPROJ_SUB = 256


def _params(sem):
    return pltpu.CompilerParams(dimension_semantics=sem, vmem_limit_bytes=VMEM_LIMIT)


def _resident(a):
    nd = a.ndim
    return pl.BlockSpec(a.shape, lambda *_: (0,) * nd, pipeline_mode=pl.Buffered(1))


def _split_bf16(a):
    hi = a.astype(BF16)
    lo = (a - hi.astype(F32)).astype(BF16)
    return hi, lo


def _dot(a, b):
    return jnp.dot(a, b, preferred_element_type=F32)


def _dot_nt(a, b):
    return lax.dot_general(a, b, (((1,), (1,)), ((), ())), preferred_element_type=F32)


def _dot_tn(a, b):
    return lax.dot_general(a, b, (((0,), (0,)), ((), ())), preferred_element_type=F32)


def _sigmoid(x):
    return 1.0 / (1.0 + jnp.exp2(x * (-LOG2_E)))


def _pack_bf16_pairs(x):
    c = x.shape[1] // 2
    as_bits = lambda a: lax.bitcast_convert_type(a.astype(BF16).astype(F32), jnp.uint32)
    word = (as_bits(x[:, :c]) >> 16) | (as_bits(x[:, c:]) & jnp.uint32(0xFFFF0000))
    return lax.bitcast_convert_type(word, jnp.int32)


def _unpack_bf16_pairs(w):
    u = lax.bitcast_convert_type(w, jnp.uint32)
    lo = lax.bitcast_convert_type(u << 16, F32)
    hi = lax.bitcast_convert_type(u & jnp.uint32(0xFFFF0000), F32)
    return jnp.concatenate([lo, hi], axis=1)


def _rmsnorm(x, g):
    ms = jnp.mean(x * x, axis=-1, keepdims=True)
    return x * lax.rsqrt(ms + EPS) * g


def _inproj_kernel(x_ref, g_ref, w_ref, wz_ref, wd_ref, bdec_ref, hm_ref, qg_ref, kg_ref,
                   proj_ref, la_ref, lamin_ref):
    for t in range(x_ref.shape[0] // INPROJ_SUB):
        rs = slice(t * INPROJ_SUB, (t + 1) * INPROJ_SUB)
        h = _rmsnorm(x_ref[rs, :], g_ref[...]).astype(BF16)
        z = _dot(h, wz_ref[...])
        y = _dot(z.astype(BF16), wd_ref[...]) + bdec_ref[...]
        ls = jnp.minimum(y, 0.0) - jnp.log(1.0 + jnp.exp(-jnp.abs(y)))
        la = ls * (1.0 / GLA_TAU)
        la_ref[rs, :] = la
        lamin_ref[t] = jnp.broadcast_to(jnp.min(la, axis=0, keepdims=True), lamin_ref.shape[1:])
        for c in range(PROJ_W // CW):
            acc = _dot(h, w_ref[:, c * CW:(c + 1) * CW])
            if c in (NQ_CHUNK, NK_CHUNK):
                gain = qg_ref if c == NQ_CHUNK else kg_ref
                ms = _dot((acc * acc).astype(BF16), hm_ref[...])
                acc = acc * lax.rsqrt(ms + EPS) * gain[...]
            proj_ref[rs, c * CW:(c + 1) * CW] = acc.astype(BF16)


def _inproj(x, g, w_main, wz, wd, bdec, hm, qg, kg, tm):
    n, d = x.shape
    return pl.pallas_call(
        _inproj_kernel,
        grid=(n // tm,),
        in_specs=[
            pl.BlockSpec((tm, d), lambda i: (i, 0)),
            _resident(g), _resident(w_main), _resident(wz), _resident(wd),
            _resident(bdec), _resident(hm), _resident(qg), _resident(kg),
        ],
        out_specs=[
            pl.BlockSpec((tm, PROJ_W), lambda i: (i, 0)),
            pl.BlockSpec((tm, 2 * QK_W), lambda i: (i, 0)),
            pl.BlockSpec((tm // INPROJ_SUB, SUBLANES, 2 * QK_W), lambda i: (i, 0, 0)),
        ],
        out_shape=[
            jax.ShapeDtypeStruct((n, PROJ_W), BF16),
            jax.ShapeDtypeStruct((n, 2 * QK_W), F32),
            jax.ShapeDtypeStruct((n // INPROJ_SUB, SUBLANES, 2 * QK_W), F32),
        ],
        compiler_params=_params(("parallel",)),
        name="inproj",
    )(x, g, w_main, wz, wd, bdec, hm, qg, kg)


N_LEVELS = 6
INTER_BLK = N_LEVELS
STATE_BLK = N_LEVELS + 1


def _decay_sum_matrix(reverse):
    c = GLA_CHUNK
    m_all = np.zeros((N_LEVELS + 2, c, c), np.float32)
    for l in range(N_LEVELS):
        m = 1 << l
        for p in range(c):
            mid = (p // (2 * m)) * 2 * m + m
            if p >= mid:
                m_all[l, p, mid + 1:p + 1] = 1.0
            else:
                m_all[l, p, p + 1:mid + 1] = 1.0
    for p in range(c):
        m_all[INTER_BLK, p, :p + 1] = 1.0
        m_all[STATE_BLK, p, p + 1:] = 1.0
    if reverse:
        m_all = m_all[:, ::-1, ::-1]
    return np.ascontiguousarray(m_all).reshape((N_LEVELS + 2) * c, c)


def _level_matrix(reverse):
    i = np.arange(GLA_CHUNK)[:, None]
    j = np.arange(GLA_CHUNK)[None, :]
    x = i ^ j
    lvl = np.where(x > 0, np.floor(np.log2(np.maximum(x, 1))), N_LEVELS).astype(np.int32)
    valid = (i <= j) if reverse else (i >= j)
    return np.where(valid, lvl, N_LEVELS + 1).astype(np.int32)


GLA_FAST_CHUNK = 128
GLA_FAST_MAX_DECAY = 60.0


def _tri_matrix(reverse, c):
    i = np.arange(c)[:, None]
    j = np.arange(c)[None, :]
    return ((j >= i) if reverse else (j <= i)).astype(np.float32)


def _decay_columns(dec_row):
    t = jnp.broadcast_to(dec_row, (GLA_DK, GLA_DK)).T
    return jnp.concatenate([t, t], axis=1)


def _gla_kernel(q_ref, k_ref, v_ref, la_ref, lamin_ref, m_ref, lvl_ref, tri_ref, o_ref,
                s_ref, lr_ref, qh_ref, kh_ref, ks_ref, dec_ref, *, reverse, tb, sb):
    @pl.when(pl.program_id(1) == 0)
    def _():
        s_ref[...] = jnp.zeros_like(s_ref)

    fc = GLA_FAST_CHUNK
    n_fast = tb // fc
    total_row = 0 if reverse else fc - 1
    seqs = range(sb)

    def fast_rows(ci):
        return pl.ds(pl.multiple_of(ci * fc, fc), fc)

    def sc_lanes(h):
        return slice(h * 2 * GLA_DK, h * 2 * GLA_DK + GLA_DK)

    def qh_lanes(h):
        return slice(h * 2 * GLA_DK + GLA_DK, (h + 1) * 2 * GLA_DK)

    def fast_factors(ci, carry):
        rows = fast_rows(ci)
        for s in seqs:
            b = _dot(tri_ref[...], la_ref[s, rows, :].astype(BF16))
            dec = jnp.exp(b[total_row:total_row + 1, :])
            f_q = jnp.exp(b)
            qh = (q_ref[s, rows, :].astype(F32) * f_q).astype(BF16)
            kh = k_ref[s, rows, :].astype(F32) / f_q
            qh_ref[s, rows, :] = qh
            for h in range(GLA_HEADS):
                lr_ref[s, rows, qh_lanes(h)] = qh[:, h * GLA_DK:(h + 1) * GLA_DK]
            kh_ref[s, rows, :] = kh.astype(BF16)
            ks_ref[s, rows, :] = (kh * dec).astype(BF16)
            dec_ref[s, pl.ds(pl.multiple_of(ci * SUBLANES, SUBLANES), SUBLANES), :] = (
                jnp.broadcast_to(dec, (SUBLANES, QK_W)))
        return carry

    def fast_scores(ci, carry):
        rows = fast_rows(ci)
        i = lax.broadcasted_iota(jnp.int32, (fc, fc), 0)
        j = lax.broadcasted_iota(jnp.int32, (fc, fc), 1)
        causal = (i <= j) if reverse else (i >= j)
        for s in seqs:
            for h in range(GLA_HEADS):
                hs = slice(h * GLA_DK, (h + 1) * GLA_DK)
                sc = _dot_nt(qh_ref[s, rows, hs], kh_ref[s, rows, hs])
                lr_ref[s, rows, sc_lanes(h)] = jnp.where(causal, sc, 0.0).astype(BF16)
        return carry

    def fast_state(ci, carry):
        cc = (n_fast - 1 - ci) if reverse else ci
        rows = fast_rows(cc)
        heads = [(s, h) for s in seqs for h in range(GLA_HEADS)]
        vals = {(s, h): v_ref[s, rows, h * GLA_DV:(h + 1) * GLA_DV] for s, h in heads}
        kv = {(s, h): _dot_tn(ks_ref[s, rows, h * GLA_DK:(h + 1) * GLA_DK], vals[s, h]) for s, h in heads}
        for s, h in heads:
            dec = dec_ref[s, pl.ds(pl.multiple_of(cc * SUBLANES, SUBLANES), SUBLANES), :][
                :1, h * GLA_DK:(h + 1) * GLA_DK]
            st = s_ref[s, h]
            o = _dot(lr_ref[s, rows, h * 2 * GLA_DK:(h + 1) * 2 * GLA_DK],
                     jnp.concatenate([vals[s, h], st.astype(BF16)], axis=0))
            o_ref[s, rows, h * GLA_DV:(h + 1) * GLA_DV] = o.astype(o_ref.dtype)
            s_ref[s, h] = st * _decay_columns(dec) + kv[s, h]
        return carry

    def robust_chunk(ci, carry):
        c = GLA_CHUNK
        n_chunks = tb // c
        lvl = lvl_ref[...]
        row = lax.broadcasted_iota(jnp.int32, (c, GLA_DK), 0)
        total_row = 0 if reverse else c - 1
        cc = (n_chunks - 1 - ci) if reverse else ci
        rows = pl.ds(pl.multiple_of(cc * c, c), c)
        for s in seqs:
            la_hi, la_lo = _split_bf16(la_ref[s, rows, :])
            fac = jnp.exp(_dot(m_ref[...], la_hi) + _dot(m_ref[...], la_lo))
            for h in range(GLA_HEADS):
                hs = slice(h * GLA_DK, (h + 1) * GLA_DK)
                vs = slice(h * GLA_DV, (h + 1) * GLA_DV)
                q = q_ref[s, rows, hs].astype(F32)
                k = k_ref[s, rows, hs].astype(F32)
                v = v_ref[s, rows, vs]
                scores = jnp.zeros((c, c), F32)
                for l in range(N_LEVELS):
                    is_query = ((row >> l) & 1) == (0 if reverse else 1)
                    f_l = fac[l * c:(l + 1) * c, hs]
                    g_l = (jnp.where(is_query, q, k) * f_l).astype(BF16)
                    scores = jnp.where(lvl == l, _dot_nt(g_l, g_l), scores)
                scores = jnp.where(lvl == N_LEVELS, _dot_nt(q.astype(BF16), k.astype(BF16)), scores)
                f_in = fac[INTER_BLK * c:(INTER_BLK + 1) * c, hs]
                f_st = fac[STATE_BLK * c:(STATE_BLK + 1) * c, hs]
                st = s_ref[s, h]
                inter = _dot((q * f_in).astype(BF16), st.astype(BF16))
                intra = _dot(scores.astype(BF16), v)
                o_ref[s, rows, vs] = (inter + intra).astype(o_ref.dtype)
                dec = f_in[total_row:total_row + 1, :]
                s_ref[s, h] = st * _decay_columns(dec) + _dot_tn((k * f_st).astype(BF16), v)
        return carry

    bounded = jnp.min(lamin_ref[...]) >= -GLA_FAST_MAX_DECAY / GLA_FAST_CHUNK

    @pl.when(bounded)
    def _():
        lax.fori_loop(0, n_fast, fast_factors, 0, unroll=4)
        lax.fori_loop(0, n_fast, fast_scores, 0, unroll=4)
        lax.fori_loop(0, n_fast, fast_state, 0, unroll=4)

    @pl.when(jnp.logical_not(bounded))
    def _():
        lax.fori_loop(0, tb // GLA_CHUNK, robust_chunk, 0)


def _gla(proj3, la3, lamin, reverse, tb, sb):
    b, l, _ = proj3.shape
    nb = l // tb
    lamin4 = lamin.reshape(b, -1, SUBLANES, 2 * QK_W)
    tiles = lamin4.shape[1] // nb
    blk = (lambda i: nb - 1 - i) if reverse else (lambda i: i)
    m_all = jnp.asarray(_decay_sum_matrix(reverse), BF16)
    lvl = jnp.asarray(_level_matrix(reverse))
    tri = jnp.asarray(_tri_matrix(reverse, GLA_FAST_CHUNK), BF16)
    kern = functools.partial(_gla_kernel, reverse=reverse, tb=tb, sb=sb)
    decay_col = 1 if reverse else 0
    return pl.pallas_call(
        kern,
        grid=(b // sb, nb),
        in_specs=[
            pl.BlockSpec((sb, tb, QK_W), lambda s, i: (s, blk(i), 0)),
            pl.BlockSpec((sb, tb, QK_W), lambda s, i: (s, blk(i), 1)),
            pl.BlockSpec((sb, tb, V_W), lambda s, i: (s, blk(i), 1)),
            pl.BlockSpec((sb, tb, QK_W), lambda s, i: (s, blk(i), decay_col)),
            pl.BlockSpec((sb, tiles, SUBLANES, QK_W), lambda s, i: (s, blk(i), 0, decay_col)),
            _resident(m_all), _resident(lvl), _resident(tri),
        ],
        out_specs=pl.BlockSpec((sb, tb, V_W), lambda s, i: (s, blk(i), 0)),
        out_shape=jax.ShapeDtypeStruct((b, l, V_W), BF16),
        scratch_shapes=[
            pltpu.VMEM((sb, GLA_HEADS, GLA_DK, GLA_DV), F32),
            pltpu.VMEM((sb, tb, 2 * QK_W), BF16),
            pltpu.VMEM((sb, tb, QK_W), BF16),
            pltpu.VMEM((sb, tb, QK_W), BF16),
            pltpu.VMEM((sb, tb, QK_W), BF16),
            pltpu.VMEM((sb, tb // GLA_FAST_CHUNK * SUBLANES, QK_W), F32),
        ],
        compiler_params=_params(("parallel", "arbitrary")),
        name="gla_bwd" if reverse else "gla_fwd",
    )(proj3, proj3, proj3, la3, lamin4, m_all, lvl, tri)


NA_GROUP = 8
NA_BLK = NA_GROUP * GRID_W
NA_BAND = NA_KR * GRID_W


def _na_bias_table(rpb):
    qc = np.arange(GRID_W)[:, None]
    kc = np.arange(GRID_W)[None, :]
    col_start = np.clip(qc - NA_KC // 2, 0, GRID_W - NA_KC)
    valid = (kc >= col_start) & (kc < col_start + NA_KC)
    dc = np.clip(kc - qc, -(NA_KC - 1), NA_KC - 1) + NA_KC - 1
    onehot = ((dc[None] == np.arange(2 * NA_KC - 1)[:, None, None]) & valid[None]).astype(np.float32)
    mask = np.where(valid, 0.0, NEG).astype(np.float32)
    rows = jnp.stack([rpb[:, NA_KR - 1 - d:2 * NA_KR - 1 - d, :] for d in range(NA_KR)])
    t = jnp.einsum('dhkc,cqj->dhqkj', rows, jnp.asarray(onehot), precision=lax.Precision.HIGHEST)
    t = t + jnp.asarray(mask)[None, None, :, None, :]
    return t.reshape(NA_KR, NA_HEADS // 2, 2 * GRID_W, NA_BAND)


def _na_kernel(q_ref, kp_ref, kc_ref, kn_ref, vp_ref, vc_ref, vn_ref, bias_ref, o_ref,
               kwin_ref, vwin_ref, *, rows):
    g = pl.program_id(1)
    for j, (kr, vr) in enumerate(((kp_ref, vp_ref), (kc_ref, vc_ref), (kn_ref, vn_ref))):
        kwin_ref[j * NA_BLK:(j + 1) * NA_BLK, :] = kr[0]
        vwin_ref[j * NA_BLK:(j + 1) * NA_BLK, :] = vr[0]
    lane = lax.broadcasted_iota(jnp.int32, (GRID_W, 2 * NA_DH), 1)
    first = lane < NA_DH

    pairs = [slice(p * 2 * NA_DH, (p + 1) * 2 * NA_DH) for p in range(NA_HEADS // 2)]

    def band_start(rl):
        r = g * NA_GROUP + rl
        rs = jnp.clip(r - NA_KR // 2, 0, rows - NA_KR)
        return pl.multiple_of((rs - (g - 1) * NA_GROUP) * GRID_W, GRID_W), r - rs

    def scores_of(rl):
        off, d = band_start(rl)
        out = []
        for p, ps in enumerate(pairs):
            qp = q_ref[0, rl * GRID_W:(rl + 1) * GRID_W, ps]
            zero = jnp.zeros_like(qp)
            qs = jnp.concatenate([jnp.where(first, qp, zero), jnp.where(first, zero, qp)], axis=0)
            kb = kwin_ref[pl.ds(off, NA_BAND), ps]
            out.append(_dot_nt(qs, kb) + bias_ref[d, p])
        return out

    def finish(rl, scores):
        off, _ = band_start(rl)
        probs = []
        for s in scores:
            e = jnp.exp(s - jnp.max(s, axis=-1, keepdims=True))
            probs.append((e.astype(BF16), jnp.sum(e, axis=-1, keepdims=True)))
        for (e, den), ps in zip(probs, pairs):
            vb = vwin_ref[pl.ds(off, NA_BAND), ps]
            pv = _dot(e, vb) / den
            o_ref[0, rl * GRID_W:(rl + 1) * GRID_W, ps] = (
                jnp.where(first, pv[:GRID_W], pv[GRID_W:]).astype(o_ref.dtype))

    scores = scores_of(0)
    for rl in range(NA_GROUP):
        nxt = scores_of(rl + 1) if rl + 1 < NA_GROUP else None
        finish(rl, scores)
        scores = nxt


def _natten(proj3, bias):
    b, l, _ = proj3.shape
    rows = l // GRID_W
    ng = rows // NA_GROUP
    qcol, kcol, vcol = NQ_CHUNK, NQ_CHUNK + 1, NQ_CHUNK + 2
    prev = lambda i: jnp.maximum(i - 1, 0)
    nxt = lambda i: jnp.minimum(i + 1, ng - 1)
    spec = lambda f, col: pl.BlockSpec((1, NA_BLK, NA_W), lambda s, i: (s, f(i), col))
    same = lambda i: i
    return pl.pallas_call(
        functools.partial(_na_kernel, rows=rows),
        grid=(b, ng),
        in_specs=[
            spec(same, qcol),
            spec(prev, kcol), spec(same, kcol), spec(nxt, kcol),
            spec(prev, vcol), spec(same, vcol), spec(nxt, vcol),
            _resident(bias),
        ],
        out_specs=pl.BlockSpec((1, NA_BLK, NA_W), lambda s, i: (s, i, 0)),
        out_shape=jax.ShapeDtypeStruct((b, l, NA_W), BF16),
        scratch_shapes=[pltpu.VMEM((3 * NA_BLK, NA_W), BF16), pltpu.VMEM((3 * NA_BLK, NA_W), BF16)],
        compiler_params=_params(("parallel", "parallel")),
        name="natten",
    )(proj3, proj3, proj3, proj3, proj3, proj3, proj3, bias)


def _postmix_kernel(*refs, with_router):
    if with_router:
        (of_ref, ob_ref, r_ref, ga_ref, gb_ref, na_ref, x_ref, gog_ref, wog_ref, won_ref, wout_ref,
         gf_ref, rh_ref, rl_ref, before_ref, x1_ref, h2_ref, route_ref, count_ref) = refs
    else:
        (of_ref, ob_ref, r_ref, ga_ref, gb_ref, na_ref, x_ref, gog_ref, wog_ref, won_ref, wout_ref,
         gf_ref, x1_ref, h2_ref) = refs
    o = of_ref[...].astype(F32) + ob_ref[...].astype(F32)
    parts = []
    for h in range(GLA_HEADS):
        seg = o[:, h * GLA_DV:(h + 1) * GLA_DV]
        ms = jnp.mean(seg * seg, axis=-1, keepdims=True)
        parts.append(seg * lax.rsqrt(ms + EPS))
    r = r_ref[...].astype(F32)
    on = jnp.concatenate(parts, axis=-1) * gog_ref[...] * (r * _sigmoid(r))
    ya = _dot(on.astype(BF16), wog_ref[...])
    yb = _dot(na_ref[...], won_ref[...])
    merged = _sigmoid(ga_ref[...].astype(F32)) * ya + _sigmoid(gb_ref[...].astype(F32)) * yb
    x1 = x_ref[...] + _dot(merged.astype(BF16), wout_ref[...])
    x1_ref[...] = x1
    h2 = _rmsnorm(x1, gf_ref[...])
    h2_ref[...] = _pack_bf16_pairs(h2) if with_router else h2.astype(BF16)
    if with_router:
        hh, hl = _split_bf16(h2)
        lg2 = _dot_nt(rh_ref[...], hh) + _dot_nt(rl_ref[...], hl)
        lg = lg2[:N_EXPERTS] + lg2[N_EXPERTS:]
        row = lax.broadcasted_iota(jnp.int32, lg.shape, 0).astype(F32)
        m1 = jnp.max(lg, axis=0, keepdims=True)
        i1 = jnp.min(jnp.where(lg == m1, row, float(N_EXPERTS)), axis=0, keepdims=True)
        lg_rest = jnp.where(row == i1, -jnp.inf, lg)
        m2 = jnp.max(lg_rest, axis=0, keepdims=True)
        i2 = jnp.min(jnp.where(lg_rest == m2, row, float(N_EXPERTS)), axis=0, keepdims=True)
        t = jnp.exp(m2 - m1)
        w1 = 1.0 / (1.0 + t)
        @pl.when(pl.program_id(0) == 0)
        def _():
            count_ref[...] = jnp.zeros_like(count_ref)

        picked = (row == i1) | (row == i2)
        sel = jnp.where(picked, 1.0, 0.0)
        sel16 = jnp.concatenate([sel, jnp.zeros_like(sel)], axis=0).astype(BF16)
        rank = _dot(sel16, before_ref[...])[:N_EXPERTS] + count_ref[:, :1]
        count_ref[...] = count_ref[...] + jnp.sum(sel, axis=1, keepdims=True)
        pick = lambda i, a: jnp.sum(jnp.where(row == i, a, 0.0), axis=0, keepdims=True)
        route_ref[...] = jnp.concatenate(
            [i1, i2, pick(i1, rank), pick(i2, rank), w1, t * w1, jnp.zeros_like(w1), jnp.zeros_like(w1)],
            axis=0)


def _postmix(of, ob, proj, na, x, gog, wog, won, wout, gf, router, tm):
    n, d = x.shape
    with_router = router is not None
    in_specs = [
        pl.BlockSpec((tm, V_W), lambda i: (i, 0)),
        pl.BlockSpec((tm, V_W), lambda i: (i, 0)),
        pl.BlockSpec((tm, D_MODEL), lambda i: (i, 2)),
        pl.BlockSpec((tm, D_MODEL), lambda i: (i, 3)),
        pl.BlockSpec((tm, D_MODEL), lambda i: (i, 4)),
        pl.BlockSpec((tm, NA_W), lambda i: (i, 0)),
        pl.BlockSpec((tm, d), lambda i: (i, 0)),
        _resident(gog), _resident(wog), _resident(won), _resident(wout), _resident(gf),
    ]
    args = [of, ob, proj, proj, proj, na, x, gog, wog, won, wout, gf]
    h2_w = d // 2 if with_router else d
    out_specs = [pl.BlockSpec((tm, d), lambda i: (i, 0)), pl.BlockSpec((tm, h2_w), lambda i: (i, 0))]
    out_shape = [jax.ShapeDtypeStruct((n, d), F32),
                 jax.ShapeDtypeStruct((n, h2_w), jnp.int32 if with_router else BF16)]
    if with_router:
        before = jnp.asarray(np.triu(np.ones((tm, tm), np.float32), 1), BF16)
        in_specs += [_resident(router[0]), _resident(router[1]), _resident(before)]
        args += list(router) + [before]
        out_specs.append(pl.BlockSpec((N_EXPERTS, tm), lambda i: (0, i)))
        out_shape.append(jax.ShapeDtypeStruct((N_EXPERTS, n), F32))
        out_specs.append(pl.BlockSpec((N_EXPERTS, LANES), lambda i: (0, 0)))
        out_shape.append(jax.ShapeDtypeStruct((N_EXPERTS, LANES), F32))
    return pl.pallas_call(
        functools.partial(_postmix_kernel, with_router=with_router),
        grid=(n // tm,),
        in_specs=in_specs,
        out_specs=out_specs,
        out_shape=out_shape,
        compiler_params=_params(("arbitrary" if with_router else "parallel",)),
        name="postmix_router" if with_router else "postmix",
    )(*args)


def _ffn_kernel(x1_ref, h_ref, wg_ref, wu_ref, wd_ref, o_ref, acc_ref):
    j = pl.program_id(1)

    @pl.when(j == 0)
    def _():
        acc_ref[...] = jnp.zeros_like(acc_ref)

    h = h_ref[...]
    g = _dot(h, wg_ref[...])
    u = _dot(h, wu_ref[...])
    a = (g * _sigmoid(g) * u).astype(BF16)
    acc_ref[...] += _dot(a, wd_ref[...])

    @pl.when(j == pl.num_programs(1) - 1)
    def _():
        o_ref[...] = x1_ref[...] + acc_ref[...]


def _ffn(x1, h2, wg, wu, wd, tm, tf):
    n, d = x1.shape
    dff = wg.shape[1]
    return pl.pallas_call(
        _ffn_kernel,
        grid=(n // tm, dff // tf),
        in_specs=[
            pl.BlockSpec((tm, d), lambda i, j: (i, 0)),
            pl.BlockSpec((tm, d), lambda i, j: (i, 0)),
            pl.BlockSpec((d, tf), lambda i, j: (0, j)),
            pl.BlockSpec((d, tf), lambda i, j: (0, j)),
            pl.BlockSpec((tf, d), lambda i, j: (j, 0)),
        ],
        out_specs=pl.BlockSpec((tm, d), lambda i, j: (i, 0)),
        out_shape=jax.ShapeDtypeStruct((n, d), F32),
        scratch_shapes=[pltpu.VMEM((tm, d), F32)],
        compiler_params=_params(("parallel", "arbitrary")),
        name="ffn",
    )(x1, h2, wg, wu, wd)


SC_CORES = 2
SC_SUBCORES = 16
SC_WORKERS = SC_CORES * SC_SUBCORES
SC_CHUNK = 32
MOE_TM = 512


def _sc_gather_rows(table, idx):
    nrow, d = idx.shape[0], table.shape[1]
    assert nrow % (SC_WORKERS * SC_CHUNK) == 0
    per_worker = nrow // SC_WORKERS
    mesh = plsc.VectorSubcoreMesh(core_axis_name="c", subcore_axis_name="s",
                                  num_cores=SC_CORES, num_subcores=SC_SUBCORES)

    @functools.partial(
        pl.kernel, mesh=mesh, out_type=jax.ShapeDtypeStruct((nrow, d), table.dtype),
        scratch_types=[pltpu.VMEM((SC_CHUNK,), jnp.int32), pltpu.VMEM((SC_CHUNK, d), table.dtype),
                       pltpu.SemaphoreType.DMA])
    def gather(table_hbm, idx_hbm, out_hbm, idx_v, rows_v, sem):
        base = (lax.axis_index("s") * SC_CORES + lax.axis_index("c")) * per_worker

        @pl.loop(0, per_worker // SC_CHUNK)
        def _(j):
            off = base + j * SC_CHUNK
            pltpu.sync_copy(idx_hbm.at[pl.ds(off, SC_CHUNK)], idx_v)
            pltpu.async_copy(table_hbm.at[idx_v], rows_v, sem).wait()
            pltpu.sync_copy(rows_v, out_hbm.at[pl.ds(off, SC_CHUNK)])

    return gather(table, idx)


def _sc_scatter_rows(rows, pos12, n_out):
    n, d = rows.shape
    assert n % (SC_WORKERS * SC_CHUNK) == 0
    per_worker = n // SC_WORKERS
    idx = pos12.reshape(2, n)
    mesh = plsc.VectorSubcoreMesh(core_axis_name="c", subcore_axis_name="s",
                                  num_cores=SC_CORES, num_subcores=SC_SUBCORES)

    @functools.partial(
        pl.kernel, mesh=mesh, out_type=jax.ShapeDtypeStruct((n_out, d), rows.dtype),
        scratch_types=[pltpu.VMEM((2, SC_CHUNK), jnp.int32), pltpu.VMEM((SC_CHUNK, d), rows.dtype)])
    def scatter(rows_hbm, idx_hbm, out_hbm, idx_v, rows_v):
        base = (lax.axis_index("s") * SC_CORES + lax.axis_index("c")) * per_worker

        @pl.loop(0, per_worker // SC_CHUNK)
        def _(j):
            off = base + j * SC_CHUNK
            pltpu.sync_copy(rows_hbm.at[pl.ds(off, SC_CHUNK)], rows_v)
            for k in range(2):
                pltpu.sync_copy(idx_hbm.at[k, pl.ds(off, SC_CHUNK)], idx_v.at[k])
                pltpu.sync_copy(rows_v, out_hbm.at[idx_v.at[k]])

    return scatter(rows, idx)


def _moe_plan(route, counts):
    n = route.shape[1]
    count = counts[:, 0].astype(jnp.int32)
    group = (count + MOE_TM - 1) // MOE_TM * MOE_TM
    ends = jnp.cumsum(group)
    start = ends - group
    experts = jnp.arange(N_EXPERTS, dtype=jnp.int32)[:, None]

    def rows_of(e, rank):
        return jnp.sum(jnp.where(e.astype(jnp.int32)[None, :] == experts, start[:, None], 0), axis=0) \
            + rank.astype(jnp.int32)

    pos12 = jnp.concatenate([rows_of(route[0], route[2]), rows_of(route[1], route[3])])
    n_rows = 2 * n + N_EXPERTS * MOE_TM
    tile_start = jnp.arange(n_rows // MOE_TM, dtype=jnp.int32) * MOE_TM
    tile_expert = jnp.minimum(jnp.sum(tile_start[:, None] >= ends[None, :], axis=1), N_EXPERTS - 1)
    filled = (start + count)[tile_expert] - tile_start
    tile_rows = jnp.where(tile_start < ends[-1], jnp.clip(filled, 0, MOE_TM), 0).astype(jnp.int32)
    w12 = jnp.stack([route[4], route[5]], axis=1)
    return pos12, n_rows, tile_expert.astype(jnp.int32), tile_rows, w12


def _moe_group_kernel(te_ref, tr_ref, xs_ref, wg_ref, wu_ref, wd_ref, ys_ref):
    n_filled = tr_ref[pl.program_id(0)]
    used = n_filled > 0

    @pl.when(used)
    def _():
        row = lax.broadcasted_iota(jnp.int32, xs_ref.shape, 0)
        x = _unpack_bf16_pairs(jnp.where(row < n_filled, xs_ref[...], 0)).astype(BF16)
        g = _dot(x, wg_ref[0])
        u = _dot(x, wu_ref[0])
        ys_ref[...] = _pack_bf16_pairs(_dot((g * _sigmoid(g) * u).astype(BF16), wd_ref[0]))

    @pl.when(jnp.logical_not(used))
    def _():
        ys_ref[...] = jnp.zeros_like(ys_ref)


def _moe_grouped(xs, tile_expert, tile_rows, wg, wu, wd):
    p, dp = xs.shape
    _, d, dfe = wg.shape
    grid_spec = pltpu.PrefetchScalarGridSpec(
        num_scalar_prefetch=2,
        grid=(p // MOE_TM,),
        in_specs=[
            pl.BlockSpec((MOE_TM, dp), lambda t, te, tr: (t, 0)),
            pl.BlockSpec((1, d, dfe), lambda t, te, tr: (te[t], 0, 0)),
            pl.BlockSpec((1, d, dfe), lambda t, te, tr: (te[t], 0, 0)),
            pl.BlockSpec((1, dfe, d), lambda t, te, tr: (te[t], 0, 0)),
        ],
        out_specs=pl.BlockSpec((MOE_TM, dp), lambda t, te, tr: (t, 0)),
    )
    return pl.pallas_call(
        _moe_group_kernel,
        grid_spec=grid_spec,
        out_shape=jax.ShapeDtypeStruct((p, dp), jnp.int32),
        compiler_params=_params(("arbitrary",)),
        name="moe_grouped",
    )(tile_expert, tile_rows, xs, wg, wu, wd)


def _moe_combine_kernel(x1_ref, y1_ref, y2_ref, w_ref, o_ref):
    w = w_ref[...]
    y1 = _unpack_bf16_pairs(y1_ref[...])
    y2 = _unpack_bf16_pairs(y2_ref[...])
    o_ref[...] = x1_ref[...] + w[:, 0:1] * y1 + w[:, 1:2] * y2


def _moe_combine(x1, y, w12, tm):
    n, d = x1.shape
    nt = n // tm
    return pl.pallas_call(
        _moe_combine_kernel,
        grid=(nt,),
        in_specs=[
            pl.BlockSpec((tm, d), lambda i: (i, 0)),
            pl.BlockSpec((tm, d // 2), lambda i: (i, 0)),
            pl.BlockSpec((tm, d // 2), lambda i: (i + nt, 0)),
            pl.BlockSpec((tm, 2), lambda i: (i, 0)),
        ],
        out_specs=pl.BlockSpec((tm, d), lambda i: (i, 0)),
        out_shape=jax.ShapeDtypeStruct((n, d), F32),
        compiler_params=_params(("parallel",)),
        name="moe_combine",
    )(x1, y, y, w12)


def _moe(x1, h2, route, counts, wg, wu, wd):
    pos12, n_rows, tile_expert, tile_rows, w12 = _moe_plan(route, counts)
    xs = _sc_scatter_rows(h2, pos12, n_rows)
    ys = _moe_grouped(xs, tile_expert, tile_rows, wg, wu, wd)
    y = _sc_gather_rows(ys, pos12)
    return _moe_combine(x1, y, w12, _row_tile(x1.shape[0], TM_COMBINE))


def _row_tile(n, want):
    t = min(n, want)
    assert n % t == 0
    return t


def _prep_layer(l, norm_mix, w_in, w_decay_f, b_decay_f, w_decay_b, b_decay_b, gla_out_norm, q_norm,
                k_norm, rpb, w_o_gla, w_o_na, w_out, norm_ffn):
    w = w_in[l]
    c0 = 2 * QK_W + 2 * V_W
    c1 = c0 + 2 * GLA_RANK
    c2 = c1 + 3 * NA_W
    w_main = jnp.concatenate([w[:, :QK_W] * (GLA_DK ** -0.5), w[:, QK_W:c0], w[:, c2:], w[:, c1:c2]],
                             axis=1).astype(BF16)
    wz = w[:, c0:c1].astype(BF16)
    zero = jnp.zeros((GLA_RANK, QK_W), F32)
    wdec = jnp.concatenate([jnp.concatenate([w_decay_f[l], zero], axis=1),
                            jnp.concatenate([zero, w_decay_b[l]], axis=1)], axis=0)
    bdec = jnp.concatenate([b_decay_f[l], b_decay_b[l]])[None, :]
    qg = jnp.tile(q_norm[l] * (NA_DH ** -0.5), NA_HEADS)[None, :]
    kg = jnp.tile(k_norm[l], NA_HEADS)[None, :]
    return dict(
        g_mix=norm_mix[l][None, :], w_main=w_main, wz=wz, wd=wdec.astype(BF16), bdec=bdec,
        qg=qg, kg=kg, bias=_na_bias_table(rpb[l]),
        gog=gla_out_norm[l].reshape(1, V_W), wog=w_o_gla[l].astype(BF16),
        won=w_o_na[l].astype(BF16), wout=w_out[l].astype(BF16), g_ffn=norm_ffn[l][None, :])


def _router_operands(router):
    hi, lo = _split_bf16(router.T)
    return jnp.concatenate([hi, lo], axis=0), jnp.concatenate([hi, jnp.zeros_like(hi)], axis=0)


def _trunk(x, layers, dense, moe):
    b, l, d = x.shape
    n = b * l
    xf = x.reshape(n, d)
    head_mean = jnp.asarray(np.kron(np.eye(NA_HEADS), np.full((NA_DH, NA_DH), 1.0 / NA_DH)), BF16)
    tb = _row_tile(l, GLA_BLOCK)
    sb = GLA_SEQS if b % GLA_SEQS == 0 else 1
    for li, p in enumerate(layers):
        proj, la, lamin = _inproj(xf, p["g_mix"], p["w_main"], p["wz"], p["wd"], p["bdec"],
                           head_mean, p["qg"], p["kg"], _row_tile(n, TM_INPROJ))
        proj3 = proj.reshape(b, l, PROJ_W)
        la3 = la.reshape(b, l, 2 * QK_W)
        of = _gla(proj3, la3, lamin, False, tb, sb).reshape(n, V_W)
        ob = _gla(proj3, la3, lamin, True, tb, sb).reshape(n, V_W)
        na = _natten(proj3, p["bias"]).reshape(n, NA_W)
        if li % 2 == 0:
            wg, wu, wd = dense[li // 2]
            x1, h2 = _postmix(of, ob, proj, na, xf, p["gog"], p["wog"], p["won"], p["wout"],
                              p["g_ffn"], None, _row_tile(n, TM_POSTMIX))
            xf = _ffn(x1, h2, wg, wu, wd, _row_tile(n, TM_FFN), wg.shape[1] // FFN_SPLIT)
        else:
            router, wg, wu, wd = moe[li // 2]
            x1, h2, route, counts = _postmix(of, ob, proj, na, xf, p["gog"], p["wog"], p["won"], p["wout"],
                                             p["g_ffn"], router, _row_tile(n, TM_POSTMIX))
            xf = _moe(x1, h2, route, counts, wg, wu, wd)
    return xf.reshape(b, l, d)


def kernel(x_prompt, x_sample, norm_mix, w_in, w_decay_f, b_decay_f, w_decay_b, b_decay_b, gla_out_norm,
           q_norm, k_norm, rpb, w_o_gla, w_o_na, w_out, norm_ffn, ffn_w_gate, ffn_w_up, ffn_w_down,
           moe_router, moe_w_gate, moe_w_up, moe_w_down):
    depth = w_in.shape[0]
    layers = [_prep_layer(l, norm_mix, w_in, w_decay_f, b_decay_f, w_decay_b, b_decay_b, gla_out_norm,
                          q_norm, k_norm, rpb, w_o_gla, w_o_na, w_out, norm_ffn) for l in range(depth)]
    dense = [(ffn_w_gate[j].astype(BF16), ffn_w_up[j].astype(BF16), ffn_w_down[j].astype(BF16))
             for j in range(ffn_w_gate.shape[0])]
    moe = [(_router_operands(moe_router[j]), moe_w_gate[j].astype(BF16), moe_w_up[j].astype(BF16),
            moe_w_down[j].astype(BF16)) for j in range(moe_router.shape[0])]
    return (_trunk(x_prompt, layers, dense, moe), _trunk(x_sample, layers, dense, moe))
```

```python
import functools

import numpy as np
import jax
import jax.numpy as jnp
from jax import lax
from jax.experimental import pallas as pl
from jax.experimental.pallas import tpu as pltpu
from jax.experimental.pallas import tpu_sc as plsc

F32 = jnp.float32
BF16 = jnp.bfloat16

EPS = 1e-6
GRID_W = 64
GLA_HEADS = 4
GLA_DK = 128
GLA_DV = 256
GLA_CHUNK = 64
GLA_RANK = 16
GLA_TAU = 16.0
NA_HEADS = 8
NA_DH = 64
NA_KR = 8
NA_KC = 16
N_EXPERTS = 8
NEG = -1e30
LOG2_E = 1.4426950408889634

VMEM_LIMIT = 56 * 1024 * 1024
SUBLANES = 8
LANES = 128
D_MODEL = 1024

TM_INPROJ = 512
TM_POSTMIX = 512
TM_FFN = 1024
FFN_SPLIT = 2
TM_COMBINE = 1024
GLA_BLOCK = 512
GLA_SEQS = 2

QK_W = GLA_HEADS * GLA_DK
V_W = GLA_HEADS * GLA_DV
NA_W = NA_HEADS * NA_DH
PROJ_W = 2 * QK_W + 2 * V_W + 2 * D_MODEL + 3 * NA_W
CW = 512
NQ_CHUNK = (2 * QK_W + 2 * V_W + 2 * D_MODEL) // CW
NK_CHUNK = NQ_CHUNK + 1
INPROJ_SUB = 256


def _params(sem):
    return pltpu.CompilerParams(dimension_semantics=sem, vmem_limit_bytes=VMEM_LIMIT)


def _resident(a):
    nd = a.ndim
    return pl.BlockSpec(a.shape, lambda *_: (0,) * nd, pipeline_mode=pl.Buffered(1))


def _split_bf16(a):
    hi = a.astype(BF16)
    lo = (a - hi.astype(F32)).astype(BF16)
    return hi, lo


def _dot(a, b):
    return jnp.dot(a, b, preferred_element_type=F32)


def _dot_nt(a, b):
    return lax.dot_general(a, b, (((1,), (1,)), ((), ())), preferred_element_type=F32)


def _dot_tn(a, b):
    return lax.dot_general(a, b, (((0,), (0,)), ((), ())), preferred_element_type=F32)


def _sigmoid(x):
    return 1.0 / (1.0 + jnp.exp2(x * (-LOG2_E)))


def _pack_bf16_pairs(x):
    c = x.shape[1] // 2
    as_bits = lambda a: lax.bitcast_convert_type(a.astype(BF16).astype(F32), jnp.uint32)
    word = (as_bits(x[:, :c]) >> 16) | (as_bits(x[:, c:]) & jnp.uint32(0xFFFF0000))
    return lax.bitcast_convert_type(word, jnp.int32)


def _unpack_bf16_pairs(w):
    u = lax.bitcast_convert_type(w, jnp.uint32)
    lo = lax.bitcast_convert_type(u << 16, F32)
    hi = lax.bitcast_convert_type(u & jnp.uint32(0xFFFF0000), F32)
    return jnp.concatenate([lo, hi], axis=1)


def _rmsnorm(x, g):
    ms = jnp.mean(x * x, axis=-1, keepdims=True)
    return x * lax.rsqrt(ms + EPS) * g


def _inproj_kernel(x_ref, g_ref, w_ref, wz_ref, wd_ref, bdec_ref, hm_ref, qg_ref, kg_ref,
                   proj_ref, la_ref, lamin_ref):
    for t in range(x_ref.shape[0] // INPROJ_SUB):
        rs = slice(t * INPROJ_SUB, (t + 1) * INPROJ_SUB)
        h = _rmsnorm(x_ref[rs, :], g_ref[...]).astype(BF16)
        z = _dot(h, wz_ref[...])
        y = _dot(z.astype(BF16), wd_ref[...]) + bdec_ref[...]
        ls = jnp.minimum(y, 0.0) - jnp.log(1.0 + jnp.exp(-jnp.abs(y)))
        la = ls * (1.0 / GLA_TAU)
        la_ref[rs, :] = la
        lamin_ref[t] = jnp.broadcast_to(jnp.min(la, axis=0, keepdims=True), lamin_ref.shape[1:])
        for c in range(PROJ_W // CW):
            acc = _dot(h, w_ref[:, c * CW:(c + 1) * CW])
            if c in (NQ_CHUNK, NK_CHUNK):
                gain = qg_ref if c == NQ_CHUNK else kg_ref
                ms = _dot((acc * acc).astype(BF16), hm_ref[...])
                acc = acc * lax.rsqrt(ms + EPS) * gain[...]
            proj_ref[rs, c * CW:(c + 1) * CW] = acc.astype(BF16)


def _inproj(x, g, w_main, wz, wd, bdec, hm, qg, kg, tm):
    n, d = x.shape
    return pl.pallas_call(
        _inproj_kernel,
        grid=(n // tm,),
        in_specs=[
            pl.BlockSpec((tm, d), lambda i: (i, 0)),
            _resident(g), _resident(w_main), _resident(wz), _resident(wd),
            _resident(bdec), _resident(hm), _resident(qg), _resident(kg),
        ],
        out_specs=[
            pl.BlockSpec((tm, PROJ_W), lambda i: (i, 0)),
            pl.BlockSpec((tm, 2 * QK_W), lambda i: (i, 0)),
            pl.BlockSpec((tm // INPROJ_SUB, SUBLANES, 2 * QK_W), lambda i: (i, 0, 0)),
        ],
        out_shape=[
            jax.ShapeDtypeStruct((n, PROJ_W), BF16),
            jax.ShapeDtypeStruct((n, 2 * QK_W), F32),
            jax.ShapeDtypeStruct((n // INPROJ_SUB, SUBLANES, 2 * QK_W), F32),
        ],
        compiler_params=_params(("parallel",)),
        name="inproj",
    )(x, g, w_main, wz, wd, bdec, hm, qg, kg)


N_LEVELS = 6
INTER_BLK = N_LEVELS
STATE_BLK = N_LEVELS + 1


def _decay_sum_matrix(reverse):
    c = GLA_CHUNK
    m_all = np.zeros((N_LEVELS + 2, c, c), np.float32)
    for l in range(N_LEVELS):
        m = 1 << l
        for p in range(c):
            mid = (p // (2 * m)) * 2 * m + m
            if p >= mid:
                m_all[l, p, mid + 1:p + 1] = 1.0
            else:
                m_all[l, p, p + 1:mid + 1] = 1.0
    for p in range(c):
        m_all[INTER_BLK, p, :p + 1] = 1.0
        m_all[STATE_BLK, p, p + 1:] = 1.0
    if reverse:
        m_all = m_all[:, ::-1, ::-1]
    return np.ascontiguousarray(m_all).reshape((N_LEVELS + 2) * c, c)


def _level_matrix(reverse):
    i = np.arange(GLA_CHUNK)[:, None]
    j = np.arange(GLA_CHUNK)[None, :]
    x = i ^ j
    lvl = np.where(x > 0, np.floor(np.log2(np.maximum(x, 1))), N_LEVELS).astype(np.int32)
    valid = (i <= j) if reverse else (i >= j)
    return np.where(valid, lvl, N_LEVELS + 1).astype(np.int32)


GLA_FAST_CHUNK = 128
GLA_FAST_MAX_DECAY = 60.0


def _tri_matrix(reverse, c):
    i = np.arange(c)[:, None]
    j = np.arange(c)[None, :]
    return ((j >= i) if reverse else (j <= i)).astype(np.float32)


def _decay_columns(dec_row):
    t = jnp.broadcast_to(dec_row, (GLA_DK, GLA_DK)).T
    return jnp.concatenate([t, t], axis=1)


def _gla_kernel(q_ref, k_ref, v_ref, la_ref, lamin_ref, m_ref, lvl_ref, tri_ref, o_ref,
                s_ref, lr_ref, qh_ref, kh_ref, ks_ref, dec_ref, *, reverse, tb, sb):
    @pl.when(pl.program_id(1) == 0)
    def _():
        s_ref[...] = jnp.zeros_like(s_ref)

    fc = GLA_FAST_CHUNK
    n_fast = tb // fc
    total_row = 0 if reverse else fc - 1
    seqs = range(sb)

    def fast_rows(ci):
        return pl.ds(pl.multiple_of(ci * fc, fc), fc)

    def sc_lanes(h):
        return slice(h * 2 * GLA_DK, h * 2 * GLA_DK + GLA_DK)

    def qh_lanes(h):
        return slice(h * 2 * GLA_DK + GLA_DK, (h + 1) * 2 * GLA_DK)

    def fast_factors(ci, carry):
        rows = fast_rows(ci)
        for s in seqs:
            b = _dot(tri_ref[...], la_ref[s, rows, :].astype(BF16))
            dec = jnp.exp(b[total_row:total_row + 1, :])
            f_q = jnp.exp(b)
            qh = (q_ref[s, rows, :].astype(F32) * f_q).astype(BF16)
            kh = k_ref[s, rows, :].astype(F32) / f_q
            qh_ref[s, rows, :] = qh
            for h in range(GLA_HEADS):
                lr_ref[s, rows, qh_lanes(h)] = qh[:, h * GLA_DK:(h + 1) * GLA_DK]
            kh_ref[s, rows, :] = kh.astype(BF16)
            ks_ref[s, rows, :] = (kh * dec).astype(BF16)
            dec_ref[s, pl.ds(pl.multiple_of(ci * SUBLANES, SUBLANES), SUBLANES), :] = (
                jnp.broadcast_to(dec, (SUBLANES, QK_W)))
        return carry

    def fast_scores(ci, carry):
        rows = fast_rows(ci)
        i = lax.broadcasted_iota(jnp.int32, (fc, fc), 0)
        j = lax.broadcasted_iota(jnp.int32, (fc, fc), 1)
        causal = (i <= j) if reverse else (i >= j)
        for s in seqs:
            for h in range(GLA_HEADS):
                hs = slice(h * GLA_DK, (h + 1) * GLA_DK)
                sc = _dot_nt(qh_ref[s, rows, hs], kh_ref[s, rows, hs])
                lr_ref[s, rows, sc_lanes(h)] = jnp.where(causal, sc, 0.0).astype(BF16)
        return carry

    def fast_state(ci, carry):
        cc = (n_fast - 1 - ci) if reverse else ci
        rows = fast_rows(cc)
        heads = [(s, h) for s in seqs for h in range(GLA_HEADS)]
        vals = {(s, h): v_ref[s, rows, h * GLA_DV:(h + 1) * GLA_DV] for s, h in heads}
        kv = {(s, h): _dot_tn(ks_ref[s, rows, h * GLA_DK:(h + 1) * GLA_DK], vals[s, h]) for s, h in heads}
        for s, h in heads:
            dec = dec_ref[s, pl.ds(pl.multiple_of(cc * SUBLANES, SUBLANES), SUBLANES), :][
                :1, h * GLA_DK:(h + 1) * GLA_DK]
            st = s_ref[s, h]
            o = _dot(lr_ref[s, rows, h * 2 * GLA_DK:(h + 1) * 2 * GLA_DK],
                     jnp.concatenate([vals[s, h], st.astype(BF16)], axis=0))
            o_ref[s, rows, h * GLA_DV:(h + 1) * GLA_DV] = o.astype(o_ref.dtype)
            s_ref[s, h] = st * _decay_columns(dec) + kv[s, h]
        return carry

    def robust_chunk(ci, carry):
        c = GLA_CHUNK
        n_chunks = tb // c
        lvl = lvl_ref[...]
        row = lax.broadcasted_iota(jnp.int32, (c, GLA_DK), 0)
        total_row = 0 if reverse else c - 1
        cc = (n_chunks - 1 - ci) if reverse else ci
        rows = pl.ds(pl.multiple_of(cc * c, c), c)
        for s in seqs:
            la_hi, la_lo = _split_bf16(la_ref[s, rows, :])
            fac = jnp.exp(_dot(m_ref[...], la_hi) + _dot(m_ref[...], la_lo))
            for h in range(GLA_HEADS):
                hs = slice(h * GLA_DK, (h + 1) * GLA_DK)
                vs = slice(h * GLA_DV, (h + 1) * GLA_DV)
                q = q_ref[s, rows, hs].astype(F32)
                k = k_ref[s, rows, hs].astype(F32)
                v = v_ref[s, rows, vs]
                scores = jnp.zeros((c, c), F32)
                for l in range(N_LEVELS):
                    is_query = ((row >> l) & 1) == (0 if reverse else 1)
                    f_l = fac[l * c:(l + 1) * c, hs]
                    g_l = (jnp.where(is_query, q, k) * f_l).astype(BF16)
                    scores = jnp.where(lvl == l, _dot_nt(g_l, g_l), scores)
                scores = jnp.where(lvl == N_LEVELS, _dot_nt(q.astype(BF16), k.astype(BF16)), scores)
                f_in = fac[INTER_BLK * c:(INTER_BLK + 1) * c, hs]
                f_st = fac[STATE_BLK * c:(STATE_BLK + 1) * c, hs]
                st = s_ref[s, h]
                inter = _dot((q * f_in).astype(BF16), st.astype(BF16))
                intra = _dot(scores.astype(BF16), v)
                o_ref[s, rows, vs] = (inter + intra).astype(o_ref.dtype)
                dec = f_in[total_row:total_row + 1, :]
                s_ref[s, h] = st * _decay_columns(dec) + _dot_tn((k * f_st).astype(BF16), v)
        return carry

    bounded = jnp.min(lamin_ref[...]) >= -GLA_FAST_MAX_DECAY / GLA_FAST_CHUNK

    @pl.when(bounded)
    def _():
        lax.fori_loop(0, n_fast, fast_factors, 0, unroll=4)
        lax.fori_loop(0, n_fast, fast_scores, 0, unroll=4)
        lax.fori_loop(0, n_fast, fast_state, 0, unroll=4)

    @pl.when(jnp.logical_not(bounded))
    def _():
        lax.fori_loop(0, tb // GLA_CHUNK, robust_chunk, 0)


def _gla(proj3, la3, lamin, reverse, tb, sb):
    b, l, _ = proj3.shape
    nb = l // tb
    lamin4 = lamin.reshape(b, -1, SUBLANES, 2 * QK_W)
    tiles = lamin4.shape[1] // nb
    blk = (lambda i: nb - 1 - i) if reverse else (lambda i: i)
    m_all = jnp.asarray(_decay_sum_matrix(reverse), BF16)
    lvl = jnp.asarray(_level_matrix(reverse))
    tri = jnp.asarray(_tri_matrix(reverse, GLA_FAST_CHUNK), BF16)
    kern = functools.partial(_gla_kernel, reverse=reverse, tb=tb, sb=sb)
    decay_col = 1 if reverse else 0
    return pl.pallas_call(
        kern,
        grid=(b // sb, nb),
        in_specs=[
            pl.BlockSpec((sb, tb, QK_W), lambda s, i: (s, blk(i), 0)),
            pl.BlockSpec((sb, tb, QK_W), lambda s, i: (s, blk(i), 1)),
            pl.BlockSpec((sb, tb, V_W), lambda s, i: (s, blk(i), 1)),
            pl.BlockSpec((sb, tb, QK_W), lambda s, i: (s, blk(i), decay_col)),
            pl.BlockSpec((sb, tiles, SUBLANES, QK_W), lambda s, i: (s, blk(i), 0, decay_col)),
            _resident(m_all), _resident(lvl), _resident(tri),
        ],
        out_specs=pl.BlockSpec((sb, tb, V_W), lambda s, i: (s, blk(i), 0)),
        out_shape=jax.ShapeDtypeStruct((b, l, V_W), BF16),
        scratch_shapes=[
            pltpu.VMEM((sb, GLA_HEADS, GLA_DK, GLA_DV), F32),
            pltpu.VMEM((sb, tb, 2 * QK_W), BF16),
            pltpu.VMEM((sb, tb, QK_W), BF16),
            pltpu.VMEM((sb, tb, QK_W), BF16),
            pltpu.VMEM((sb, tb, QK_W), BF16),
            pltpu.VMEM((sb, tb // GLA_FAST_CHUNK * SUBLANES, QK_W), F32),
        ],
        compiler_params=_params(("parallel", "arbitrary")),
        name="gla_bwd" if reverse else "gla_fwd",
    )(proj3, proj3, proj3, la3, lamin4, m_all, lvl, tri)


NA_GROUP = 8
NA_BLK = NA_GROUP * GRID_W
NA_BAND = NA_KR * GRID_W


def _na_bias_table(rpb):
    qc = np.arange(GRID_W)[:, None]
    kc = np.arange(GRID_W)[None, :]
    col_start = np.clip(qc - NA_KC // 2, 0, GRID_W - NA_KC)
    valid = (kc >= col_start) & (kc < col_start + NA_KC)
    dc = np.clip(kc - qc, -(NA_KC - 1), NA_KC - 1) + NA_KC - 1
    onehot = ((dc[None] == np.arange(2 * NA_KC - 1)[:, None, None]) & valid[None]).astype(np.float32)
    mask = np.where(valid, 0.0, NEG).astype(np.float32)
    rows = jnp.stack([rpb[:, NA_KR - 1 - d:2 * NA_KR - 1 - d, :] for d in range(NA_KR)])
    t = jnp.einsum('dhkc,cqj->dhqkj', rows, jnp.asarray(onehot), precision=lax.Precision.HIGHEST)
    t = t + jnp.asarray(mask)[None, None, :, None, :]
    return t.reshape(NA_KR, NA_HEADS // 2, 2 * GRID_W, NA_BAND)


def _na_kernel(q_ref, kp_ref, kc_ref, kn_ref, vp_ref, vc_ref, vn_ref, bias_ref, o_ref,
               kwin_ref, vwin_ref, *, rows):
    g = pl.program_id(1)
    for j, (kr, vr) in enumerate(((kp_ref, vp_ref), (kc_ref, vc_ref), (kn_ref, vn_ref))):
        kwin_ref[j * NA_BLK:(j + 1) * NA_BLK, :] = kr[0]
        vwin_ref[j * NA_BLK:(j + 1) * NA_BLK, :] = vr[0]
    lane = lax.broadcasted_iota(jnp.int32, (GRID_W, 2 * NA_DH), 1)
    first = lane < NA_DH

    pairs = [slice(p * 2 * NA_DH, (p + 1) * 2 * NA_DH) for p in range(NA_HEADS // 2)]

    def one_row(rl, carry):
        r = g * NA_GROUP + rl
        rs = jnp.clip(r - NA_KR // 2, 0, rows - NA_KR)
        d = r - rs
        off = pl.multiple_of((rs - (g - 1) * NA_GROUP) * GRID_W, GRID_W)
        qrows = pl.ds(pl.multiple_of(rl * GRID_W, GRID_W), GRID_W)
        scores = []
        for p, ps in enumerate(pairs):
            qp = q_ref[0, qrows, ps]
            zero = jnp.zeros_like(qp)
            qs = jnp.concatenate([jnp.where(first, qp, zero), jnp.where(first, zero, qp)], axis=0)
            kb = kwin_ref[pl.ds(off, NA_BAND), ps]
            scores.append(_dot_nt(qs, kb) + bias_ref[d, p])
        probs = []
        for s in scores:
            e = jnp.exp(s - jnp.max(s, axis=-1, keepdims=True))
            probs.append((e.astype(BF16), jnp.sum(e, axis=-1, keepdims=True)))
        for (e, den), ps in zip(probs, pairs):
            vb = vwin_ref[pl.ds(off, NA_BAND), ps]
            pv = _dot(e, vb) / den
            o_ref[0, qrows, ps] = jnp.where(first, pv[:GRID_W], pv[GRID_W:]).astype(o_ref.dtype)
        return carry

    lax.fori_loop(0, NA_GROUP, one_row, 0, unroll=NA_GROUP)


def _natten(proj3, bias):
    b, l, _ = proj3.shape
    rows = l // GRID_W
    ng = rows // NA_GROUP
    qcol, kcol, vcol = NQ_CHUNK, NQ_CHUNK + 1, NQ_CHUNK + 2
    prev = lambda i: jnp.maximum(i - 1, 0)
    nxt = lambda i: jnp.minimum(i + 1, ng - 1)
    spec = lambda f, col: pl.BlockSpec((1, NA_BLK, NA_W), lambda s, i: (s, f(i), col))
    same = lambda i: i
    return pl.pallas_call(
        functools.partial(_na_kernel, rows=rows),
        grid=(b, ng),
        in_specs=[
            spec(same, qcol),
            spec(prev, kcol), spec(same, kcol), spec(nxt, kcol),
            spec(prev, vcol), spec(same, vcol), spec(nxt, vcol),
            _resident(bias),
        ],
        out_specs=pl.BlockSpec((1, NA_BLK, NA_W), lambda s, i: (s, i, 0)),
        out_shape=jax.ShapeDtypeStruct((b, l, NA_W), BF16),
        scratch_shapes=[pltpu.VMEM((3 * NA_BLK, NA_W), BF16), pltpu.VMEM((3 * NA_BLK, NA_W), BF16)],
        compiler_params=_params(("parallel", "parallel")),
        name="natten",
    )(proj3, proj3, proj3, proj3, proj3, proj3, proj3, bias)


def _postmix_kernel(*refs, with_router):
    if with_router:
        (of_ref, ob_ref, r_ref, ga_ref, gb_ref, na_ref, x_ref, gog_ref, wog_ref, won_ref, wout_ref,
         gf_ref, rh_ref, rl_ref, before_ref, x1_ref, h2_ref, route_ref, count_ref) = refs
    else:
        (of_ref, ob_ref, r_ref, ga_ref, gb_ref, na_ref, x_ref, gog_ref, wog_ref, won_ref, wout_ref,
         gf_ref, x1_ref, h2_ref) = refs
    o = of_ref[...].astype(F32) + ob_ref[...].astype(F32)
    parts = []
    for h in range(GLA_HEADS):
        seg = o[:, h * GLA_DV:(h + 1) * GLA_DV]
        ms = jnp.mean(seg * seg, axis=-1, keepdims=True)
        parts.append(seg * lax.rsqrt(ms + EPS))
    r = r_ref[...].astype(F32)
    on = jnp.concatenate(parts, axis=-1) * gog_ref[...] * (r * _sigmoid(r))
    ya = _dot(on.astype(BF16), wog_ref[...])
    yb = _dot(na_ref[...], won_ref[...])
    merged = _sigmoid(ga_ref[...].astype(F32)) * ya + _sigmoid(gb_ref[...].astype(F32)) * yb
    x1 = x_ref[...] + _dot(merged.astype(BF16), wout_ref[...])
    x1_ref[...] = x1
    h2 = _rmsnorm(x1, gf_ref[...])
    h2_ref[...] = _pack_bf16_pairs(h2) if with_router else h2.astype(BF16)
    if with_router:
        hh, hl = _split_bf16(h2)
        lg2 = _dot_nt(rh_ref[...], hh) + _dot_nt(rl_ref[...], hl)
        lg = lg2[:N_EXPERTS] + lg2[N_EXPERTS:]
        row = lax.broadcasted_iota(jnp.int32, lg.shape, 0).astype(F32)
        m1 = jnp.max(lg, axis=0, keepdims=True)
        i1 = jnp.min(jnp.where(lg == m1, row, float(N_EXPERTS)), axis=0, keepdims=True)
        lg_rest = jnp.where(row == i1, -jnp.inf, lg)
        m2 = jnp.max(lg_rest, axis=0, keepdims=True)
        i2 = jnp.min(jnp.where(lg_rest == m2, row, float(N_EXPERTS)), axis=0, keepdims=True)
        t = jnp.exp(m2 - m1)
        w1 = 1.0 / (1.0 + t)
        @pl.when(pl.program_id(0) == 0)
        def _():
            count_ref[...] = jnp.zeros_like(count_ref)

        picked = (row == i1) | (row == i2)
        sel = jnp.where(picked, 1.0, 0.0)
        sel16 = jnp.concatenate([sel, jnp.zeros_like(sel)], axis=0).astype(BF16)
        rank = _dot(sel16, before_ref[...])[:N_EXPERTS] + count_ref[:, :1]
        count_ref[...] = count_ref[...] + jnp.sum(sel, axis=1, keepdims=True)
        pick = lambda i, a: jnp.sum(jnp.where(row == i, a, 0.0), axis=0, keepdims=True)
        route_ref[...] = jnp.concatenate(
            [i1, i2, pick(i1, rank), pick(i2, rank), w1, t * w1, jnp.zeros_like(w1), jnp.zeros_like(w1)],
            axis=0)


def _postmix(of, ob, proj, na, x, gog, wog, won, wout, gf, router, tm):
    n, d = x.shape
    with_router = router is not None
    in_specs = [
        pl.BlockSpec((tm, V_W), lambda i: (i, 0)),
        pl.BlockSpec((tm, V_W), lambda i: (i, 0)),
        pl.BlockSpec((tm, D_MODEL), lambda i: (i, 2)),
        pl.BlockSpec((tm, D_MODEL), lambda i: (i, 3)),
        pl.BlockSpec((tm, D_MODEL), lambda i: (i, 4)),
        pl.BlockSpec((tm, NA_W), lambda i: (i, 0)),
        pl.BlockSpec((tm, d), lambda i: (i, 0)),
        _resident(gog), _resident(wog), _resident(won), _resident(wout), _resident(gf),
    ]
    args = [of, ob, proj, proj, proj, na, x, gog, wog, won, wout, gf]
    h2_w = d // 2 if with_router else d
    out_specs = [pl.BlockSpec((tm, d), lambda i: (i, 0)), pl.BlockSpec((tm, h2_w), lambda i: (i, 0))]
    out_shape = [jax.ShapeDtypeStruct((n, d), F32),
                 jax.ShapeDtypeStruct((n, h2_w), jnp.int32 if with_router else BF16)]
    if with_router:
        before = jnp.asarray(np.triu(np.ones((tm, tm), np.float32), 1), BF16)
        in_specs += [_resident(router[0]), _resident(router[1]), _resident(before)]
        args += list(router) + [before]
        out_specs.append(pl.BlockSpec((N_EXPERTS, tm), lambda i: (0, i)))
        out_shape.append(jax.ShapeDtypeStruct((N_EXPERTS, n), F32))
        out_specs.append(pl.BlockSpec((N_EXPERTS, LANES), lambda i: (0, 0)))
        out_shape.append(jax.ShapeDtypeStruct((N_EXPERTS, LANES), F32))
    return pl.pallas_call(
        functools.partial(_postmix_kernel, with_router=with_router),
        grid=(n // tm,),
        in_specs=in_specs,
        out_specs=out_specs,
        out_shape=out_shape,
        compiler_params=_params(("arbitrary" if with_router else "parallel",)),
        name="postmix_router" if with_router else "postmix",
    )(*args)


def _ffn_kernel(x1_ref, h_ref, wg_ref, wu_ref, wd_ref, o_ref, acc_ref):
    j = pl.program_id(1)

    @pl.when(j == 0)
    def _():
        acc_ref[...] = jnp.zeros_like(acc_ref)

    h = h_ref[...]
    g = _dot(h, wg_ref[...])
    u = _dot(h, wu_ref[...])
    a = (g * _sigmoid(g) * u).astype(BF16)
    acc_ref[...] += _dot(a, wd_ref[...])

    @pl.when(j == pl.num_programs(1) - 1)
    def _():
        o_ref[...] = x1_ref[...] + acc_ref[...]


def _ffn(x1, h2, wg, wu, wd, tm, tf):
    n, d = x1.shape
    dff = wg.shape[1]
    return pl.pallas_call(
        _ffn_kernel,
        grid=(n // tm, dff // tf),
        in_specs=[
            pl.BlockSpec((tm, d), lambda i, j: (i, 0)),
            pl.BlockSpec((tm, d), lambda i, j: (i, 0)),
            pl.BlockSpec((d, tf), lambda i, j: (0, j)),
            pl.BlockSpec((d, tf), lambda i, j: (0, j)),
            pl.BlockSpec((tf, d), lambda i, j: (j, 0)),
        ],
        out_specs=pl.BlockSpec((tm, d), lambda i, j: (i, 0)),
        out_shape=jax.ShapeDtypeStruct((n, d), F32),
        scratch_shapes=[pltpu.VMEM((tm, d), F32)],
        compiler_params=_params(("parallel", "arbitrary")),
        name="ffn",
    )(x1, h2, wg, wu, wd)


SC_CORES = 2
SC_SUBCORES = 16
SC_WORKERS = SC_CORES * SC_SUBCORES
SC_CHUNK = 32
MOE_TM = 512


def _sc_gather_rows(table, idx):
    nrow, d = idx.shape[0], table.shape[1]
    assert nrow % (SC_WORKERS * SC_CHUNK) == 0
    per_worker = nrow // SC_WORKERS
    mesh = plsc.VectorSubcoreMesh(core_axis_name="c", subcore_axis_name="s",
                                  num_cores=SC_CORES, num_subcores=SC_SUBCORES)

    @functools.partial(
        pl.kernel, mesh=mesh, out_type=jax.ShapeDtypeStruct((nrow, d), table.dtype),
        scratch_types=[pltpu.VMEM((SC_CHUNK,), jnp.int32), pltpu.VMEM((SC_CHUNK, d), table.dtype),
                       pltpu.SemaphoreType.DMA])
    def gather(table_hbm, idx_hbm, out_hbm, idx_v, rows_v, sem):
        base = (lax.axis_index("s") * SC_CORES + lax.axis_index("c")) * per_worker

        @pl.loop(0, per_worker // SC_CHUNK)
        def _(j):
            off = base + j * SC_CHUNK
            pltpu.sync_copy(idx_hbm.at[pl.ds(off, SC_CHUNK)], idx_v)
            pltpu.async_copy(table_hbm.at[idx_v], rows_v, sem).wait()
            pltpu.sync_copy(rows_v, out_hbm.at[pl.ds(off, SC_CHUNK)])

    return gather(table, idx)


def _sc_scatter_rows(rows, pos12, n_out):
    n, d = rows.shape
    assert n % (SC_WORKERS * SC_CHUNK) == 0
    per_worker = n // SC_WORKERS
    idx = pos12.reshape(2, n)
    mesh = plsc.VectorSubcoreMesh(core_axis_name="c", subcore_axis_name="s",
                                  num_cores=SC_CORES, num_subcores=SC_SUBCORES)

    @functools.partial(
        pl.kernel, mesh=mesh, out_type=jax.ShapeDtypeStruct((n_out, d), rows.dtype),
        scratch_types=[pltpu.VMEM((2, SC_CHUNK), jnp.int32), pltpu.VMEM((SC_CHUNK, d), rows.dtype)])
    def scatter(rows_hbm, idx_hbm, out_hbm, idx_v, rows_v):
        base = (lax.axis_index("s") * SC_CORES + lax.axis_index("c")) * per_worker

        @pl.loop(0, per_worker // SC_CHUNK)
        def _(j):
            off = base + j * SC_CHUNK
            pltpu.sync_copy(rows_hbm.at[pl.ds(off, SC_CHUNK)], rows_v)
            for k in range(2):
                pltpu.sync_copy(idx_hbm.at[k, pl.ds(off, SC_CHUNK)], idx_v.at[k])
                pltpu.sync_copy(rows_v, out_hbm.at[idx_v.at[k]])

    return scatter(rows, idx)


def _moe_plan(route, counts):
    n = route.shape[1]
    count = counts[:, 0].astype(jnp.int32)
    group = (count + MOE_TM - 1) // MOE_TM * MOE_TM
    ends = jnp.cumsum(group)
    start = ends - group
    experts = jnp.arange(N_EXPERTS, dtype=jnp.int32)[:, None]

    def rows_of(e, rank):
        return jnp.sum(jnp.where(e.astype(jnp.int32)[None, :] == experts, start[:, None], 0), axis=0) \
            + rank.astype(jnp.int32)

    pos12 = jnp.concatenate([rows_of(route[0], route[2]), rows_of(route[1], route[3])])
    n_rows = 2 * n + N_EXPERTS * MOE_TM
    tile_start = jnp.arange(n_rows // MOE_TM, dtype=jnp.int32) * MOE_TM
    tile_expert = jnp.minimum(jnp.sum(tile_start[:, None] >= ends[None, :], axis=1), N_EXPERTS - 1)
    filled = (start + count)[tile_expert] - tile_start
    tile_rows = jnp.where(tile_start < ends[-1], jnp.clip(filled, 0, MOE_TM), 0).astype(jnp.int32)
    w12 = jnp.stack([route[4], route[5]], axis=1)
    return pos12, n_rows, tile_expert.astype(jnp.int32), tile_rows, w12


def _moe_group_kernel(te_ref, tr_ref, xs_ref, wg_ref, wu_ref, wd_ref, ys_ref):
    n_filled = tr_ref[pl.program_id(0)]
    used = n_filled > 0

    @pl.when(used)
    def _():
        row = lax.broadcasted_iota(jnp.int32, xs_ref.shape, 0)
        x = _unpack_bf16_pairs(jnp.where(row < n_filled, xs_ref[...], 0)).astype(BF16)
        g = _dot(x, wg_ref[0])
        u = _dot(x, wu_ref[0])
        ys_ref[...] = _pack_bf16_pairs(_dot((g * _sigmoid(g) * u).astype(BF16), wd_ref[0]))

    @pl.when(jnp.logical_not(used))
    def _():
        ys_ref[...] = jnp.zeros_like(ys_ref)


def _moe_grouped(xs, tile_expert, tile_rows, wg, wu, wd):
    p, dp = xs.shape
    _, d, dfe = wg.shape
    grid_spec = pltpu.PrefetchScalarGridSpec(
        num_scalar_prefetch=2,
        grid=(p // MOE_TM,),
        in_specs=[
            pl.BlockSpec((MOE_TM, dp), lambda t, te, tr: (t, 0)),
            pl.BlockSpec((1, d, dfe), lambda t, te, tr: (te[t], 0, 0)),
            pl.BlockSpec((1, d, dfe), lambda t, te, tr: (te[t], 0, 0)),
            pl.BlockSpec((1, dfe, d), lambda t, te, tr: (te[t], 0, 0)),
        ],
        out_specs=pl.BlockSpec((MOE_TM, dp), lambda t, te, tr: (t, 0)),
    )
    return pl.pallas_call(
        _moe_group_kernel,
        grid_spec=grid_spec,
        out_shape=jax.ShapeDtypeStruct((p, dp), jnp.int32),
        compiler_params=_params(("arbitrary",)),
        name="moe_grouped",
    )(tile_expert, tile_rows, xs, wg, wu, wd)


def _moe_combine_kernel(x1_ref, y1_ref, y2_ref, w_ref, o_ref):
    w = w_ref[...]
    y1 = _unpack_bf16_pairs(y1_ref[...])
    y2 = _unpack_bf16_pairs(y2_ref[...])
    o_ref[...] = x1_ref[...] + w[:, 0:1] * y1 + w[:, 1:2] * y2


def _moe_combine(x1, y, w12, tm):
    n, d = x1.shape
    nt = n // tm
    return pl.pallas_call(
        _moe_combine_kernel,
        grid=(nt,),
        in_specs=[
            pl.BlockSpec((tm, d), lambda i: (i, 0)),
            pl.BlockSpec((tm, d // 2), lambda i: (i, 0)),
            pl.BlockSpec((tm, d // 2), lambda i: (i + nt, 0)),
            pl.BlockSpec((tm, 2), lambda i: (i, 0)),
        ],
        out_specs=pl.BlockSpec((tm, d), lambda i: (i, 0)),
        out_shape=jax.ShapeDtypeStruct((n, d), F32),
        compiler_params=_params(("parallel",)),
        name="moe_combine",
    )(x1, y, y, w12)


def _moe(x1, h2, route, counts, wg, wu, wd):
    pos12, n_rows, tile_expert, tile_rows, w12 = _moe_plan(route, counts)
    xs = _sc_scatter_rows(h2, pos12, n_rows)
    ys = _moe_grouped(xs, tile_expert, tile_rows, wg, wu, wd)
    y = _sc_gather_rows(ys, pos12)
    return _moe_combine(x1, y, w12, _row_tile(x1.shape[0], TM_COMBINE))


def _row_tile(n, want):
    t = min(n, want)
    assert n % t == 0
    return t


def _prep_layer(l, norm_mix, w_in, w_decay_f, b_decay_f, w_decay_b, b_decay_b, gla_out_norm, q_norm,
                k_norm, rpb, w_o_gla, w_o_na, w_out, norm_ffn):
    w = w_in[l]
    c0 = 2 * QK_W + 2 * V_W
    c1 = c0 + 2 * GLA_RANK
    c2 = c1 + 3 * NA_W
    w_main = jnp.concatenate([w[:, :QK_W] * (GLA_DK ** -0.5), w[:, QK_W:c0], w[:, c2:], w[:, c1:c2]],
                             axis=1).astype(BF16)
    wz = w[:, c0:c1].astype(BF16)
    zero = jnp.zeros((GLA_RANK, QK_W), F32)
    wdec = jnp.concatenate([jnp.concatenate([w_decay_f[l], zero], axis=1),
                            jnp.concatenate([zero, w_decay_b[l]], axis=1)], axis=0)
    bdec = jnp.concatenate([b_decay_f[l], b_decay_b[l]])[None, :]
    qg = jnp.tile(q_norm[l] * (NA_DH ** -0.5), NA_HEADS)[None, :]
    kg = jnp.tile(k_norm[l], NA_HEADS)[None, :]
    return dict(
        g_mix=norm_mix[l][None, :], w_main=w_main, wz=wz, wd=wdec.astype(BF16), bdec=bdec,
        qg=qg, kg=kg, bias=_na_bias_table(rpb[l]),
        gog=gla_out_norm[l].reshape(1, V_W), wog=w_o_gla[l].astype(BF16),
        won=w_o_na[l].astype(BF16), wout=w_out[l].astype(BF16), g_ffn=norm_ffn[l][None, :])


def _router_operands(router):
    hi, lo = _split_bf16(router.T)
    return jnp.concatenate([hi, lo], axis=0), jnp.concatenate([hi, jnp.zeros_like(hi)], axis=0)


def _trunk(x, layers, dense, moe):
    b, l, d = x.shape
    n = b * l
    xf = x.reshape(n, d)
    head_mean = jnp.asarray(np.kron(np.eye(NA_HEADS), np.full((NA_DH, NA_DH), 1.0 / NA_DH)), BF16)
    tb = _row_tile(l, GLA_BLOCK)
    sb = GLA_SEQS if b % GLA_SEQS == 0 else 1
    for li, p in enumerate(layers):
        proj, la, lamin = _inproj(xf, p["g_mix"], p["w_main"], p["wz"], p["wd"], p["bdec"],
                           head_mean, p["qg"], p["kg"], _row_tile(n, TM_INPROJ))
        proj3 = proj.reshape(b, l, PROJ_W)
        la3 = la.reshape(b, l, 2 * QK_W)
        of = _gla(proj3, la3, lamin, False, tb, sb).reshape(n, V_W)
        ob = _gla(proj3, la3, lamin, True, tb, sb).reshape(n, V_W)
        na = _natten(proj3, p["bias"]).reshape(n, NA_W)
        if li % 2 == 0:
            wg, wu, wd = dense[li // 2]
            x1, h2 = _postmix(of, ob, proj, na, xf, p["gog"], p["wog"], p["won"], p["wout"],
                              p["g_ffn"], None, _row_tile(n, TM_POSTMIX))
            xf = _ffn(x1, h2, wg, wu, wd, _row_tile(n, TM_FFN), wg.shape[1] // FFN_SPLIT)
        else:
            router, wg, wu, wd = moe[li // 2]
            x1, h2, route, counts = _postmix(of, ob, proj, na, xf, p["gog"], p["wog"], p["won"], p["wout"],
                                             p["g_ffn"], router, _row_tile(n, TM_POSTMIX))
            xf = _moe(x1, h2, route, counts, wg, wu, wd)
    return xf.reshape(b, l, d)


def kernel(x_prompt, x_sample, norm_mix, w_in, w_decay_f, b_decay_f, w_decay_b, b_decay_b, gla_out_norm,
           q_norm, k_norm, rpb, w_o_gla, w_o_na, w_out, norm_ffn, ffn_w_gate, ffn_w_up, ffn_w_down,
           moe_router, moe_w_gate, moe_w_up, moe_w_down):
    depth = w_in.shape[0]
    layers = [_prep_layer(l, norm_mix, w_in, w_decay_f, b_decay_f, w_decay_b, b_decay_b, gla_out_norm,
                          q_norm, k_norm, rpb, w_o_gla, w_o_na, w_out, norm_ffn) for l in range(depth)]
    dense = [(ffn_w_gate[j].astype(BF16), ffn_w_up[j].astype(BF16), ffn_w_down[j].astype(BF16))
             for j in range(ffn_w_gate.shape[0])]
    moe = [(_router_operands(moe_router[j]), moe_w_gate[j].astype(BF16), moe_w_up[j].astype(BF16),
            moe_w_down[j].astype(BF16)) for j in range(moe_router.shape[0])]
    return (_trunk(x_prompt, layers, dense, moe), _trunk(x_sample, layers, dense, moe))
```

```python
import functools

import numpy as np
import jax
import jax.numpy as jnp
from jax import lax
from jax.experimental import pallas as pl
from jax.experimental.pallas import tpu as pltpu
from jax.experimental.pallas import tpu_sc as plsc

F32 = jnp.float32
BF16 = jnp.bfloat16

EPS = 1e-6
GRID_W = 64
GLA_HEADS = 4
GLA_DK = 128
GLA_DV = 256
GLA_CHUNK = 64
GLA_RANK = 16
GLA_TAU = 16.0
NA_HEADS = 8
NA_DH = 64
NA_KR = 8
NA_KC = 16
N_EXPERTS = 8
NEG = -1e30
LOG2_E = 1.4426950408889634

VMEM_LIMIT = 56 * 1024 * 1024
SUBLANES = 8
LANES = 128
D_MODEL = 1024

TM_INPROJ = 512
TM_POSTMIX = 512
TM_FFN = 1024
FFN_SPLIT = 2
TM_COMBINE = 1024
GLA_BLOCK = 512
GLA_SEQS = 2

QK_W = GLA_HEADS * GLA_DK
V_W = GLA_HEADS * GLA_DV
NA_W = NA_HEADS * NA_DH
PROJ_W = 2 * QK_W + 2 * V_W + 2 * D_MODEL + 3 * NA_W
CW = 512
NQ_CHUNK = (2 * QK_W + 2 * V_W + 2 * D_MODEL) // CW
NK_CHUNK = NQ_CHUNK + 1
INPROJ_SUB = 256


def _params(sem):
    return pltpu.CompilerParams(dimension_semantics=sem, vmem_limit_bytes=VMEM_LIMIT)


def _resident(a):
    nd = a.ndim
    return pl.BlockSpec(a.shape, lambda *_: (0,) * nd, pipeline_mode=pl.Buffered(1))


def _split_bf16(a):
    hi = a.astype(BF16)
    lo = (a - hi.astype(F32)).astype(BF16)
    return hi, lo


def _dot(a, b):
    return jnp.dot(a, b, preferred_element_type=F32)


def _dot_nt(a, b):
    return lax.dot_general(a, b, (((1,), (1,)), ((), ())), preferred_element_type=F32)


def _dot_tn(a, b):
    return lax.dot_general(a, b, (((0,), (0,)), ((), ())), preferred_element_type=F32)


def _sigmoid(x):
    return 1.0 / (1.0 + jnp.exp2(x * (-LOG2_E)))


def _pack_bf16_pairs(x):
    c = x.shape[1] // 2
    as_bits = lambda a: lax.bitcast_convert_type(a.astype(BF16).astype(F32), jnp.uint32)
    word = (as_bits(x[:, :c]) >> 16) | (as_bits(x[:, c:]) & jnp.uint32(0xFFFF0000))
    return lax.bitcast_convert_type(word, jnp.int32)


def _unpack_bf16_pairs(w):
    u = lax.bitcast_convert_type(w, jnp.uint32)
    lo = lax.bitcast_convert_type(u << 16, F32)
    hi = lax.bitcast_convert_type(u & jnp.uint32(0xFFFF0000), F32)
    return jnp.concatenate([lo, hi], axis=1)


def _rmsnorm(x, g):
    ms = jnp.mean(x * x, axis=-1, keepdims=True)
    return x * lax.rsqrt(ms + EPS) * g


def _inproj_kernel(x_ref, g_ref, w_ref, wz_ref, wd_ref, bdec_ref, hm_ref, qg_ref, kg_ref,
                   proj_ref, la_ref, lamin_ref):
    for t in range(x_ref.shape[0] // INPROJ_SUB):
        rs = slice(t * INPROJ_SUB, (t + 1) * INPROJ_SUB)
        h = _rmsnorm(x_ref[rs, :], g_ref[...]).astype(BF16)
        z = _dot(h, wz_ref[...])
        y = _dot(z.astype(BF16), wd_ref[...]) + bdec_ref[...]
        ls = jnp.minimum(y, 0.0) - jnp.log(1.0 + jnp.exp(-jnp.abs(y)))
        la = ls * (1.0 / GLA_TAU)
        la_ref[rs, :] = la
        lamin_ref[t] = jnp.broadcast_to(jnp.min(la, axis=0, keepdims=True), lamin_ref.shape[1:])
        for c in range(PROJ_W // CW):
            acc = _dot(h, w_ref[:, c * CW:(c + 1) * CW])
            if c in (NQ_CHUNK, NK_CHUNK):
                gain = qg_ref if c == NQ_CHUNK else kg_ref
                ms = _dot((acc * acc).astype(BF16), hm_ref[...])
                acc = acc * lax.rsqrt(ms + EPS) * gain[...]
            proj_ref[rs, c * CW:(c + 1) * CW] = acc.astype(BF16)


def _inproj(x, g, w_main, wz, wd, bdec, hm, qg, kg, tm):
    n, d = x.shape
    return pl.pallas_call(
        _inproj_kernel,
        grid=(n // tm,),
        in_specs=[
            pl.BlockSpec((tm, d), lambda i: (i, 0)),
            _resident(g), _resident(w_main), _resident(wz), _resident(wd),
            _resident(bdec), _resident(hm), _resident(qg), _resident(kg),
        ],
        out_specs=[
            pl.BlockSpec((tm, PROJ_W), lambda i: (i, 0)),
            pl.BlockSpec((tm, 2 * QK_W), lambda i: (i, 0)),
            pl.BlockSpec((tm // INPROJ_SUB, SUBLANES, 2 * QK_W), lambda i: (i, 0, 0)),
        ],
        out_shape=[
            jax.ShapeDtypeStruct((n, PROJ_W), BF16),
            jax.ShapeDtypeStruct((n, 2 * QK_W), F32),
            jax.ShapeDtypeStruct((n // INPROJ_SUB, SUBLANES, 2 * QK_W), F32),
        ],
        compiler_params=_params(("parallel",)),
        name="inproj",
    )(x, g, w_main, wz, wd, bdec, hm, qg, kg)


N_LEVELS = 6
INTER_BLK = N_LEVELS
STATE_BLK = N_LEVELS + 1


def _decay_sum_matrix(reverse):
    c = GLA_CHUNK
    m_all = np.zeros((N_LEVELS + 2, c, c), np.float32)
    for l in range(N_LEVELS):
        m = 1 << l
        for p in range(c):
            mid = (p // (2 * m)) * 2 * m + m
            if p >= mid:
                m_all[l, p, mid + 1:p + 1] = 1.0
            else:
                m_all[l, p, p + 1:mid + 1] = 1.0
    for p in range(c):
        m_all[INTER_BLK, p, :p + 1] = 1.0
        m_all[STATE_BLK, p, p + 1:] = 1.0
    if reverse:
        m_all = m_all[:, ::-1, ::-1]
    return np.ascontiguousarray(m_all).reshape((N_LEVELS + 2) * c, c)


def _level_matrix(reverse):
    i = np.arange(GLA_CHUNK)[:, None]
    j = np.arange(GLA_CHUNK)[None, :]
    x = i ^ j
    lvl = np.where(x > 0, np.floor(np.log2(np.maximum(x, 1))), N_LEVELS).astype(np.int32)
    valid = (i <= j) if reverse else (i >= j)
    return np.where(valid, lvl, N_LEVELS + 1).astype(np.int32)


GLA_FAST_CHUNK = 128
GLA_FAST_MAX_DECAY = 60.0


def _tri_matrix(reverse, c):
    i = np.arange(c)[:, None]
    j = np.arange(c)[None, :]
    return ((j >= i) if reverse else (j <= i)).astype(np.float32)


def _decay_columns(dec_row):
    t = jnp.broadcast_to(dec_row, (GLA_DK, GLA_DK)).T
    return jnp.concatenate([t, t], axis=1)


def _gla_kernel(q_ref, k_ref, v_ref, la_ref, lamin_ref, m_ref, lvl_ref, tri_ref, o_ref,
                s_ref, lr_ref, qh_ref, kh_ref, ks_ref, dec_ref, *, reverse, tb, sb):
    @pl.when(pl.program_id(1) == 0)
    def _():
        s_ref[...] = jnp.zeros_like(s_ref)

    fc = GLA_FAST_CHUNK
    n_fast = tb // fc
    total_row = 0 if reverse else fc - 1
    seqs = range(sb)

    def fast_rows(ci):
        return pl.ds(pl.multiple_of(ci * fc, fc), fc)

    def sc_lanes(h):
        return slice(h * 2 * GLA_DK, h * 2 * GLA_DK + GLA_DK)

    def qh_lanes(h):
        return slice(h * 2 * GLA_DK + GLA_DK, (h + 1) * 2 * GLA_DK)

    def fast_factors(ci, carry):
        rows = fast_rows(ci)
        for s in seqs:
            b = _dot(tri_ref[...], la_ref[s, rows, :].astype(BF16))
            dec = jnp.exp(b[total_row:total_row + 1, :])
            f_q = jnp.exp(b)
            qh = (q_ref[s, rows, :].astype(F32) * f_q).astype(BF16)
            kh = k_ref[s, rows, :].astype(F32) / f_q
            qh_ref[s, rows, :] = qh
            for h in range(GLA_HEADS):
                lr_ref[s, rows, qh_lanes(h)] = qh[:, h * GLA_DK:(h + 1) * GLA_DK]
            kh_ref[s, rows, :] = kh.astype(BF16)
            ks_ref[s, rows, :] = (kh * dec).astype(BF16)
            dec_ref[s, pl.ds(pl.multiple_of(ci * SUBLANES, SUBLANES), SUBLANES), :] = (
                jnp.broadcast_to(dec, (SUBLANES, QK_W)))
        return carry

    def fast_scores(ci, carry):
        rows = fast_rows(ci)
        i = lax.broadcasted_iota(jnp.int32, (fc, fc), 0)
        j = lax.broadcasted_iota(jnp.int32, (fc, fc), 1)
        causal = (i <= j) if reverse else (i >= j)
        for s in seqs:
            for h in range(GLA_HEADS):
                hs = slice(h * GLA_DK, (h + 1) * GLA_DK)
                sc = _dot_nt(qh_ref[s, rows, hs], kh_ref[s, rows, hs])
                lr_ref[s, rows, sc_lanes(h)] = jnp.where(causal, sc, 0.0).astype(BF16)
        return carry

    def fast_state(ci, carry):
        cc = (n_fast - 1 - ci) if reverse else ci
        rows = fast_rows(cc)
        heads = [(s, h) for s in seqs for h in range(GLA_HEADS)]
        vals = {(s, h): v_ref[s, rows, h * GLA_DV:(h + 1) * GLA_DV] for s, h in heads}
        kv = {(s, h): _dot_tn(ks_ref[s, rows, h * GLA_DK:(h + 1) * GLA_DK], vals[s, h]) for s, h in heads}
        for s, h in heads:
            dec = dec_ref[s, pl.ds(pl.multiple_of(cc * SUBLANES, SUBLANES), SUBLANES), :][
                :1, h * GLA_DK:(h + 1) * GLA_DK]
            st = s_ref[s, h]
            o = _dot(lr_ref[s, rows, h * 2 * GLA_DK:(h + 1) * 2 * GLA_DK],
                     jnp.concatenate([vals[s, h], st.astype(BF16)], axis=0))
            o_ref[s, rows, h * GLA_DV:(h + 1) * GLA_DV] = o.astype(o_ref.dtype)
            s_ref[s, h] = st * _decay_columns(dec) + kv[s, h]
        return carry

    def robust_chunk(ci, carry):
        c = GLA_CHUNK
        n_chunks = tb // c
        lvl = lvl_ref[...]
        row = lax.broadcasted_iota(jnp.int32, (c, GLA_DK), 0)
        total_row = 0 if reverse else c - 1
        cc = (n_chunks - 1 - ci) if reverse else ci
        rows = pl.ds(pl.multiple_of(cc * c, c), c)
        for s in seqs:
            la_hi, la_lo = _split_bf16(la_ref[s, rows, :])
            fac = jnp.exp(_dot(m_ref[...], la_hi) + _dot(m_ref[...], la_lo))
            for h in range(GLA_HEADS):
                hs = slice(h * GLA_DK, (h + 1) * GLA_DK)
                vs = slice(h * GLA_DV, (h + 1) * GLA_DV)
                q = q_ref[s, rows, hs].astype(F32)
                k = k_ref[s, rows, hs].astype(F32)
                v = v_ref[s, rows, vs]
                scores = jnp.zeros((c, c), F32)
                for l in range(N_LEVELS):
                    is_query = ((row >> l) & 1) == (0 if reverse else 1)
                    f_l = fac[l * c:(l + 1) * c, hs]
                    g_l = (jnp.where(is_query, q, k) * f_l).astype(BF16)
                    scores = jnp.where(lvl == l, _dot_nt(g_l, g_l), scores)
                scores = jnp.where(lvl == N_LEVELS, _dot_nt(q.astype(BF16), k.astype(BF16)), scores)
                f_in = fac[INTER_BLK * c:(INTER_BLK + 1) * c, hs]
                f_st = fac[STATE_BLK * c:(STATE_BLK + 1) * c, hs]
                st = s_ref[s, h]
                inter = _dot((q * f_in).astype(BF16), st.astype(BF16))
                intra = _dot(scores.astype(BF16), v)
                o_ref[s, rows, vs] = (inter + intra).astype(o_ref.dtype)
                dec = f_in[total_row:total_row + 1, :]
                s_ref[s, h] = st * _decay_columns(dec) + _dot_tn((k * f_st).astype(BF16), v)
        return carry

    bounded = jnp.min(lamin_ref[...]) >= -GLA_FAST_MAX_DECAY / GLA_FAST_CHUNK

    @pl.when(bounded)
    def _():
        lax.fori_loop(0, n_fast, fast_factors, 0, unroll=4)
        lax.fori_loop(0, n_fast, fast_scores, 0, unroll=4)
        lax.fori_loop(0, n_fast, fast_state, 0, unroll=4)

    @pl.when(jnp.logical_not(bounded))
    def _():
        lax.fori_loop(0, tb // GLA_CHUNK, robust_chunk, 0)


def _gla(proj3, la3, lamin, reverse, tb, sb):
    b, l, _ = proj3.shape
    nb = l // tb
    lamin4 = lamin.reshape(b, -1, SUBLANES, 2 * QK_W)
    tiles = lamin4.shape[1] // nb
    blk = (lambda i: nb - 1 - i) if reverse else (lambda i: i)
    m_all = jnp.asarray(_decay_sum_matrix(reverse), BF16)
    lvl = jnp.asarray(_level_matrix(reverse))
    tri = jnp.asarray(_tri_matrix(reverse, GLA_FAST_CHUNK), BF16)
    kern = functools.partial(_gla_kernel, reverse=reverse, tb=tb, sb=sb)
    decay_col = 1 if reverse else 0
    return pl.pallas_call(
        kern,
        grid=(b // sb, nb),
        in_specs=[
            pl.BlockSpec((sb, tb, QK_W), lambda s, i: (s, blk(i), 0)),
            pl.BlockSpec((sb, tb, QK_W), lambda s, i: (s, blk(i), 1)),
            pl.BlockSpec((sb, tb, V_W), lambda s, i: (s, blk(i), 1)),
            pl.BlockSpec((sb, tb, QK_W), lambda s, i: (s, blk(i), decay_col)),
            pl.BlockSpec((sb, tiles, SUBLANES, QK_W), lambda s, i: (s, blk(i), 0, decay_col)),
            _resident(m_all), _resident(lvl), _resident(tri),
        ],
        out_specs=pl.BlockSpec((sb, tb, V_W), lambda s, i: (s, blk(i), 0)),
        out_shape=jax.ShapeDtypeStruct((b, l, V_W), BF16),
        scratch_shapes=[
            pltpu.VMEM((sb, GLA_HEADS, GLA_DK, GLA_DV), F32),
            pltpu.VMEM((sb, tb, 2 * QK_W), BF16),
            pltpu.VMEM((sb, tb, QK_W), BF16),
            pltpu.VMEM((sb, tb, QK_W), BF16),
            pltpu.VMEM((sb, tb, QK_W), BF16),
            pltpu.VMEM((sb, tb // GLA_FAST_CHUNK * SUBLANES, QK_W), F32),
        ],
        compiler_params=_params(("parallel", "arbitrary")),
        name="gla_bwd" if reverse else "gla_fwd",
    )(proj3, proj3, proj3, la3, lamin4, m_all, lvl, tri)


NA_GROUP = 8
NA_BLK = NA_GROUP * GRID_W
NA_BAND = NA_KR * GRID_W
NA_WINDOW = 3 * NA_BLK


def _na_bias_table(rpb):
    qc = np.arange(GRID_W)[:, None]
    kc = np.arange(GRID_W)[None, :]
    col_start = np.clip(qc - NA_KC // 2, 0, GRID_W - NA_KC)
    valid = (kc >= col_start) & (kc < col_start + NA_KC)
    dc = np.clip(kc - qc, -(NA_KC - 1), NA_KC - 1) + NA_KC - 1
    onehot = ((dc[None] == np.arange(2 * NA_KC - 1)[:, None, None]) & valid[None]).astype(np.float32)
    mask = np.where(valid, 0.0, NEG).astype(np.float32)
    rows = jnp.stack([rpb[:, NA_KR - 1 - d:2 * NA_KR - 1 - d, :] for d in range(NA_KR)])
    t = jnp.einsum('dhkc,cqj->dhqkj', rows, jnp.asarray(onehot), precision=lax.Precision.HIGHEST)
    t = t + jnp.asarray(mask)[None, None, :, None, :]
    return t.reshape(NA_KR, NA_HEADS // 2, 2 * GRID_W, NA_BAND)


def _na_window_start(g, l):
    return jnp.clip(g - 1, 0, (l - min(NA_WINDOW, l)) // NA_BLK) * NA_BLK


def _na_kernel(q_ref, k_ref, v_ref, bias_ref, o_ref, *, rows):
    g = pl.program_id(1)
    start = _na_window_start(g, rows * GRID_W)
    lane = lax.broadcasted_iota(jnp.int32, (GRID_W, 2 * NA_DH), 1)
    first = lane < NA_DH

    pairs = [slice(p * 2 * NA_DH, (p + 1) * 2 * NA_DH) for p in range(NA_HEADS // 2)]

    def one_row(rl, carry):
        r = g * NA_GROUP + rl
        rs = jnp.clip(r - NA_KR // 2, 0, rows - NA_KR)
        d = r - rs
        off = pl.multiple_of(rs * GRID_W - start, GRID_W)
        qrows = pl.ds(pl.multiple_of(rl * GRID_W, GRID_W), GRID_W)
        scores = []
        for p, ps in enumerate(pairs):
            qp = q_ref[0, qrows, ps]
            zero = jnp.zeros_like(qp)
            qs = jnp.concatenate([jnp.where(first, qp, zero), jnp.where(first, zero, qp)], axis=0)
            kb = k_ref[0, pl.ds(off, NA_BAND), ps]
            scores.append(_dot_nt(qs, kb) + bias_ref[d, p])
        probs = []
        for s in scores:
            e = jnp.exp(s - jnp.max(s, axis=-1, keepdims=True))
            probs.append((e.astype(BF16), jnp.sum(e, axis=-1, keepdims=True)))
        for (e, den), ps in zip(probs, pairs):
            vb = v_ref[0, pl.ds(off, NA_BAND), ps]
            pv = _dot(e, vb) / den
            o_ref[0, qrows, ps] = jnp.where(first, pv[:GRID_W], pv[GRID_W:]).astype(o_ref.dtype)
        return carry

    lax.fori_loop(0, NA_GROUP, one_row, 0, unroll=NA_GROUP)


def _natten(proj3, bias):
    b, l, _ = proj3.shape
    rows = l // GRID_W
    ng = rows // NA_GROUP
    qcol, kcol, vcol = NQ_CHUNK, NQ_CHUNK + 1, NQ_CHUNK + 2
    window = lambda col: pl.BlockSpec(
        (pl.Element(1), pl.Element(min(NA_WINDOW, l)), pl.Element(NA_W)),
        lambda s, i: (s, _na_window_start(i, l), col * NA_W))
    return pl.pallas_call(
        functools.partial(_na_kernel, rows=rows),
        grid=(b, ng),
        in_specs=[
            pl.BlockSpec((1, NA_BLK, NA_W), lambda s, i: (s, i, qcol)),
            window(kcol), window(vcol),
            _resident(bias),
        ],
        out_specs=pl.BlockSpec((1, NA_BLK, NA_W), lambda s, i: (s, i, 0)),
        out_shape=jax.ShapeDtypeStruct((b, l, NA_W), BF16),
        compiler_params=_params(("parallel", "parallel")),
        name="natten",
    )(proj3, proj3, proj3, bias)


def _postmix_kernel(*refs, with_router):
    if with_router:
        (of_ref, ob_ref, r_ref, ga_ref, gb_ref, na_ref, x_ref, gog_ref, wog_ref, won_ref, wout_ref,
         gf_ref, rh_ref, rl_ref, before_ref, x1_ref, h2_ref, route_ref, count_ref) = refs
    else:
        (of_ref, ob_ref, r_ref, ga_ref, gb_ref, na_ref, x_ref, gog_ref, wog_ref, won_ref, wout_ref,
         gf_ref, x1_ref, h2_ref) = refs
    o = of_ref[...].astype(F32) + ob_ref[...].astype(F32)
    parts = []
    for h in range(GLA_HEADS):
        seg = o[:, h * GLA_DV:(h + 1) * GLA_DV]
        ms = jnp.mean(seg * seg, axis=-1, keepdims=True)
        parts.append(seg * lax.rsqrt(ms + EPS))
    r = r_ref[...].astype(F32)
    on = jnp.concatenate(parts, axis=-1) * gog_ref[...] * (r * _sigmoid(r))
    ya = _dot(on.astype(BF16), wog_ref[...])
    yb = _dot(na_ref[...], won_ref[...])
    merged = _sigmoid(ga_ref[...].astype(F32)) * ya + _sigmoid(gb_ref[...].astype(F32)) * yb
    x1 = x_ref[...] + _dot(merged.astype(BF16), wout_ref[...])
    x1_ref[...] = x1
    h2 = _rmsnorm(x1, gf_ref[...])
    h2_ref[...] = _pack_bf16_pairs(h2) if with_router else h2.astype(BF16)
    if with_router:
        hh, hl = _split_bf16(h2)
        lg2 = _dot_nt(rh_ref[...], hh) + _dot_nt(rl_ref[...], hl)
        lg = lg2[:N_EXPERTS] + lg2[N_EXPERTS:]
        row = lax.broadcasted_iota(jnp.int32, lg.shape, 0).astype(F32)
        m1 = jnp.max(lg, axis=0, keepdims=True)
        i1 = jnp.min(jnp.where(lg == m1, row, float(N_EXPERTS)), axis=0, keepdims=True)
        lg_rest = jnp.where(row == i1, -jnp.inf, lg)
        m2 = jnp.max(lg_rest, axis=0, keepdims=True)
        i2 = jnp.min(jnp.where(lg_rest == m2, row, float(N_EXPERTS)), axis=0, keepdims=True)
        t = jnp.exp(m2 - m1)
        w1 = 1.0 / (1.0 + t)
        @pl.when(pl.program_id(0) == 0)
        def _():
            count_ref[...] = jnp.zeros_like(count_ref)

        picked = (row == i1) | (row == i2)
        sel = jnp.where(picked, 1.0, 0.0)
        sel16 = jnp.concatenate([sel, jnp.zeros_like(sel)], axis=0).astype(BF16)
        rank = _dot(sel16, before_ref[...])[:N_EXPERTS] + count_ref[:, :1]
        count_ref[...] = count_ref[...] + jnp.sum(sel, axis=1, keepdims=True)
        pick = lambda i, a: jnp.sum(jnp.where(row == i, a, 0.0), axis=0, keepdims=True)
        route_ref[...] = jnp.concatenate(
            [i1, i2, pick(i1, rank), pick(i2, rank), w1, t * w1, jnp.zeros_like(w1), jnp.zeros_like(w1)],
            axis=0)


def _postmix(of, ob, proj, na, x, gog, wog, won, wout, gf, router, tm):
    n, d = x.shape
    with_router = router is not None
    in_specs = [
        pl.BlockSpec((tm, V_W), lambda i: (i, 0)),
        pl.BlockSpec((tm, V_W), lambda i: (i, 0)),
        pl.BlockSpec((tm, D_MODEL), lambda i: (i, 2)),
        pl.BlockSpec((tm, D_MODEL), lambda i: (i, 3)),
        pl.BlockSpec((tm, D_MODEL), lambda i: (i, 4)),
        pl.BlockSpec((tm, NA_W), lambda i: (i, 0)),
        pl.BlockSpec((tm, d), lambda i: (i, 0)),
        _resident(gog), _resident(wog), _resident(won), _resident(wout), _resident(gf),
    ]
    args = [of, ob, proj, proj, proj, na, x, gog, wog, won, wout, gf]
    h2_w = d // 2 if with_router else d
    out_specs = [pl.BlockSpec((tm, d), lambda i: (i, 0)), pl.BlockSpec((tm, h2_w), lambda i: (i, 0))]
    out_shape = [jax.ShapeDtypeStruct((n, d), F32),
                 jax.ShapeDtypeStruct((n, h2_w), jnp.int32 if with_router else BF16)]
    if with_router:
        before = jnp.asarray(np.triu(np.ones((tm, tm), np.float32), 1), BF16)
        in_specs += [_resident(router[0]), _resident(router[1]), _resident(before)]
        args += list(router) + [before]
        out_specs.append(pl.BlockSpec((N_EXPERTS, tm), lambda i: (0, i)))
        out_shape.append(jax.ShapeDtypeStruct((N_EXPERTS, n), F32))
        out_specs.append(pl.BlockSpec((N_EXPERTS, LANES), lambda i: (0, 0)))
        out_shape.append(jax.ShapeDtypeStruct((N_EXPERTS, LANES), F32))
    return pl.pallas_call(
        functools.partial(_postmix_kernel, with_router=with_router),
        grid=(n // tm,),
        in_specs=in_specs,
        out_specs=out_specs,
        out_shape=out_shape,
        compiler_params=_params(("arbitrary" if with_router else "parallel",)),
        name="postmix_router" if with_router else "postmix",
    )(*args)


def _ffn_kernel(x1_ref, h_ref, wg_ref, wu_ref, wd_ref, o_ref, acc_ref):
    j = pl.program_id(1)

    @pl.when(j == 0)
    def _():
        acc_ref[...] = jnp.zeros_like(acc_ref)

    h = h_ref[...]
    g = _dot(h, wg_ref[...])
    u = _dot(h, wu_ref[...])
    a = (g * _sigmoid(g) * u).astype(BF16)
    acc_ref[...] += _dot(a, wd_ref[...])

    @pl.when(j == pl.num_programs(1) - 1)
    def _():
        o_ref[...] = x1_ref[...] + acc_ref[...]


def _ffn(x1, h2, wg, wu, wd, tm, tf):
    n, d = x1.shape
    dff = wg.shape[1]
    return pl.pallas_call(
        _ffn_kernel,
        grid=(n // tm, dff // tf),
        in_specs=[
            pl.BlockSpec((tm, d), lambda i, j: (i, 0)),
            pl.BlockSpec((tm, d), lambda i, j: (i, 0)),
            pl.BlockSpec((d, tf), lambda i, j: (0, j)),
            pl.BlockSpec((d, tf), lambda i, j: (0, j)),
            pl.BlockSpec((tf, d), lambda i, j: (j, 0)),
        ],
        out_specs=pl.BlockSpec((tm, d), lambda i, j: (i, 0)),
        out_shape=jax.ShapeDtypeStruct((n, d), F32),
        scratch_shapes=[pltpu.VMEM((tm, d), F32)],
        compiler_params=_params(("parallel", "arbitrary")),
        name="ffn",
    )(x1, h2, wg, wu, wd)


SC_CORES = 2
SC_SUBCORES = 16
SC_WORKERS = SC_CORES * SC_SUBCORES
SC_CHUNK = 32
MOE_TM = 512


def _sc_gather_rows(table, idx):
    nrow, d = idx.shape[0], table.shape[1]
    assert nrow % (SC_WORKERS * SC_CHUNK) == 0
    per_worker = nrow // SC_WORKERS
    mesh = plsc.VectorSubcoreMesh(core_axis_name="c", subcore_axis_name="s",
                                  num_cores=SC_CORES, num_subcores=SC_SUBCORES)

    @functools.partial(
        pl.kernel, mesh=mesh, out_type=jax.ShapeDtypeStruct((nrow, d), table.dtype),
        scratch_types=[pltpu.VMEM((SC_CHUNK,), jnp.int32), pltpu.VMEM((SC_CHUNK, d), table.dtype),
                       pltpu.SemaphoreType.DMA])
    def gather(table_hbm, idx_hbm, out_hbm, idx_v, rows_v, sem):
        base = (lax.axis_index("s") * SC_CORES + lax.axis_index("c")) * per_worker

        @pl.loop(0, per_worker // SC_CHUNK)
        def _(j):
            off = base + j * SC_CHUNK
            pltpu.sync_copy(idx_hbm.at[pl.ds(off, SC_CHUNK)], idx_v)
            pltpu.async_copy(table_hbm.at[idx_v], rows_v, sem).wait()
            pltpu.sync_copy(rows_v, out_hbm.at[pl.ds(off, SC_CHUNK)])

    return gather(table, idx)


def _sc_scatter_rows(rows, pos12, n_out):
    n, d = rows.shape
    assert n % (SC_WORKERS * SC_CHUNK) == 0
    per_worker = n // SC_WORKERS
    idx = pos12.reshape(2, n)
    mesh = plsc.VectorSubcoreMesh(core_axis_name="c", subcore_axis_name="s",
                                  num_cores=SC_CORES, num_subcores=SC_SUBCORES)

    @functools.partial(
        pl.kernel, mesh=mesh, out_type=jax.ShapeDtypeStruct((n_out, d), rows.dtype),
        scratch_types=[pltpu.VMEM((2, SC_CHUNK), jnp.int32), pltpu.VMEM((SC_CHUNK, d), rows.dtype)])
    def scatter(rows_hbm, idx_hbm, out_hbm, idx_v, rows_v):
        base = (lax.axis_index("s") * SC_CORES + lax.axis_index("c")) * per_worker

        @pl.loop(0, per_worker // SC_CHUNK)
        def _(j):
            off = base + j * SC_CHUNK
            pltpu.sync_copy(rows_hbm.at[pl.ds(off, SC_CHUNK)], rows_v)
            for k in range(2):
                pltpu.sync_copy(idx_hbm.at[k, pl.ds(off, SC_CHUNK)], idx_v.at[k])
                pltpu.sync_copy(rows_v, out_hbm.at[idx_v.at[k]])

    return scatter(rows, idx)


def _moe_plan(route, counts):
    n = route.shape[1]
    count = counts[:, 0].astype(jnp.int32)
    group = (count + MOE_TM - 1) // MOE_TM * MOE_TM
    ends = jnp.cumsum(group)
    start = ends - group
    experts = jnp.arange(N_EXPERTS, dtype=jnp.int32)[:, None]

    def rows_of(e, rank):
        return jnp.sum(jnp.where(e.astype(jnp.int32)[None, :] == experts, start[:, None], 0), axis=0) \
            + rank.astype(jnp.int32)

    pos12 = jnp.concatenate([rows_of(route[0], route[2]), rows_of(route[1], route[3])])
    n_rows = 2 * n + N_EXPERTS * MOE_TM
    tile_start = jnp.arange(n_rows // MOE_TM, dtype=jnp.int32) * MOE_TM
    tile_expert = jnp.minimum(jnp.sum(tile_start[:, None] >= ends[None, :], axis=1), N_EXPERTS - 1)
    filled = (start + count)[tile_expert] - tile_start
    tile_rows = jnp.where(tile_start < ends[-1], jnp.clip(filled, 0, MOE_TM), 0).astype(jnp.int32)
    w12 = jnp.stack([route[4], route[5]], axis=1)
    return pos12, n_rows, tile_expert.astype(jnp.int32), tile_rows, w12


def _moe_group_kernel(te_ref, tr_ref, xs_ref, wg_ref, wu_ref, wd_ref, ys_ref):
    n_filled = tr_ref[pl.program_id(0)]
    used = n_filled > 0

    @pl.when(used)
    def _():
        row = lax.broadcasted_iota(jnp.int32, xs_ref.shape, 0)
        x = _unpack_bf16_pairs(jnp.where(row < n_filled, xs_ref[...], 0)).astype(BF16)
        g = _dot(x, wg_ref[0])
        u = _dot(x, wu_ref[0])
        ys_ref[...] = _pack_bf16_pairs(_dot((g * _sigmoid(g) * u).astype(BF16), wd_ref[0]))

    @pl.when(jnp.logical_not(used))
    def _():
        ys_ref[...] = jnp.zeros_like(ys_ref)


def _moe_grouped(xs, tile_expert, tile_rows, wg, wu, wd):
    p, dp = xs.shape
    _, d, dfe = wg.shape
    grid_spec = pltpu.PrefetchScalarGridSpec(
        num_scalar_prefetch=2,
        grid=(p // MOE_TM,),
        in_specs=[
            pl.BlockSpec((MOE_TM, dp), lambda t, te, tr: (t, 0)),
            pl.BlockSpec((1, d, dfe), lambda t, te, tr: (te[t], 0, 0)),
            pl.BlockSpec((1, d, dfe), lambda t, te, tr: (te[t], 0, 0)),
            pl.BlockSpec((1, dfe, d), lambda t, te, tr: (te[t], 0, 0)),
        ],
        out_specs=pl.BlockSpec((MOE_TM, dp), lambda t, te, tr: (t, 0)),
    )
    return pl.pallas_call(
        _moe_group_kernel,
        grid_spec=grid_spec,
        out_shape=jax.ShapeDtypeStruct((p, dp), jnp.int32),
        compiler_params=_params(("arbitrary",)),
        name="moe_grouped",
    )(tile_expert, tile_rows, xs, wg, wu, wd)


def _moe_combine_kernel(x1_ref, y1_ref, y2_ref, w_ref, o_ref):
    w = w_ref[...]
    y1 = _unpack_bf16_pairs(y1_ref[...])
    y2 = _unpack_bf16_pairs(y2_ref[...])
    o_ref[...] = x1_ref[...] + w[:, 0:1] * y1 + w[:, 1:2] * y2


def _moe_combine(x1, y, w12, tm):
    n, d = x1.shape
    nt = n // tm
    return pl.pallas_call(
        _moe_combine_kernel,
        grid=(nt,),
        in_specs=[
            pl.BlockSpec((tm, d), lambda i: (i, 0)),
            pl.BlockSpec((tm, d // 2), lambda i: (i, 0)),
            pl.BlockSpec((tm, d // 2), lambda i: (i + nt, 0)),
            pl.BlockSpec((tm, 2), lambda i: (i, 0)),
        ],
        out_specs=pl.BlockSpec((tm, d), lambda i: (i, 0)),
        out_shape=jax.ShapeDtypeStruct((n, d), F32),
        compiler_params=_params(("parallel",)),
        name="moe_combine",
    )(x1, y, y, w12)


def _moe(x1, h2, route, counts, wg, wu, wd):
    pos12, n_rows, tile_expert, tile_rows, w12 = _moe_plan(route, counts)
    xs = _sc_scatter_rows(h2, pos12, n_rows)
    ys = _moe_grouped(xs, tile_expert, tile_rows, wg, wu, wd)
    y = _sc_gather_rows(ys, pos12)
    return _moe_combine(x1, y, w12, _row_tile(x1.shape[0], TM_COMBINE))


def _row_tile(n, want):
    t = min(n, want)
    assert n % t == 0
    return t


def _prep_layer(l, norm_mix, w_in, w_decay_f, b_decay_f, w_decay_b, b_decay_b, gla_out_norm, q_norm,
                k_norm, rpb, w_o_gla, w_o_na, w_out, norm_ffn):
    w = w_in[l]
    c0 = 2 * QK_W + 2 * V_W
    c1 = c0 + 2 * GLA_RANK
    c2 = c1 + 3 * NA_W
    w_main = jnp.concatenate([w[:, :QK_W] * (GLA_DK ** -0.5), w[:, QK_W:c0], w[:, c2:], w[:, c1:c2]],
                             axis=1).astype(BF16)
    wz = w[:, c0:c1].astype(BF16)
    zero = jnp.zeros((GLA_RANK, QK_W), F32)
    wdec = jnp.concatenate([jnp.concatenate([w_decay_f[l], zero], axis=1),
                            jnp.concatenate([zero, w_decay_b[l]], axis=1)], axis=0)
    bdec = jnp.concatenate([b_decay_f[l], b_decay_b[l]])[None, :]
    qg = jnp.tile(q_norm[l] * (NA_DH ** -0.5), NA_HEADS)[None, :]
    kg = jnp.tile(k_norm[l], NA_HEADS)[None, :]
    return dict(
        g_mix=norm_mix[l][None, :], w_main=w_main, wz=wz, wd=wdec.astype(BF16), bdec=bdec,
        qg=qg, kg=kg, bias=_na_bias_table(rpb[l]),
        gog=gla_out_norm[l].reshape(1, V_W), wog=w_o_gla[l].astype(BF16),
        won=w_o_na[l].astype(BF16), wout=w_out[l].astype(BF16), g_ffn=norm_ffn[l][None, :])


def _router_operands(router):
    hi, lo = _split_bf16(router.T)
    return jnp.concatenate([hi, lo], axis=0), jnp.concatenate([hi, jnp.zeros_like(hi)], axis=0)


def _trunk(x, layers, dense, moe):
    b, l, d = x.shape
    n = b * l
    xf = x.reshape(n, d)
    head_mean = jnp.asarray(np.kron(np.eye(NA_HEADS), np.full((NA_DH, NA_DH), 1.0 / NA_DH)), BF16)
    tb = _row_tile(l, GLA_BLOCK)
    sb = GLA_SEQS if b % GLA_SEQS == 0 else 1
    for li, p in enumerate(layers):
        proj, la, lamin = _inproj(xf, p["g_mix"], p["w_main"], p["wz"], p["wd"], p["bdec"],
                           head_mean, p["qg"], p["kg"], _row_tile(n, TM_INPROJ))
        proj3 = proj.reshape(b, l, PROJ_W)
        la3 = la.reshape(b, l, 2 * QK_W)
        of = _gla(proj3, la3, lamin, False, tb, sb).reshape(n, V_W)
        ob = _gla(proj3, la3, lamin, True, tb, sb).reshape(n, V_W)
        na = _natten(proj3, p["bias"]).reshape(n, NA_W)
        if li % 2 == 0:
            wg, wu, wd = dense[li // 2]
            x1, h2 = _postmix(of, ob, proj, na, xf, p["gog"], p["wog"], p["won"], p["wout"],
                              p["g_ffn"], None, _row_tile(n, TM_POSTMIX))
            xf = _ffn(x1, h2, wg, wu, wd, _row_tile(n, TM_FFN), wg.shape[1] // FFN_SPLIT)
        else:
            router, wg, wu, wd = moe[li // 2]
            x1, h2, route, counts = _postmix(of, ob, proj, na, xf, p["gog"], p["wog"], p["won"], p["wout"],
                                             p["g_ffn"], router, _row_tile(n, TM_POSTMIX))
            xf = _moe(x1, h2, route, counts, wg, wu, wd)
    return xf.reshape(b, l, d)


def kernel(x_prompt, x_sample, norm_mix, w_in, w_decay_f, b_decay_f, w_decay_b, b_decay_b, gla_out_norm,
           q_norm, k_norm, rpb, w_o_gla, w_o_na, w_out, norm_ffn, ffn_w_gate, ffn_w_up, ffn_w_down,
           moe_router, moe_w_gate, moe_w_up, moe_w_down):
    depth = w_in.shape[0]
    layers = [_prep_layer(l, norm_mix, w_in, w_decay_f, b_decay_f, w_decay_b, b_decay_b, gla_out_norm,
                          q_norm, k_norm, rpb, w_o_gla, w_o_na, w_out, norm_ffn) for l in range(depth)]
    dense = [(ffn_w_gate[j].astype(BF16), ffn_w_up[j].astype(BF16), ffn_w_down[j].astype(BF16))
             for j in range(ffn_w_gate.shape[0])]
    moe = [(_router_operands(moe_router[j]), moe_w_gate[j].astype(BF16), moe_w_up[j].astype(BF16),
            moe_w_down[j].astype(BF16)) for j in range(moe_router.shape[0])]
    return (_trunk(x_prompt, layers, dense, moe), _trunk(x_sample, layers, dense, moe))
```
